```python
import math
import jax, jax.numpy as jnp
from jax import lax
import numpy as np

D_MODEL = 1024
BATCH = 4
SEQ = 8192
DEPTH = 1
DEC_BATCH = 128
DEC_SEQ = 1
PAST_LEN = 16384
PAGE_SIZE = 128

SSD_EXPAND = 2
D_INNER = SSD_EXPAND * D_MODEL
SSD_HEAD_DIM = 64
SSD_HEADS = D_INNER // SSD_HEAD_DIM
SSD_GROUPS = 4
SSD_HPG = SSD_HEADS // SSD_GROUPS
SSD_STATE = 128
CONV_K = 4
CONV_DIM = D_INNER + 2 * SSD_GROUPS * SSD_STATE
SSD_CHUNK = 128
HEAD_DIM = 64
N_HEADS = D_MODEL // HEAD_DIM
N_KV = 4
GQA = N_HEADS // N_KV
ATTN_WIDTH = N_HEADS * HEAD_DIM
KV_WIDTH = N_KV * HEAD_DIM
WINDOW = 128
MEM_LEN = 256
C_HEADS = 4
C_HEAD_DIM = 128
C_WIDTH = C_HEADS * C_HEAD_DIM
N_GROUPS = 4
EXP_PER_GROUP = 8
N_EXPERTS = N_GROUPS * EXP_PER_GROUP
TOP_K = 2
D_EXPERT = D_MODEL // 2
MOE_BLOCK = 128
IN_WIDTHS = (D_INNER, CONV_DIM, SSD_HEADS, ATTN_WIDTH, KV_WIDTH, KV_WIDTH, D_MODEL, D_MODEL)
IN_COLS = D_INNER + CONV_DIM + SSD_HEADS + ATTN_WIDTH + 2 * KV_WIDTH + 2 * D_MODEL
EPS = 1e-6
NEG_INF = -1e30

kernel_name = 'hybrid_ssd_swa_hmoe_step'


def rms_norm(x, g):
    xf = x.astype(jnp.float32)
    y = xf * lax.rsqrt(jnp.mean(xf * xf, axis=-1, keepdims=True) + EPS)
    return (y * g.astype(jnp.float32)).astype(x.dtype)


def alibi_slopes():
    return jnp.exp2(-8.0 * jnp.arange(1, N_HEADS + 1, dtype=jnp.float32) / N_HEADS)


def causal_conv(u, buf, w, b):
    up = jnp.concatenate([buf.astype(u.dtype), u], axis=1)
    L = u.shape[1]
    out = b + sum(w[k] * up[:, k:k + L] for k in range(CONV_K))
    return jax.nn.silu(out), up[:, L:]


def ssd_scan(x, dt, a, bm, cm, h0):
    b, L = x.shape[:2]
    lc = SSD_CHUNK if L % SSD_CHUNK == 0 else L
    nc = L // lc

    def to_chunks(t):
        return jnp.moveaxis(t.reshape((b, nc, lc) + t.shape[2:]), 1, 0)

    xc = to_chunks(x.reshape(b, L, SSD_GROUPS, SSD_HPG, SSD_HEAD_DIM))
    dtc = to_chunks(dt.reshape(b, L, SSD_GROUPS, SSD_HPG))
    bc, cc = to_chunks(bm), to_chunks(cm)
    a_g = a.reshape(SSD_GROUPS, SSD_HPG)
    causal = jnp.tril(jnp.ones((lc, lc), bool))[None, :, :, None, None]

    def step(h, inp):
        xk, dk, bk, ck = inp
        acum = jnp.cumsum(dk * a_g, axis=1)
        seg = acum[:, :, None] - acum[:, None, :]
        decay = jnp.exp(jnp.where(causal, seg, NEG_INF))
        cb = jnp.einsum('blgn,bsgn->blsg', ck, bk)
        y = jnp.einsum('blsgr,bsgrp->blgrp', cb[..., None] * decay * dk[:, None], xk)
        y = y + jnp.einsum('blgn,bgrpn->blgrp', ck, h) * jnp.exp(acum)[..., None]
        w_tail = jnp.exp(acum[:, -1:] - acum) * dk
        h = h * jnp.exp(acum[:, -1])[..., None, None] + jnp.einsum('bsgn,bsgr,bsgrp->bgrpn', bk, w_tail, xk)
        return h, y

    h0g = h0.reshape(b, SSD_GROUPS, SSD_HPG, SSD_HEAD_DIM, SSD_STATE)
    h, ys = lax.scan(step, h0g, (xc, dtc, bc, cc))
    y = jnp.moveaxis(ys, 0, 1).reshape(b, L, SSD_HEADS, SSD_HEAD_DIM)
    return y, h.reshape(b, SSD_HEADS, SSD_HEAD_DIM, SSD_STATE)


def window_attend(q, k, v, key_valid, sinks):
    lq, lk = q.shape[-3], k.shape[-3]
    qg = q.reshape(q.shape[:-2] + (N_KV, GQA, HEAD_DIM))
    s = jnp.einsum('...qkgd,...skd->...kgqs', qg, k.astype(q.dtype)).astype(jnp.float32) * (HEAD_DIM ** -0.5)
    dist = (jnp.arange(lq)[:, None] + (lk - lq) - jnp.arange(lk)[None, :]).astype(jnp.float32)
    slopes = alibi_slopes().reshape(N_KV, GQA, 1, 1)
    allowed = (dist >= 0) & (dist <= WINDOW) & key_valid[..., None, None, None, :]
    s = jnp.where(allowed, s - slopes * dist, NEG_INF)
    sink = sinks.astype(jnp.float32).reshape(N_KV, GQA, 1, 1)
    m = jnp.maximum(jnp.max(s, axis=-1, keepdims=True), sink)
    p = jnp.exp(s - m)
    denom = jnp.sum(p, axis=-1, keepdims=True) + jnp.exp(sink - m)
    o = jnp.einsum('...kgqs,...skd->...qkgd', p / denom, v.astype(jnp.float32))
    return o.reshape(q.shape[:-2] + (ATTN_WIDTH,)).astype(q.dtype)


def swa_banded(q, k, v, sinks):
    b, L = q.shape[:2]
    nb = L // WINDOW
    qb = q.reshape(b, nb, WINDOW, N_HEADS, HEAD_DIM)

    def band(t):
        tp = jnp.concatenate([jnp.zeros_like(t[:, :WINDOW]), t], axis=1)
        tp = tp.reshape(b, nb + 1, WINDOW, N_KV, HEAD_DIM)
        return jnp.concatenate([tp[:, :-1], tp[:, 1:]], axis=2)

    valid = (jnp.arange(nb)[:, None] > 0) | (jnp.arange(2 * WINDOW)[None, :] >= WINDOW)
    o = window_attend(qb, band(k), band(v), valid, sinks)
    return o.reshape(b, L, ATTN_WIDTH)


def parallel_mixers(xn, conv_buf, ssd_h, k_buf, v_buf, w_in, conv_w, conv_b, dt_bias, a_log, d_skip,
                    g_ssd, w_ssd_down, attn_sinks, w_mix_out):
    b, L, _ = xn.shape
    f32 = jnp.float32
    cuts = [int(c) for c in np.cumsum(IN_WIDTHS)[:-1]]
    z, xbc, dt_raw, q, k, v, g_s, g_a = jnp.split(xn @ w_in, cuts, axis=-1)
    xbc, conv_new = causal_conv(xbc, conv_buf, conv_w, conv_b)
    xs, bm, cm = jnp.split(xbc.astype(f32), [D_INNER, D_INNER + SSD_GROUPS * SSD_STATE], axis=-1)
    xs = xs.reshape(b, L, SSD_HEADS, SSD_HEAD_DIM)
    dt = jax.nn.softplus(dt_raw.astype(f32) + dt_bias.astype(f32))
    a = -jnp.exp(a_log.astype(f32))
    y, h_new = ssd_scan(xs, dt, a, bm.reshape(b, L, SSD_GROUPS, SSD_STATE),
                        cm.reshape(b, L, SSD_GROUPS, SSD_STATE), ssd_h.astype(f32))
    y = (y + d_skip.astype(f32)[:, None] * xs).reshape(b, L, D_INNER) * jax.nn.silu(z.astype(f32))
    y = rms_norm(y.reshape(b, L, SSD_GROUPS, D_INNER // SSD_GROUPS), g_ssd.reshape(SSD_GROUPS, -1))
    y_ssd = y.reshape(b, L, D_INNER).astype(xn.dtype) @ w_ssd_down
    q = q.reshape(b, L, N_HEADS, HEAD_DIM)
    k = k.reshape(b, L, N_KV, HEAD_DIM)
    v = v.reshape(b, L, N_KV, HEAD_DIM)
    if k_buf is None:
        y_att = swa_banded(q, k, v, attn_sinks)
        k_new, v_new = k[:, -WINDOW:], v[:, -WINDOW:]
    else:
        k_all = jnp.concatenate([k_buf.astype(k.dtype), k], axis=1)
        v_all = jnp.concatenate([v_buf.astype(v.dtype), v], axis=1)
        y_att = window_attend(q, k_all, v_all, jnp.ones((k_all.shape[1],), bool), attn_sinks).reshape(b, L, ATTN_WIDTH)
        k_new, v_new = k_all[:, -k_buf.shape[1]:], v_all[:, -v_buf.shape[1]:]
    merged = jax.nn.sigmoid(g_s) * y_ssd + jax.nn.sigmoid(g_a) * y_att
    return merged @ w_mix_out, (h_new.astype(xn.dtype), conv_new, k_new, v_new)


def memory_kv(mem, g_mem, w_ckv):
    b, m, _ = mem.shape
    mk, mv = jnp.split(rms_norm(mem, g_mem) @ w_ckv, 2, axis=-1)
    return mk.reshape(b, m, C_HEADS, C_HEAD_DIM), mv.reshape(b, m, C_HEADS, C_HEAD_DIM)


def cross_attend(xn, mem_k, mem_v, w_cq, w_co):
    b, L, _ = xn.shape
    q = (xn @ w_cq).reshape(b, L, C_HEADS, C_HEAD_DIM)
    s = jnp.einsum('bqhd,bmhd->bhqm', q, mem_k.astype(q.dtype)).astype(jnp.float32) * (C_HEAD_DIM ** -0.5)
    o = jnp.einsum('bhqm,bmhd->bqhd', jax.nn.softmax(s, axis=-1), mem_v.astype(jnp.float32))
    return o.reshape(b, L, C_WIDTH).astype(xn.dtype) @ w_co


def routed_experts(xf, expert, gate, w_gate, w_up, w_down):
    t = xf.shape[0]
    n_assign = t * TOP_K
    flat_e = expert.reshape(-1)
    order = jnp.argsort(flat_e)
    e_sorted = flat_e[order]
    tok_sorted = (order // TOP_K).astype(jnp.int32)
    counts = jnp.bincount(flat_e, length=N_EXPERTS)
    padded = (counts + MOE_BLOCK - 1) // MOE_BLOCK * MOE_BLOCK
    pad_end = jnp.cumsum(padded)
    rank = jnp.arange(n_assign) - (jnp.cumsum(counts) - counts)[e_sorted]
    dest = (pad_end - padded)[e_sorted] + rank
    n_blocks = -(-n_assign // MOE_BLOCK) + N_EXPERTS
    row_tok = jnp.zeros((n_blocks * MOE_BLOCK,), jnp.int32).at[dest].set(tok_sorted)
    xb = xf[row_tok].reshape(n_blocks, MOE_BLOCK, D_MODEL)
    block_e = jnp.minimum(jnp.searchsorted(pad_end, jnp.arange(n_blocks) * MOE_BLOCK, side='right'), N_EXPERTS - 1)

    def expert_mlp(args):
        xk, e = args
        return (jax.nn.silu(xk @ w_gate[e]) * (xk @ w_up[e])) @ w_down[e]

    yb = lax.map(expert_mlp, (xb, block_e)).reshape(-1, D_MODEL)
    contrib = yb[dest].astype(jnp.float32) * gate.reshape(-1)[order][:, None]
    return jax.ops.segment_sum(contrib, tok_sorted, num_segments=t).astype(xf.dtype)


def hier_moe(xn, w_rg, b_rg, w_re, b_re, w_gate, w_up, w_down):
    shp = xn.shape
    xf = xn.reshape(-1, D_MODEL)
    t = xf.shape[0]
    lg = (xf @ w_rg).astype(jnp.float32) + b_rg.astype(jnp.float32)
    grp = jnp.argmax(lg, axis=-1)
    p_grp = jnp.take_along_axis(jax.nn.softmax(lg, axis=-1), grp[:, None], axis=1)
    le = ((xf @ w_re).astype(jnp.float32) + b_re.astype(jnp.float32)).reshape(t, N_GROUPS, EXP_PER_GROUP)
    le = jnp.take_along_axis(le, grp[:, None, None], axis=1)[:, 0]
    top_p, top_i = lax.top_k(jax.nn.softmax(le, axis=-1), TOP_K)
    gate = p_grp * top_p / jnp.sum(top_p, axis=-1, keepdims=True)
    expert = (grp[:, None] * EXP_PER_GROUP + top_i).astype(jnp.int32)
    return routed_experts(xf, expert, gate, w_gate, w_up, w_down).reshape(shp)


def decoder_layer(x, conv_buf, ssd_h, k_buf, v_buf, mem_k, mem_v, lw):
    (g_mix, w_in, conv_w, conv_b, dt_bias, a_log, d_skip, g_ssd, w_ssd_down, attn_sinks, w_mix_out,
     g_cross, w_cq, w_co, g_moe, w_rg, b_rg, w_re, b_re, w_gate, w_up, w_down) = lw
    mix, state = parallel_mixers(rms_norm(x, g_mix), conv_buf, ssd_h, k_buf, v_buf, w_in, conv_w, conv_b,
                                 dt_bias, a_log, d_skip, g_ssd, w_ssd_down, attn_sinks, w_mix_out)
    x = x + mix
    x = x + cross_attend(rms_norm(x, g_cross), mem_k, mem_v, w_cq, w_co)
    x = x + hier_moe(rms_norm(x, g_moe), w_rg, b_rg, w_re, b_re, w_gate, w_up, w_down)
    return x, state


def setup_inputs(seed: int = 0) -> dict:
    key = jax.random.key(seed)
    keys = iter(list(jax.random.split(key, 64)))

    def nrm(shape, scale):
        return scale * jax.random.normal(next(keys), shape, jnp.float32)

    win_buf = min(WINDOW, PAST_LEN)
    dt0 = jnp.exp(jax.random.uniform(next(keys), (DEPTH, SSD_HEADS), jnp.float32, math.log(1e-3), math.log(1e-1)))
    dt_bias = dt0 + jnp.log(-jnp.expm1(-dt0))
    a_log = jnp.log(jax.random.uniform(next(keys), (DEPTH, SSD_HEADS), jnp.float32, 1.0, 16.0))
    return {
        'x_prompt': nrm((BATCH, SEQ, D_MODEL), 1.0),
        'x_sample': nrm((DEC_BATCH, DEC_SEQ, D_MODEL), 1.0),
        'state_ssd': nrm((DEPTH, DEC_BATCH, SSD_HEADS, SSD_HEAD_DIM, SSD_STATE), 0.5),
        'state_conv': nrm((DEPTH, DEC_BATCH, CONV_K - 1, CONV_DIM), 1.0),
        'cache_win_k': nrm((DEPTH, DEC_BATCH, win_buf, N_KV, HEAD_DIM), 1.0),
        'cache_win_v': nrm((DEPTH, DEC_BATCH, win_buf, N_KV, HEAD_DIM), 1.0),
        'cache_mem_k': nrm((DEPTH, DEC_BATCH, MEM_LEN, C_HEADS, C_HEAD_DIM), 1.0),
        'cache_mem_v': nrm((DEPTH, DEC_BATCH, MEM_LEN, C_HEADS, C_HEAD_DIM), 1.0),
        'mem_prompt': nrm((BATCH, MEM_LEN, D_MODEL), 1.0),
        'g_mix': 1.0 + nrm((DEPTH, D_MODEL), 0.02),
        'w_in': nrm((DEPTH, D_MODEL, IN_COLS), D_MODEL ** -0.5),
        'conv_w': nrm((DEPTH, CONV_K, CONV_DIM), CONV_K ** -0.5),
        'conv_b': nrm((DEPTH, CONV_DIM), 0.02),
        'dt_bias': dt_bias,
        'a_log': a_log,
        'd_skip': 1.0 + nrm((DEPTH, SSD_HEADS), 0.1),
        'g_ssd': 1.0 + nrm((DEPTH, D_INNER), 0.02),
        'w_ssd_down': nrm((DEPTH, D_INNER, D_MODEL), D_INNER ** -0.5),
        'attn_sinks': nrm((DEPTH, N_HEADS), 1.0),
        'w_mix_out': nrm((DEPTH, D_MODEL, D_MODEL), D_MODEL ** -0.5),
        'g_cross': 1.0 + nrm((DEPTH, D_MODEL), 0.02),
        'g_mem': 1.0 + nrm((DEPTH, D_MODEL), 0.02),
        'w_cq': nrm((DEPTH, D_MODEL, C_WIDTH), D_MODEL ** -0.5),
        'w_ckv': nrm((DEPTH, D_MODEL, 2 * C_WIDTH), D_MODEL ** -0.5),
        'w_co': nrm((DEPTH, C_WIDTH, D_MODEL), C_WIDTH ** -0.5),
        'g_moe': 1.0 + nrm((DEPTH, D_MODEL), 0.02),
        'w_route_group': nrm((DEPTH, D_MODEL, N_GROUPS), D_MODEL ** -0.5),
        'b_route_group': nrm((DEPTH, N_GROUPS), 0.01),
        'w_route_expert': nrm((DEPTH, D_MODEL, N_EXPERTS), D_MODEL ** -0.5),
        'b_route_expert': nrm((DEPTH, N_EXPERTS), 0.01),
        'w_e_gate': nrm((DEPTH, N_EXPERTS, D_MODEL, D_EXPERT), D_MODEL ** -0.5),
        'w_e_up': nrm((DEPTH, N_EXPERTS, D_MODEL, D_EXPERT), D_MODEL ** -0.5),
        'w_e_down': nrm((DEPTH, N_EXPERTS, D_EXPERT, D_MODEL), D_EXPERT ** -0.5),
        'g_final': 1.0 + nrm((D_MODEL,), 0.02),
    }


def reference(x_prompt, x_sample, state_ssd, state_conv, cache_win_k, cache_win_v, cache_mem_k, cache_mem_v,
              mem_prompt, g_mix, w_in, conv_w, conv_b, dt_bias, a_log, d_skip, g_ssd, w_ssd_down, attn_sinks,
              w_mix_out, g_cross, g_mem, w_cq, w_ckv, w_co, g_moe, w_route_group, b_route_group,
              w_route_expert, b_route_expert, w_e_gate, w_e_up, w_e_down, g_final):
    b = x_prompt.shape[0]
    xp, xs = x_prompt, x_sample
    p_ssd, p_conv, p_wk, p_wv, p_mk, p_mv = [], [], [], [], [], []
    s_ssd, s_conv, s_wk, s_wv = [], [], [], []
    for l in range(DEPTH):
        lw = (g_mix[l], w_in[l], conv_w[l], conv_b[l], dt_bias[l], a_log[l], d_skip[l], g_ssd[l], w_ssd_down[l],
              attn_sinks[l], w_mix_out[l], g_cross[l], w_cq[l], w_co[l], g_moe[l], w_route_group[l],
              b_route_group[l], w_route_expert[l], b_route_expert[l], w_e_gate[l], w_e_up[l], w_e_down[l])
        mk, mv = memory_kv(mem_prompt, g_mem[l], w_ckv[l])
        zero_conv = jnp.zeros((b, CONV_K - 1, CONV_DIM), xp.dtype)
        zero_h = jnp.zeros((b, SSD_HEADS, SSD_HEAD_DIM, SSD_STATE), xp.dtype)
        xp, (h_p, c_p, k_p, v_p) = decoder_layer(xp, zero_conv, zero_h, None, None, mk, mv, lw)
        xs, (h_s, c_s, k_s, v_s) = decoder_layer(xs, state_conv[l], state_ssd[l], cache_win_k[l], cache_win_v[l],
                                                 cache_mem_k[l], cache_mem_v[l], lw)
        p_ssd.append(h_p); p_conv.append(c_p); p_wk.append(k_p); p_wv.append(v_p)
        p_mk.append(mk); p_mv.append(mv)
        s_ssd.append(h_s); s_conv.append(c_s); s_wk.append(k_s); s_wv.append(v_s)
    y_prompt = rms_norm(xp, g_final)
    y_sample = rms_norm(xs, g_final)
    return (y_prompt, y_sample,
            jnp.stack(p_ssd), jnp.stack(p_conv), jnp.stack(p_wk), jnp.stack(p_wv), jnp.stack(p_mk), jnp.stack(p_mv),
            jnp.stack(s_ssd), jnp.stack(s_conv), jnp.stack(s_wk), jnp.stack(s_wv))
```

```python
import functools
import math

import jax
import jax.numpy as jnp
import numpy as np
from jax import lax
from jax.experimental import pallas as pl
from jax.experimental.pallas import tpu as pltpu

F32 = jnp.float32
BF16 = jnp.bfloat16
HIGHEST = lax.Precision.HIGHEST

D_MODEL = 1024
D_INNER = 2048
SSD_HEAD_DIM = 64
SSD_HEADS = 32
SSD_GROUPS = 4
SSD_HPG = 8
SSD_STATE = 128
CONV_K = 4
CONV_DIM = 3072
CHUNK = 128
HEAD_DIM = 64
N_HEADS = 16
N_KV = 4
GQA = 4
ATTN_WIDTH = 1024
KV_WIDTH = 256
WINDOW = 128
MEM_LEN = 256
C_HEADS = 4
C_HEAD_DIM = 128
C_WIDTH = 512
N_GROUPS = 4
EXP_PER_GROUP = 8
N_EXPERTS = 32
D_EXPERT = 512
EPS = 1e-6
NEG_INF = -1e30
LANES = 128
SUBLANES = 8

SEG_Z, SEG_XBC, SEG_Q, SEG_K, SEG_V, SEG_GS, SEG_GA, SEG_DT = (
    0, 2048, 5120, 6144, 6400, 6656, 7680, 8704)
PACKED_COLS = 8832
ROUTE_COLS = 128

MOE_ROWS = 256
ALIBI_SLOPES = tuple(2.0 ** (-8.0 * (h + 1) / N_HEADS) for h in range(N_HEADS))


def _tile(n, pref, mult=SUBLANES):
    if n <= pref:
        return n
    for t in range(pref, 0, -1):
        if n % t == 0 and t % mult == 0:
            return t
    return n


def _params(sem, vmem_mb):
    return pltpu.CompilerParams(dimension_semantics=sem, vmem_limit_bytes=vmem_mb * 1024 * 1024)


def _const_spec(shape):
    nd = len(shape)
    return pl.BlockSpec(shape, lambda *_: (0,) * nd, pipeline_mode=pl.Buffered(1))


def _sigmoid(x):
    return 1.0 / (1.0 + jnp.exp(-x))


def _silu(x):
    return x * _sigmoid(x)


def _softplus(x):
    return jnp.maximum(x, 0.0) + jnp.log1p(jnp.exp(-jnp.abs(x)))


def _rms(x, g):
    return x * lax.rsqrt(jnp.mean(x * x, axis=-1, keepdims=True) + EPS) * g


def _bf16_round(x):
    return x.astype(BF16).astype(F32)


def _dot(a, b, precision=None):
    return jnp.dot(a, b, preferred_element_type=F32, precision=precision)


def _dot_nt(a, b, precision=None):
    return lax.dot_general(a, b, (((1,), (1,)), ((), ())), preferred_element_type=F32, precision=precision)


def _inproj_body(x_ref, g_ref, w_ref, z_ref, xbc_ref, q_ref, k_ref, v_ref, gs_ref, ga_ref, dt_ref):
    xb = _rms(x_ref[...], g_ref[...]).astype(BF16)
    segs = ((z_ref, SEG_Z, D_INNER, 1.0), (xbc_ref, SEG_XBC, CONV_DIM, 1.0),
            (q_ref, SEG_Q, ATTN_WIDTH, HEAD_DIM ** -0.5), (k_ref, SEG_K, KV_WIDTH, 1.0),
            (v_ref, SEG_V, KV_WIDTH, 1.0), (gs_ref, SEG_GS, D_MODEL, 1.0), (ga_ref, SEG_GA, D_MODEL, 1.0),
            (dt_ref, SEG_DT, LANES, 1.0))
    step = 512
    for ref, off, width, scale in segs:
        for c0 in range(0, width, step):
            cw = min(step, width - c0)
            r = _dot(xb, w_ref[:, off + c0:off + c0 + cw])
            if scale != 1.0:
                r = r * scale
            ref[:, c0:c0 + cw] = r.astype(ref.dtype)


def _inproj(x2d, g_mix, w_packed):
    t = x2d.shape[0]
    tm = _tile(t, 512)
    widths = (D_INNER, CONV_DIM, ATTN_WIDTH, KV_WIDTH, KV_WIDTH, D_MODEL, D_MODEL)
    out_shape = [jax.ShapeDtypeStruct((t, w), BF16) for w in widths] + [jax.ShapeDtypeStruct((t, LANES), F32)]
    out_specs = [pl.BlockSpec((tm, w), lambda i: (i, 0)) for w in widths] + [pl.BlockSpec((tm, LANES), lambda i: (i, 0))]
    return pl.pallas_call(
        _inproj_body,
        grid=(t // tm,),
        in_specs=[pl.BlockSpec((tm, D_MODEL), lambda i: (i, 0)),
                  _const_spec((1, D_MODEL)),
                  _const_spec((D_MODEL, PACKED_COLS))],
        out_specs=out_specs,
        out_shape=out_shape,
        compiler_params=_params(("parallel",), 56),
        name="prompt_inproj",
    )(x2d, g_mix, w_packed)


def _mixer_body(xbc_ref, z_ref, dt_ref, q_ref, k_ref, v_ref, gs_ref, ga_ref, x_ref,
                convw_ref, convb_ref, dtb_ref, alog_ref, dskip_ref, gssd_ref, sinks_ref, wdown_ref, wmix_ref,
                h1_ref, pstate_ref, pconv_ref, pk_ref, pv_ref,
                cbuf, act, ybuf, att, kprev, vprev, state):
    c = pl.program_id(1)
    last = pl.num_programs(1) - 1
    L = CHUNK

    @pl.when(c == 0)
    def _():
        cbuf[0:SUBLANES, :] = jnp.zeros((SUBLANES, CONV_DIM), F32)
        kprev[...] = jnp.zeros_like(kprev)
        vprev[...] = jnp.zeros_like(vprev)
        state[...] = jnp.zeros_like(state)

    cbuf[SUBLANES:SUBLANES + L, :] = xbc_ref[...].astype(F32)
    cstep = 512
    for c0 in range(0, CONV_DIM, cstep):
        cs = slice(c0, c0 + cstep)
        acc = convb_ref[:, cs] + convw_ref[3:4, cs] * cbuf[8:8 + L, cs]
        acc = acc + convw_ref[2:3, cs] * cbuf[7:7 + L, cs]
        acc = acc + convw_ref[1:2, cs] * cbuf[6:6 + L, cs]
        acc = acc + convw_ref[0:1, cs] * cbuf[5:5 + L, cs]
        act[:, cs] = _silu(acc)

    @pl.when(c == last)
    def _():
        pconv_ref[...] = cbuf[SUBLANES + L - 3:SUBLANES + L, :]

    cbuf[0:SUBLANES, :] = cbuf[L:L + SUBLANES, :]

    dt = _softplus(dt_ref[...] + dtb_ref[...])
    a_neg = -jnp.exp(alog_ref[...])
    da = dt * a_neg
    ri = lax.broadcasted_iota(jnp.int32, (L, L), 0)
    ci = lax.broadcasted_iota(jnp.int32, (L, L), 1)
    causal = ri >= ci
    tri = jnp.where(causal, 1.0, 0.0).astype(F32)
    acum = _dot(tri, da, precision=HIGHEST)
    acum_t = acum.T
    dt_t = dt.T
    eacum = jnp.exp(acum)
    a_last = acum[L - 1:L, :]
    w_tail = jnp.exp(a_last - acum) * dt
    e_last = jnp.exp(a_last)
    lane = lax.broadcasted_iota(jnp.int32, (L, LANES), 1)
    lo_half = lane < SSD_HEAD_DIM
    lane1 = lax.broadcasted_iota(jnp.int32, (1, LANES), 1)
    lo_half1 = lane1 < SSD_HEAD_DIM

    for g in range(SSD_GROUPS):
        bg = act[:, D_INNER + g * SSD_STATE:D_INNER + (g + 1) * SSD_STATE]
        cg = act[:, D_INNER + (SSD_GROUPS + g) * SSD_STATE:D_INNER + (SSD_GROUPS + g + 1) * SSD_STATE]
        bgb = bg.astype(BF16)
        cgb = cg.astype(BF16)
        cb = _dot_nt(cgb, bgb)
        hg = state[g]
        yoff = _dot(cgb, hg.astype(BF16))
        bgt = bg.T.astype(BF16)
        for j in range(SSD_HPG // 2):
            h0 = g * SSD_HPG + 2 * j
            h1 = h0 + 1
            col = (g * SSD_HPG + 2 * j) * SSD_HEAD_DIM
            xs_pair = act[:, col:col + LANES]
            xpb = xs_pair.astype(BF16)
            ys = []
            for h in (h0, h1):
                seg = acum[:, h:h + 1] - acum_t[h:h + 1, :]
                decay = jnp.exp(jnp.where(causal, seg, NEG_INF))
                lmat = cb * decay * dt_t[h:h + 1, :]
                ys.append(_dot(lmat.astype(BF16), xpb))
            ydiag = jnp.where(lo_half, ys[0], ys[1])
            esc = jnp.where(lo_half, eacum[:, h0:h0 + 1], eacum[:, h1:h1 + 1])
            ybuf[:, col:col + LANES] = ydiag + yoff[:, 2 * j * SSD_HEAD_DIM:2 * j * SSD_HEAD_DIM + LANES] * esc
            wsc = jnp.where(lo_half, w_tail[:, h0:h0 + 1], w_tail[:, h1:h1 + 1])
            upd = _dot(bgt, (xs_pair * wsc).astype(BF16))
            dsc = jnp.where(lo_half1, e_last[:, h0:h0 + 1], e_last[:, h1:h1 + 1])
            pc = slice(2 * j * SSD_HEAD_DIM, 2 * j * SSD_HEAD_DIM + LANES)
            state[g, :, pc] = state[g, :, pc] * dsc + upd

    @pl.when(c == last)
    def _():
        for g in range(SSD_GROUPS):
            pstate_ref[g] = state[g].T

    gw = D_INNER // SSD_GROUPS
    for g in range(SSD_GROUPS):
        gs_ = slice(g * gw, (g + 1) * gw)
        yg = (ybuf[:, gs_] + dskip_ref[:, gs_] * act[:, gs_]) * _silu(z_ref[:, gs_].astype(F32))
        ybuf[:, gs_] = _rms(yg, gssd_ref[:, gs_])
    y_ssd = _dot(ybuf[...].astype(BF16), wdown_ref[...])

    dist_c = (ri - ci).astype(F32)
    dist_p = dist_c + float(L)
    ok_c = causal
    ok_p = ci >= ri + jnp.where(c > 0, 0, L)
    for h in range(N_HEADS):
        kv = h // GQA
        hs = slice(h * HEAD_DIM, (h + 1) * HEAD_DIM)
        ks = slice(kv * HEAD_DIM, (kv + 1) * HEAD_DIM)
        qh = q_ref[:, hs]
        s_p = jnp.where(ok_p, _dot_nt(qh, kprev[:, ks]) - ALIBI_SLOPES[h] * dist_p, NEG_INF)
        s_c = jnp.where(ok_c, _dot_nt(qh, k_ref[:, ks]) - ALIBI_SLOPES[h] * dist_c, NEG_INF)
        sink = sinks_ref[h]
        m = jnp.maximum(jnp.maximum(jnp.max(s_p, axis=-1, keepdims=True), jnp.max(s_c, axis=-1, keepdims=True)), sink)
        p_p = jnp.exp(s_p - m)
        p_c = jnp.exp(s_c - m)
        denom = jnp.sum(p_p, axis=-1, keepdims=True) + jnp.sum(p_c, axis=-1, keepdims=True) + jnp.exp(sink - m)
        o = _dot(p_p.astype(BF16), vprev[:, ks]) + _dot(p_c.astype(BF16), v_ref[:, ks])
        att[:, hs] = o / denom
    kprev[...] = k_ref[...]
    vprev[...] = v_ref[...]

    @pl.when(c == last)
    def _():
        pk_ref[...] = k_ref[...].astype(F32)
        pv_ref[...] = v_ref[...].astype(F32)

    merged = _sigmoid(gs_ref[...].astype(F32)) * y_ssd + _sigmoid(ga_ref[...].astype(F32)) * att[...]
    h1_ref[...] = x_ref[...] + _dot(merged.astype(BF16), wmix_ref[...])


def _mixer(x, z, xbc, q, k, v, gs, ga, dt, conv_w, conv_b, dtb, alog, dskip_x, g_ssd, sinks, w_down, w_mix):
    b, seq, _ = x.shape
    nc = seq // CHUNK

    def blk(width):
        return pl.BlockSpec((None, CHUNK, width), lambda i, j: (i, j, 0))

    def per_b(*shape):
        nd = len(shape)
        return pl.BlockSpec((None,) + shape, lambda i, j: (i,) + (0,) * nd)

    in_specs = [blk(CONV_DIM), blk(D_INNER), blk(LANES), blk(ATTN_WIDTH), blk(KV_WIDTH), blk(KV_WIDTH),
                blk(D_MODEL), blk(D_MODEL), blk(D_MODEL),
                _const_spec((CONV_K, CONV_DIM)), _const_spec((1, CONV_DIM)), _const_spec((1, LANES)),
                _const_spec((1, LANES)), _const_spec((1, D_INNER)), _const_spec((1, D_INNER)),
                pl.BlockSpec(memory_space=pltpu.SMEM),
                _const_spec((D_INNER, D_MODEL)), _const_spec((D_MODEL, D_MODEL))]
    out_shape = [jax.ShapeDtypeStruct((b, seq, D_MODEL), F32),
                 jax.ShapeDtypeStruct((b, SSD_GROUPS, SSD_HPG * SSD_HEAD_DIM, SSD_STATE), F32),
                 jax.ShapeDtypeStruct((b, CONV_K - 1, CONV_DIM), F32),
                 jax.ShapeDtypeStruct((b, WINDOW, KV_WIDTH), F32),
                 jax.ShapeDtypeStruct((b, WINDOW, KV_WIDTH), F32)]
    out_specs = [blk(D_MODEL), per_b(SSD_GROUPS, SSD_HPG * SSD_HEAD_DIM, SSD_STATE), per_b(CONV_K - 1, CONV_DIM),
                 per_b(WINDOW, KV_WIDTH), per_b(WINDOW, KV_WIDTH)]
    scratch = [pltpu.VMEM((CHUNK + 2 * SUBLANES, CONV_DIM), F32),
               pltpu.VMEM((CHUNK, CONV_DIM), F32),
               pltpu.VMEM((CHUNK, D_INNER), F32),
               pltpu.VMEM((CHUNK, ATTN_WIDTH), F32),
               pltpu.VMEM((CHUNK, KV_WIDTH), BF16), pltpu.VMEM((CHUNK, KV_WIDTH), BF16),
               pltpu.VMEM((SSD_GROUPS, SSD_STATE, SSD_HPG * SSD_HEAD_DIM), F32)]
    return pl.pallas_call(
        _mixer_body,
        grid=(b, nc),
        in_specs=in_specs,
        out_specs=out_specs,
        out_shape=out_shape,
        scratch_shapes=scratch,
        compiler_params=_params(("parallel", "arbitrary"), 48),
        name="prompt_mixer",
    )(xbc, z, dt, q, k, v, gs, ga, x, conv_w, conv_b, dtb, alog, dskip_x, g_ssd, sinks, w_down, w_mix)


def _memkv_body(m_ref, g_ref, w_ref, o_ref):
    o_ref[...] = _dot(_rms(m_ref[...], g_ref[...]).astype(BF16), w_ref[...])


def _memkv(mem2d, g_mem, w_ckv):
    t = mem2d.shape[0]
    tm = _tile(t, 256)
    return pl.pallas_call(
        _memkv_body,
        grid=(t // tm,),
        in_specs=[pl.BlockSpec((tm, D_MODEL), lambda i: (i, 0)), _const_spec((1, D_MODEL)),
                  _const_spec((D_MODEL, 2 * C_WIDTH))],
        out_specs=pl.BlockSpec((tm, 2 * C_WIDTH), lambda i: (i, 0)),
        out_shape=jax.ShapeDtypeStruct((t, 2 * C_WIDTH), F32),
        compiler_params=_params(("parallel",), 32),
        name="memory_kv",
    )(mem2d, g_mem, w_ckv)


def _route(logits):
    rows = logits.shape[0]
    lane = lax.broadcasted_iota(jnp.int32, (rows, ROUTE_COLS), 1).astype(F32)
    big = 1e9
    is_g = lane < N_GROUPS
    lg = jnp.where(is_g, logits, NEG_INF)
    gmax = jnp.max(lg, axis=-1, keepdims=True)
    grp = jnp.min(jnp.where(lg == gmax, lane, big), axis=-1, keepdims=True)
    p_grp = 1.0 / jnp.sum(jnp.where(is_g, jnp.exp(lg - gmax), 0.0), axis=-1, keepdims=True)
    lo = N_GROUPS + EXP_PER_GROUP * grp
    in_grp = (lane >= lo) & (lane < lo + EXP_PER_GROUP)
    le = jnp.where(in_grp, logits, NEG_INF)
    m1 = jnp.max(le, axis=-1, keepdims=True)
    i1 = jnp.min(jnp.where(le == m1, lane, big), axis=-1, keepdims=True)
    le2 = jnp.where(lane == i1, NEG_INF, le)
    m2 = jnp.max(le2, axis=-1, keepdims=True)
    i2 = jnp.min(jnp.where(le2 == m2, lane, big), axis=-1, keepdims=True)
    t2 = jnp.exp(m2 - m1)
    g1 = p_grp / (1.0 + t2)
    g2 = p_grp * t2 / (1.0 + t2)
    info = jnp.where(lane == 0, i1 - N_GROUPS,
                     jnp.where(lane == 1, i2 - N_GROUPS,
                               jnp.where(lane == 2, g1, jnp.where(lane == 3, g2, 0.0))))
    return info


def _cross_body(h1_ref, mkv_ref, gc_ref, wcq_ref, wco_ref, gm_ref, wr_ref, br_ref,
                h2_ref, xm_ref, route_ref, obuf):
    h1 = h1_ref[...]
    xn = _rms(h1, gc_ref[...]).astype(BF16)
    qc = _dot(xn, wcq_ref[...])
    scale = C_HEAD_DIM ** -0.5
    for h in range(C_HEADS):
        hs = slice(h * C_HEAD_DIM, (h + 1) * C_HEAD_DIM)
        mk = mkv_ref[:, hs].astype(BF16)
        mv = mkv_ref[:, C_WIDTH + h * C_HEAD_DIM:C_WIDTH + (h + 1) * C_HEAD_DIM].astype(BF16)
        s = _dot_nt(qc[:, hs].astype(BF16), mk) * scale
        m = jnp.max(s, axis=-1, keepdims=True)
        p = jnp.exp(s - m)
        obuf[:, hs] = _dot(p.astype(BF16), mv) / jnp.sum(p, axis=-1, keepdims=True)
    h2 = h1 + _dot(obuf[...].astype(BF16), wco_ref[...])
    h2_ref[...] = h2
    xm = _rms(h2, gm_ref[...])
    xm_ref[...] = xm
    logits = _dot(xm.astype(BF16), wr_ref[...]) + br_ref[...]
    route_ref[...] = _route(logits).T[0:SUBLANES, :]


def _cross(h1, mkv, g_cross, w_cq, w_co, g_moe, w_r, b_r):
    b, seq, _ = h1.shape
    tq = _tile(seq, 512)
    nq = seq // tq
    return pl.pallas_call(
        _cross_body,
        grid=(b, nq),
        in_specs=[pl.BlockSpec((None, tq, D_MODEL), lambda i, j: (i, j, 0)),
                  pl.BlockSpec((None, MEM_LEN, 2 * C_WIDTH), lambda i, j: (i, 0, 0)),
                  _const_spec((1, D_MODEL)), _const_spec((D_MODEL, C_WIDTH)), _const_spec((C_WIDTH, D_MODEL)),
                  _const_spec((1, D_MODEL)), _const_spec((D_MODEL, ROUTE_COLS)), _const_spec((1, ROUTE_COLS))],
        out_specs=[pl.BlockSpec((None, tq, D_MODEL), lambda i, j: (i, j, 0)),
                   pl.BlockSpec((None, tq, D_MODEL), lambda i, j: (i, j, 0)),
                   pl.BlockSpec((SUBLANES, tq), lambda i, j: (0, i * nq + j))],
        out_shape=[jax.ShapeDtypeStruct((b, seq, D_MODEL), F32),
                   jax.ShapeDtypeStruct((b, seq, D_MODEL), F32),
                   jax.ShapeDtypeStruct((SUBLANES, b * seq), F32)],
        scratch_shapes=[pltpu.VMEM((tq, C_WIDTH), F32)],
        compiler_params=_params(("parallel", "parallel"), 48),
        name="prompt_cross_route",
    )(h1, mkv, g_cross, w_cq, w_co, g_moe, w_r, b_r)


RANK_TILE = 512


def _rank_body(route_ref, rank_ref, count_ref, carry):
    i = pl.program_id(0)

    @pl.when(i == 0)
    def _():
        carry[...] = jnp.zeros_like(carry)

    e1 = route_ref[0:1, :]
    e2 = route_ref[1:2, :]
    eid = lax.broadcasted_iota(jnp.int32, (N_EXPERTS, RANK_TILE), 0).astype(F32)
    is1 = e1 == eid
    is2 = e2 == eid
    onehot = jnp.where(is1 | is2, 1.0, 0.0)
    si = lax.broadcasted_iota(jnp.int32, (RANK_TILE, RANK_TILE), 0)
    ti = lax.broadcasted_iota(jnp.int32, (RANK_TILE, RANK_TILE), 1)
    before = jnp.where(si < ti, 1.0, 0.0).astype(BF16)
    prefix = _dot(onehot.astype(BF16), before) + carry[:, 0:1]
    r1 = jnp.sum(jnp.where(is1, prefix, 0.0), axis=0, keepdims=True)
    r2 = jnp.sum(jnp.where(is2, prefix, 0.0), axis=0, keepdims=True)
    row = lax.broadcasted_iota(jnp.int32, (SUBLANES, RANK_TILE), 0)
    rank_ref[...] = jnp.where(row == 0, r1, jnp.where(row == 1, r2, 0.0))
    carry[...] = carry[...] + jnp.sum(onehot, axis=1, keepdims=True)
    count_ref[...] = carry[...]


def _rank(route):
    tp = route.shape[1]
    return pl.pallas_call(
        _rank_body,
        grid=(tp // RANK_TILE,),
        in_specs=[pl.BlockSpec((SUBLANES, RANK_TILE), lambda i: (0, i))],
        out_specs=[pl.BlockSpec((SUBLANES, RANK_TILE), lambda i: (0, i)),
                   pl.BlockSpec((N_EXPERTS, LANES), lambda i: (0, 0))],
        out_shape=[jax.ShapeDtypeStruct((SUBLANES, tp), F32), jax.ShapeDtypeStruct((N_EXPERTS, LANES), F32)],
        scratch_shapes=[pltpu.VMEM((N_EXPERTS, LANES), F32)],
        compiler_params=_params(("arbitrary",), 32),
        name="moe_rank",
    )(route)


def _dest_body(route_ref, rank_ref, offs_ref, dest_ref):
    e1 = route_ref[0:1, :]
    e2 = route_ref[1:2, :]
    eid = lax.broadcasted_iota(jnp.int32, (N_EXPERTS, RANK_TILE), 0).astype(F32)
    offs = offs_ref[:, 0:1]
    d1 = jnp.sum(jnp.where(e1 == eid, offs, 0.0), axis=0, keepdims=True) + rank_ref[0:1, :]
    d2 = jnp.sum(jnp.where(e2 == eid, offs, 0.0), axis=0, keepdims=True) + rank_ref[1:2, :]
    row = lax.broadcasted_iota(jnp.int32, (SUBLANES, RANK_TILE), 0)
    dest_ref[...] = jnp.where(row == 0, d1, jnp.where(row == 1, d2, 0.0)).astype(jnp.int32)


def _dest(route, rank, offs):
    tp = route.shape[1]
    return pl.pallas_call(
        _dest_body,
        grid=(tp // RANK_TILE,),
        in_specs=[pl.BlockSpec((SUBLANES, RANK_TILE), lambda i: (0, i)),
                  pl.BlockSpec((SUBLANES, RANK_TILE), lambda i: (0, i)),
                  pl.BlockSpec((N_EXPERTS, LANES), lambda i: (0, 0))],
        out_specs=pl.BlockSpec((SUBLANES, RANK_TILE), lambda i: (0, i)),
        out_shape=jax.ShapeDtypeStruct((SUBLANES, tp), jnp.int32),
        compiler_params=_params(("parallel",), 32),
        name="moe_dest",
    )(route, rank, offs)


def _row_copy(src, dst, s_row, d_row, sem):
    return pltpu.make_async_copy(src.at[pl.ds(s_row, 1)], dst.at[pl.ds(d_row, 1)], sem)


def _dispatch_body(td, dest_ref, xm_ref, xb_in_ref, xb_ref, sem):
    del xb_in_ref
    base = pl.program_id(0) * td

    def issue(t, carry):
        row = base + t
        _row_copy(xm_ref, xb_ref, row, dest_ref[2 * row], sem).start()
        _row_copy(xm_ref, xb_ref, row, dest_ref[2 * row + 1], sem).start()
        return carry

    lax.fori_loop(0, td, issue, 0)

    def drain(t, carry):
        _row_copy(xm_ref, xb_ref, 0, 0, sem).wait()
        _row_copy(xm_ref, xb_ref, 0, 0, sem).wait()
        return carry

    lax.fori_loop(0, td, drain, 0)


def _dispatch(dest_flat, xm2d, xb):
    t = xm2d.shape[0]
    td = _tile(t, 512)
    return pl.pallas_call(
        functools.partial(_dispatch_body, td),
        grid_spec=pltpu.PrefetchScalarGridSpec(
            num_scalar_prefetch=1,
            grid=(t // td,),
            in_specs=[pl.BlockSpec(memory_space=pl.ANY), pl.BlockSpec(memory_space=pl.ANY)],
            out_specs=pl.BlockSpec(memory_space=pl.ANY),
            scratch_shapes=[pltpu.SemaphoreType.DMA],
        ),
        out_shape=jax.ShapeDtypeStruct(xb.shape, xb.dtype),
        input_output_aliases={2: 0},
        compiler_params=_params(("arbitrary",), 32),
        name="moe_dispatch",
    )(dest_flat, xm2d, xb)


def _expert_body(be_ref, nused_ref, xb_ref, wg_ref, wu_ref, wd_ref, yb_ref):
    i = pl.program_id(0)

    @pl.when(i < nused_ref[0])
    def _():
        x = xb_ref[...].astype(BF16)
        hmid = _silu(_dot(x, wg_ref[...])) * _dot(x, wu_ref[...])
        yb_ref[...] = _dot(hmid.astype(BF16), wd_ref[...])

    @pl.when(i >= nused_ref[0])
    def _():
        yb_ref[...] = jnp.zeros_like(yb_ref)


def _experts(block_e, n_used, xb, wg, wu, wd):
    rows = xb.shape[0]
    nb = rows // MOE_ROWS

    def xmap(i, be, nu):
        return (jnp.minimum(i, nu[0] - 1), 0)

    def wmap(i, be, nu):
        return (be[i], 0, 0)

    return pl.pallas_call(
        _expert_body,
        grid_spec=pltpu.PrefetchScalarGridSpec(
            num_scalar_prefetch=2,
            grid=(nb,),
            in_specs=[pl.BlockSpec((MOE_ROWS, D_MODEL), xmap),
                      pl.BlockSpec((None, D_MODEL, D_EXPERT), wmap),
                      pl.BlockSpec((None, D_MODEL, D_EXPERT), wmap),
                      pl.BlockSpec((None, D_EXPERT, D_MODEL), wmap)],
            out_specs=pl.BlockSpec((MOE_ROWS, D_MODEL), lambda i, be, nu: (i, 0)),
        ),
        out_shape=jax.ShapeDtypeStruct((rows, D_MODEL), F32),
        compiler_params=_params(("arbitrary",), 40),
        name="moe_experts",
    )(block_e, n_used, xb, wg, wu, wd)


def _combine_body(tc, dest_ref, yb_ref, h2_ref, gate_ref, gf_ref, y_ref, buf, sem):
    base = pl.program_id(0) * tc

    def issue(t, carry):
        row = base + t
        pltpu.make_async_copy(yb_ref.at[pl.ds(dest_ref[2 * row], 1)], buf.at[0, pl.ds(t, 1)], sem).start()
        pltpu.make_async_copy(yb_ref.at[pl.ds(dest_ref[2 * row + 1], 1)], buf.at[1, pl.ds(t, 1)], sem).start()
        return carry

    lax.fori_loop(0, tc, issue, 0)

    def drain(t, carry):
        pltpu.make_async_copy(yb_ref.at[pl.ds(0, 1)], buf.at[0, pl.ds(0, 1)], sem).wait()
        pltpu.make_async_copy(yb_ref.at[pl.ds(0, 1)], buf.at[1, pl.ds(0, 1)], sem).wait()
        return carry

    lax.fori_loop(0, tc, drain, 0)
    g = gate_ref[...]
    out = h2_ref[...] + (g[:, 0:1] * buf[0] + g[:, 1:2] * buf[1])
    y_ref[...] = _rms(out, gf_ref[...])


def _combine(dest_flat, yb, h2_2d, gates, g_final):
    t = h2_2d.shape[0]
    tc = _tile(t, 256)
    return pl.pallas_call(
        functools.partial(_combine_body, tc),
        grid_spec=pltpu.PrefetchScalarGridSpec(
            num_scalar_prefetch=1,
            grid=(t // tc,),
            in_specs=[pl.BlockSpec(memory_space=pl.ANY),
                      pl.BlockSpec((tc, D_MODEL), lambda i, d: (i, 0)),
                      pl.BlockSpec((tc, 2), lambda i, d: (i, 0)),
                      pl.BlockSpec((1, D_MODEL), lambda i, d: (0, 0))],
            out_specs=pl.BlockSpec((tc, D_MODEL), lambda i, d: (i, 0)),
            scratch_shapes=[pltpu.VMEM((2, tc, D_MODEL), F32), pltpu.SemaphoreType.DMA],
        ),
        out_shape=jax.ShapeDtypeStruct((t, D_MODEL), F32),
        compiler_params=_params(("arbitrary",), 32),
        name="moe_combine",
    )(dest_flat, yb, h2_2d, gates, g_final)


def _s_inproj_body(x_ref, g_ref, w_ref, o_ref):
    o_ref[...] = _dot(_rms(x_ref[...], g_ref[...]).astype(BF16), w_ref[...])


def _s_inproj(x, g_mix, w_packed_f32):
    n = x.shape[0]
    tn = PACKED_COLS // 3
    return pl.pallas_call(
        _s_inproj_body,
        grid=(PACKED_COLS // tn,),
        in_specs=[pl.BlockSpec((n, D_MODEL), lambda j: (0, 0)), pl.BlockSpec((1, D_MODEL), lambda j: (0, 0)),
                  pl.BlockSpec((D_MODEL, tn), lambda j: (0, j))],
        out_specs=pl.BlockSpec((n, tn), lambda j: (0, j)),
        out_shape=jax.ShapeDtypeStruct((n, PACKED_COLS), F32),
        compiler_params=_params(("parallel",), 48),
        name="sample_inproj",
    )(x, g_mix, w_packed_f32)


def _s_conv_body(proj_ref, cs_ref, convw_ref, convb_ref, dtb_ref, alog_ref, xexp_ref,
                 act_ref, ncs_ref, dtx_ref, e_ref):
    step = 512
    for c0 in range(0, CONV_DIM, step):
        cs = slice(c0, c0 + step)
        s0 = cs_ref[:, c0:c0 + step]
        s1 = cs_ref[:, CONV_DIM + c0:CONV_DIM + c0 + step]
        s2 = cs_ref[:, 2 * CONV_DIM + c0:2 * CONV_DIM + c0 + step]
        xn = proj_ref[:, SEG_XBC + c0:SEG_XBC + c0 + step]
        acc = convb_ref[:, cs] + convw_ref[0:1, cs] * s0
        acc = acc + convw_ref[1:2, cs] * s1
        acc = acc + convw_ref[2:3, cs] * s2
        acc = acc + convw_ref[3:4, cs] * xn
        act_ref[:, cs] = _silu(acc)
        ncs_ref[:, c0:c0 + step] = s1
        ncs_ref[:, CONV_DIM + c0:CONV_DIM + c0 + step] = s2
        ncs_ref[:, 2 * CONV_DIM + c0:2 * CONV_DIM + c0 + step] = xn
    dt = _softplus(proj_ref[:, SEG_DT:SEG_DT + LANES] + dtb_ref[...])
    e_ref[...] = jnp.exp(dt * (-jnp.exp(alog_ref[...])))
    dtx_ref[...] = _dot(dt, xexp_ref[...], precision=HIGHEST) * act_ref[:, 0:D_INNER]


def _s_conv(proj, conv_state2d, conv_w, conv_b, dtb, alog, xexp):
    n = proj.shape[0]
    return pl.pallas_call(
        _s_conv_body,
        out_shape=[jax.ShapeDtypeStruct((n, CONV_DIM), F32), jax.ShapeDtypeStruct((n, 3 * CONV_DIM), F32),
                   jax.ShapeDtypeStruct((n, D_INNER), F32), jax.ShapeDtypeStruct((n, LANES), F32)],
        compiler_params=pltpu.CompilerParams(vmem_limit_bytes=48 * 1024 * 1024),
        name="sample_conv",
    )(proj, conv_state2d, conv_w, conv_b, dtb, alog, xexp)


def _s_state_body(bb, e_ref, st_ref, dtxt_ref, b_ref, c_ref, so_ref, yt_ref):
    base = pl.program_id(0) * bb
    lane = lax.broadcasted_iota(jnp.int32, (SSD_HEAD_DIM, LANES), 1)

    def one(bl, carry):
        yacc = jnp.zeros((SSD_HEAD_DIM, LANES), F32)
        for h in range(SSD_HEADS):
            g = h // SSD_HPG
            brow = b_ref[bl, :, g * SSD_STATE:(g + 1) * SSD_STATE]
            crow = c_ref[bl, :, g * SSD_STATE:(g + 1) * SSD_STATE]
            xcol = dtxt_ref[bl, :, h:h + 1]
            hn = st_ref[bl, h] * e_ref[base + bl, h] + xcol * brow
            so_ref[bl, h] = hn
            ycol = jnp.sum(hn * crow, axis=-1, keepdims=True)
            yacc = jnp.where(lane == h, ycol, yacc)
        yt_ref[bl] = yacc
        return carry

    lax.fori_loop(0, bb, one, 0)


def _s_state(e, state, dtxt, bmat, cmat):
    n = state.shape[0]
    bb = _tile(n, 4, 1)
    sblk = (bb, SSD_HEADS, SSD_HEAD_DIM, SSD_STATE)
    bmat = bmat.reshape(n, 1, SSD_GROUPS * SSD_STATE)
    cmat = cmat.reshape(n, 1, SSD_GROUPS * SSD_STATE)
    return pl.pallas_call(
        functools.partial(_s_state_body, bb),
        grid=(n // bb,),
        in_specs=[pl.BlockSpec(memory_space=pltpu.SMEM),
                  pl.BlockSpec(sblk, lambda i: (i, 0, 0, 0)),
                  pl.BlockSpec((bb, SSD_HEAD_DIM, SSD_HEADS), lambda i: (i, 0, 0)),
                  pl.BlockSpec((bb, 1, SSD_GROUPS * SSD_STATE), lambda i: (i, 0, 0)),
                  pl.BlockSpec((bb, 1, SSD_GROUPS * SSD_STATE), lambda i: (i, 0, 0))],
        out_specs=[pl.BlockSpec(sblk, lambda i: (i, 0, 0, 0)),
                   pl.BlockSpec((bb, SSD_HEAD_DIM, LANES), lambda i: (i, 0, 0))],
        out_shape=[jax.ShapeDtypeStruct(state.shape, F32), jax.ShapeDtypeStruct((n, SSD_HEAD_DIM, LANES), F32)],
        compiler_params=_params(("parallel",), 40),
        name="sample_ssd_state",
    )(e, state, dtxt, bmat, cmat)


def _s_attn_body(bb, qexp_ref, ck_ref, cv_ref, kn_ref, vn_ref, sink_ref, slope_ref,
                 y_ref, ok_ref, ov_ref):
    W = WINDOW
    lane = lax.broadcasted_iota(jnp.int32, (W, LANES), 1)
    lane1 = lax.broadcasted_iota(jnp.int32, (1, LANES), 1)
    jrow = lax.broadcasted_iota(jnp.int32, (W, LANES), 0)
    bias = slope_ref[...] * (W - jrow).astype(F32)
    sink = sink_ref[...]
    lo_half = lane1 < HEAD_DIM

    def one(bl, carry):
        kn = kn_ref[bl]
        vn = vn_ref[bl]
        kmat = _bf16_round(ck_ref[bl])
        vmat = _bf16_round(cv_ref[bl])
        knr = _bf16_round(kn)
        vnr = _bf16_round(vn)
        s = jnp.zeros((W, LANES), F32)
        sn = jnp.zeros((1, LANES), F32)
        for h in range(N_HEADS):
            qrow = _bf16_round(qexp_ref[bl, h:h + 1, :])
            s = jnp.where(lane == h, jnp.sum(kmat * qrow, axis=-1, keepdims=True), s)
            sn = jnp.where(lane1 == h, jnp.sum(knr * qrow, axis=-1, keepdims=True), sn)
        s = s * (HEAD_DIM ** -0.5) - bias
        sn = sn * (HEAD_DIM ** -0.5)
        m = jnp.maximum(jnp.maximum(jnp.max(s, axis=0, keepdims=True), sn), sink)
        p = jnp.exp(s - m)
        pn = jnp.exp(sn - m)
        denom = jnp.sum(p, axis=0, keepdims=True) + pn + jnp.exp(sink - m)
        p = _bf16_round(p / denom)
        pn = _bf16_round(pn / denom)
        for j in range(N_HEADS // 2):
            outs = []
            for h in (2 * j, 2 * j + 1):
                kv = h // GQA
                cs = slice((kv // 2) * LANES, (kv // 2 + 1) * LANES)
                o = jnp.sum(p[:, h:h + 1] * vmat[:, cs], axis=0, keepdims=True) + pn[:, h:h + 1] * vnr[:, cs]
                if (kv % 2) != (h % 2):
                    o = pltpu.roll(o, HEAD_DIM, axis=1)
                outs.append(o)
            y_ref[bl, :, j * LANES:(j + 1) * LANES] = jnp.where(lo_half, outs[0], outs[1])
        ok_ref[bl, 0:W - 1, :] = ck_ref[bl, 1:W, :]
        ok_ref[bl, W - 1:W, :] = kn
        ov_ref[bl, 0:W - 1, :] = cv_ref[bl, 1:W, :]
        ov_ref[bl, W - 1:W, :] = vn
        return carry

    lax.fori_loop(0, bb, one, 0)


def _s_attn(qexp, ck, cv, kn, vn, sink_row, slope_row):
    n = ck.shape[0]
    bb = _tile(n, 8, 1)
    cblk = pl.BlockSpec((bb, WINDOW, KV_WIDTH), lambda i: (i, 0, 0))
    rblk = pl.BlockSpec((bb, 1, KV_WIDTH), lambda i: (i, 0, 0))
    y, ok, ov = pl.pallas_call(
        functools.partial(_s_attn_body, bb),
        grid=(n // bb,),
        in_specs=[pl.BlockSpec((bb, N_HEADS, KV_WIDTH), lambda i: (i, 0, 0)), cblk, cblk, rblk, rblk,
                  pl.BlockSpec((1, LANES), lambda i: (0, 0)), pl.BlockSpec((1, LANES), lambda i: (0, 0))],
        out_specs=[pl.BlockSpec((bb, 1, ATTN_WIDTH), lambda i: (i, 0, 0)), cblk, cblk],
        out_shape=[jax.ShapeDtypeStruct((n, 1, ATTN_WIDTH), F32), jax.ShapeDtypeStruct(ck.shape, F32),
                   jax.ShapeDtypeStruct(cv.shape, F32)],
        compiler_params=_params(("parallel",), 32),
        name="sample_window_attn",
    )(qexp, ck, cv, kn.reshape(n, 1, KV_WIDTH), vn.reshape(n, 1, KV_WIDTH), sink_row, slope_row)
    return y.reshape(n, ATTN_WIDTH), ok, ov


def _s_post_body(y_ref, act_ref, proj_ref, att_ref, x_ref, dskip_ref, gssd_ref, wdown_ref, wmix_ref,
                 gc_ref, wcq_ref, h1_ref, qc_ref, ybuf):
    gw = D_INNER // SSD_GROUPS
    for g in range(SSD_GROUPS):
        gs_ = slice(g * gw, (g + 1) * gw)
        yg = (y_ref[:, gs_] + dskip_ref[:, gs_] * act_ref[:, gs_]) * _silu(proj_ref[:, SEG_Z + g * gw:SEG_Z + (g + 1) * gw])
        ybuf[:, gs_] = _rms(yg, gssd_ref[:, gs_])
    y_ssd = _dot(ybuf[...].astype(BF16), wdown_ref[...])
    merged = (_sigmoid(proj_ref[:, SEG_GS:SEG_GS + D_MODEL]) * y_ssd
              + _sigmoid(proj_ref[:, SEG_GA:SEG_GA + D_MODEL]) * att_ref[...])
    h1 = x_ref[...] + _dot(merged.astype(BF16), wmix_ref[...])
    h1_ref[...] = h1
    qc_ref[...] = _bf16_round(_dot(_rms(h1, gc_ref[...]).astype(BF16), wcq_ref[...]))


def _s_post(y, act, proj, att, x, dskip_x, g_ssd, w_down, w_mix, g_cross, w_cq):
    n = x.shape[0]
    return pl.pallas_call(
        _s_post_body,
        out_shape=[jax.ShapeDtypeStruct((n, D_MODEL), F32), jax.ShapeDtypeStruct((n, C_WIDTH), F32)],
        scratch_shapes=[pltpu.VMEM((n, D_INNER), F32)],
        compiler_params=pltpu.CompilerParams(vmem_limit_bytes=48 * 1024 * 1024),
        name="sample_post_mixer",
    )(y, act, proj, att, x, dskip_x, g_ssd, w_down, w_mix, g_cross, w_cq)


def _s_cross_body(bb, qc_ref, mk_ref, mv_ref, o_ref):
    lane = lax.broadcasted_iota(jnp.int32, (MEM_LEN, LANES), 1)
    scale = C_HEAD_DIM ** -0.5

    def one(bl, carry):
        kmat = _bf16_round(mk_ref[bl])
        vmat = _bf16_round(mv_ref[bl])
        q = qc_ref[bl]
        s = jnp.zeros((MEM_LEN, LANES), F32)
        for h in range(C_HEADS):
            hs = slice(h * C_HEAD_DIM, (h + 1) * C_HEAD_DIM)
            s = jnp.where(lane == h, jnp.sum(kmat[:, hs] * q[:, hs], axis=-1, keepdims=True), s)
        s = s * scale
        m = jnp.max(s, axis=0, keepdims=True)
        p = jnp.exp(s - m)
        p = _bf16_round(p / jnp.sum(p, axis=0, keepdims=True))
        for h in range(C_HEADS):
            hs = slice(h * C_HEAD_DIM, (h + 1) * C_HEAD_DIM)
            o_ref[bl, :, hs] = jnp.sum(p[:, h:h + 1] * vmat[:, hs], axis=0, keepdims=True)
        return carry

    lax.fori_loop(0, bb, one, 0)


def _s_cross(qc, mk, mv):
    n = qc.shape[0]
    bb = _tile(n, 8, 1)
    mblk = pl.BlockSpec((bb, MEM_LEN, C_WIDTH), lambda i: (i, 0, 0))
    rblk = pl.BlockSpec((bb, 1, C_WIDTH), lambda i: (i, 0, 0))
    return pl.pallas_call(
        functools.partial(_s_cross_body, bb),
        grid=(n // bb,),
        in_specs=[rblk, mblk, mblk],
        out_specs=rblk,
        out_shape=jax.ShapeDtypeStruct((n, 1, C_WIDTH), F32),
        compiler_params=_params(("parallel",), 40),
        name="sample_cross_attn",
    )(qc.reshape(n, 1, C_WIDTH), mk, mv).reshape(n, C_WIDTH)


def _s_route_body(o_ref, h1_ref, wco_ref, gm_ref, wr_ref, br_ref, h2_ref, xm_ref, route_ref):
    h2 = h1_ref[...] + _dot(o_ref[...].astype(BF16), wco_ref[...])
    h2_ref[...] = h2
    xm = _rms(h2, gm_ref[...])
    xm_ref[...] = xm
    logits = _dot(xm.astype(BF16), wr_ref[...]) + br_ref[...]
    route_ref[...] = _route(logits).T[0:SUBLANES, :]


def _s_route(o, h1, w_co, g_moe, w_r, b_r):
    n = o.shape[0]
    return pl.pallas_call(
        _s_route_body,
        out_shape=[jax.ShapeDtypeStruct((n, D_MODEL), F32), jax.ShapeDtypeStruct((n, D_MODEL), F32),
                   jax.ShapeDtypeStruct((SUBLANES, n), F32)],
        compiler_params=pltpu.CompilerParams(vmem_limit_bytes=32 * 1024 * 1024),
        name="sample_cross_out_route",
    )(o, h1, w_co, g_moe, w_r, b_r)


def _pack_in_weights(w_in):
    cuts = np.cumsum((D_INNER, CONV_DIM, SSD_HEADS, ATTN_WIDTH, KV_WIDTH, KV_WIDTH, D_MODEL, D_MODEL))[:-1]
    z, xbc, dt, q, k, v, gs, ga = jnp.split(w_in, [int(c) for c in cuts], axis=1)
    dt = jnp.pad(dt, ((0, 0), (0, LANES - SSD_HEADS)))
    return jnp.concatenate([z, xbc, q, k, v, gs, ga, dt], axis=1)


def _head_expand_matrix():
    m = np.zeros((LANES, D_INNER), np.float32)
    for h in range(SSD_HEADS):
        m[h, h * SSD_HEAD_DIM:(h + 1) * SSD_HEAD_DIM] = 1.0
    return m


def _row(v, width=None):
    v = v.reshape(1, -1)
    if width is not None and v.shape[1] < width:
        v = jnp.pad(v, ((0, 0), (0, width - v.shape[1])))
    return v


def kernel(x_prompt, x_sample, state_ssd, state_conv, cache_win_k, cache_win_v, cache_mem_k, cache_mem_v, mem_prompt, g_mix, w_in, conv_w, conv_b, dt_bias, a_log, d_skip, g_ssd, w_ssd_down, attn_sinks, w_mix_out, g_cross, g_mem, w_cq, w_ckv, w_co, g_moe, w_route_group, b_route_group, w_route_expert, b_route_expert, w_e_gate, w_e_up, w_e_down, g_final):
    assert g_mix.shape[0] == 1, "single layer"
    b, seq, _ = x_prompt.shape
    n_s = x_sample.shape[0]
    t_p = b * seq
    assert seq % CHUNK == 0 and x_sample.shape[1] == 1

    w_packed_bf = _pack_in_weights(w_in[0].astype(BF16))
    w_down_bf, w_mix_bf = w_ssd_down[0].astype(BF16), w_mix_out[0].astype(BF16)
    w_cq_bf, w_co_bf = w_cq[0].astype(BF16), w_co[0].astype(BF16)
    g_mix_r, g_cross_r, g_mem_r, g_moe_r, g_final_r = (_row(g_mix[0]), _row(g_cross[0]), _row(g_mem[0]),
                                                        _row(g_moe[0]), _row(g_final))
    conv_b_r = _row(conv_b[0])
    dtb_r = _row(dt_bias[0], LANES)
    alog_r = _row(a_log[0], LANES)
    dskip_x = _row(jnp.repeat(d_skip[0], SSD_HEAD_DIM))
    g_ssd_r = _row(g_ssd[0])
    w_r = jnp.pad(jnp.concatenate([w_route_group[0], w_route_expert[0]], axis=1),
                  ((0, 0), (0, ROUTE_COLS - N_GROUPS - N_EXPERTS))).astype(BF16)
    b_r = _row(jnp.concatenate([b_route_group[0], b_route_expert[0]]), ROUTE_COLS)

    z, xbc, q, k, v, gs, ga, dt = _inproj(x_prompt.reshape(t_p, D_MODEL), g_mix_r, w_packed_bf)
    r3 = lambda a: a.reshape(b, seq, a.shape[-1])
    h1, p_state, p_conv, p_wk, p_wv = _mixer(
        x_prompt, r3(z), r3(xbc), r3(q), r3(k), r3(v), r3(gs), r3(ga), r3(dt),
        conv_w[0], conv_b_r, dtb_r, alog_r, dskip_x, g_ssd_r, attn_sinks[0], w_down_bf, w_mix_bf)
    mkv = _memkv(mem_prompt.reshape(b * MEM_LEN, D_MODEL), g_mem_r, w_ckv[0].astype(BF16))
    h2_p, xm_p, route_p = _cross(h1, mkv.reshape(b, MEM_LEN, 2 * C_WIDTH), g_cross_r,
                                 w_cq_bf, w_co_bf, g_moe_r, w_r, b_r)

    xs2 = x_sample.reshape(n_s, D_MODEL)
    proj = _s_inproj(xs2, g_mix_r, w_packed_bf)
    xexp = jnp.asarray(_head_expand_matrix())
    act_s, s_conv, dtx, e_s = _s_conv(proj, state_conv[0].reshape(n_s, 3 * CONV_DIM), conv_w[0], conv_b_r,
                                      dtb_r, alog_r, xexp)
    dtxt = dtx.reshape(n_s, SSD_HEADS, SSD_HEAD_DIM).transpose(0, 2, 1)
    s_state, yt = _s_state(e_s, state_ssd[0], dtxt, act_s[:, D_INNER:D_INNER + SSD_GROUPS * SSD_STATE],
                           act_s[:, D_INNER + SSD_GROUPS * SSD_STATE:])
    y_s = yt[:, :, :SSD_HEADS].transpose(0, 2, 1).reshape(n_s, D_INNER)
    q_s = proj[:, SEG_Q:SEG_Q + ATTN_WIDTH].reshape(n_s, N_KV, GQA, 1, HEAD_DIM)
    kv_eye = jnp.eye(N_KV, dtype=F32).reshape(1, N_KV, 1, N_KV, 1)
    qexp = (q_s * kv_eye).reshape(n_s, N_HEADS, KV_WIDTH)
    slope_row = _row(jnp.asarray(ALIBI_SLOPES, F32), LANES)
    att_s, s_wk, s_wv = _s_attn(qexp, cache_win_k[0].reshape(n_s, WINDOW, KV_WIDTH),
                                cache_win_v[0].reshape(n_s, WINDOW, KV_WIDTH),
                                proj[:, SEG_K:SEG_K + KV_WIDTH], proj[:, SEG_V:SEG_V + KV_WIDTH],
                                _row(attn_sinks[0], LANES), slope_row)
    h1_s, qc_s = _s_post(y_s, act_s, proj, att_s, xs2, dskip_x, g_ssd_r, w_down_bf, w_mix_bf, g_cross_r, w_cq_bf)
    o_s = _s_cross(qc_s, cache_mem_k[0].reshape(n_s, MEM_LEN, C_WIDTH), cache_mem_v[0].reshape(n_s, MEM_LEN, C_WIDTH))
    h2_s, xm_s, route_s = _s_route(o_s, h1_s, w_co_bf, g_moe_r, w_r, b_r)

    t_all = t_p + n_s
    t_pad = -(-t_all // RANK_TILE) * RANK_TILE
    route_all = jnp.concatenate([route_p, route_s, jnp.full((SUBLANES, t_pad - t_all), -1.0, F32)], axis=1)
    rank, counts = _rank(route_all)
    cnt = counts[:, 0].astype(jnp.int32)
    padded = (cnt + MOE_ROWS - 1) // MOE_ROWS * MOE_ROWS
    pad_end = jnp.cumsum(padded)
    offs = (pad_end - padded).astype(F32)
    nb = -(-(2 * t_all) // MOE_ROWS) + N_EXPERTS
    block_e = jnp.minimum(jnp.searchsorted(pad_end, jnp.arange(nb, dtype=jnp.int32) * MOE_ROWS, side='right'),
                          N_EXPERTS - 1).astype(jnp.int32)
    n_used = (pad_end[-1] // MOE_ROWS).astype(jnp.int32).reshape(1)
    dest = _dest(route_all, rank, jnp.broadcast_to(offs[:, None], (N_EXPERTS, LANES)))
    dest_p = dest[0:2, :t_p].T.reshape(-1)
    dest_s = dest[0:2, t_p:t_all].T.reshape(-1)
    xb = jnp.zeros((nb * MOE_ROWS, D_MODEL), F32)
    xb = _dispatch(dest_p, xm_p.reshape(t_p, D_MODEL), xb)
    xb = _dispatch(dest_s, xm_s, xb)
    yb = _experts(block_e, n_used, xb, w_e_gate[0].astype(BF16), w_e_up[0].astype(BF16), w_e_down[0].astype(BF16))
    y_p = _combine(dest_p, yb, h2_p.reshape(t_p, D_MODEL), route_p[2:4, :].T, g_final_r)
    y_smp = _combine(dest_s, yb, h2_s, route_s[2:4, :].T, g_final_r)

    return (y_p.reshape(b, seq, D_MODEL), y_smp.reshape(n_s, 1, D_MODEL),
            p_state.reshape(1, b, SSD_HEADS, SSD_HEAD_DIM, SSD_STATE), p_conv[None],
            p_wk.reshape(1, b, WINDOW, N_KV, HEAD_DIM), p_wv.reshape(1, b, WINDOW, N_KV, HEAD_DIM),
            mkv[:, :C_WIDTH].reshape(1, b, MEM_LEN, C_HEADS, C_HEAD_DIM),
            mkv[:, C_WIDTH:].reshape(1, b, MEM_LEN, C_HEADS, C_HEAD_DIM),
            s_state[None], s_conv.reshape(1, n_s, CONV_K - 1, CONV_DIM),
            s_wk.reshape(1, n_s, WINDOW, N_KV, HEAD_DIM), s_wv.reshape(1, n_s, WINDOW, N_KV, HEAD_DIM))
```

```python
import functools
import math

import jax
import jax.numpy as jnp
import numpy as np
from jax import lax
from jax.experimental import pallas as pl
from jax.experimental.pallas import tpu as pltpu

F32 = jnp.float32
BF16 = jnp.bfloat16
HIGHEST = lax.Precision.HIGHEST

D_MODEL = 1024
D_INNER = 2048
SSD_HEAD_DIM = 64
SSD_HEADS = 32
SSD_GROUPS = 4
SSD_HPG = 8
SSD_STATE = 128
CONV_K = 4
CONV_DIM = 3072
CHUNK = 128
HEAD_DIM = 64
N_HEADS = 16
N_KV = 4
GQA = 4
ATTN_WIDTH = 1024
KV_WIDTH = 256
WINDOW = 128
MEM_LEN = 256
C_HEADS = 4
C_HEAD_DIM = 128
C_WIDTH = 512
N_GROUPS = 4
EXP_PER_GROUP = 8
N_EXPERTS = 32
D_EXPERT = 512
EPS = 1e-6
NEG_INF = -1e30
LANES = 128
SUBLANES = 8

KV_DUP = 2 * KV_WIDTH
SEG_Z, SEG_XBC, SEG_Q, SEG_K, SEG_V, SEG_GS, SEG_GA, SEG_DT = (
    0, 2048, 5120, 6144, 6656, 7168, 8192, 9216)
PACKED_COLS = 9472
MASKED_DIST = 1e32
ROUTE_COLS = 128

MOE_ROWS = 256
ALIBI_SLOPES = tuple(2.0 ** (-8.0 * (h + 1) / N_HEADS) for h in range(N_HEADS))


def _tile(n, pref, mult=SUBLANES):
    if n <= pref:
        return n
    for t in range(pref, 0, -1):
        if n % t == 0 and t % mult == 0:
            return t
    return n


def _params(sem, vmem_mb):
    return pltpu.CompilerParams(dimension_semantics=sem, vmem_limit_bytes=vmem_mb * 1024 * 1024)


def _const_spec(shape):
    nd = len(shape)
    return pl.BlockSpec(shape, lambda *_: (0,) * nd, pipeline_mode=pl.Buffered(1))


def _sigmoid(x):
    return 1.0 / (1.0 + jnp.exp(-x))


def _silu(x):
    return x * _sigmoid(x)


def _softplus(x):
    return jnp.maximum(x, 0.0) + jnp.log1p(jnp.exp(-jnp.abs(x)))


def _rms(x, g):
    return x * lax.rsqrt(jnp.mean(x * x, axis=-1, keepdims=True) + EPS) * g


def _bf16_round(x):
    return x.astype(BF16).astype(F32)


def _dot(a, b, precision=None):
    return jnp.dot(a, b, preferred_element_type=F32, precision=precision)


def _dot_nt(a, b, precision=None):
    return lax.dot_general(a, b, (((1,), (1,)), ((), ())), preferred_element_type=F32, precision=precision)


def _inproj_body(x_ref, g_ref, w_ref, z_ref, xbc_ref, q_ref, k_ref, v_ref, gs_ref, ga_ref, dt_ref):
    xb = _rms(x_ref[...], g_ref[...]).astype(BF16)
    segs = ((z_ref, SEG_Z, D_INNER, 1.0), (xbc_ref, SEG_XBC, CONV_DIM, 1.0),
            (q_ref, SEG_Q, ATTN_WIDTH, HEAD_DIM ** -0.5), (k_ref, SEG_K, KV_DUP, 1.0),
            (v_ref, SEG_V, KV_DUP, 1.0), (gs_ref, SEG_GS, D_MODEL, 1.0), (ga_ref, SEG_GA, D_MODEL, 1.0),
            (dt_ref, SEG_DT, LANES, 1.0))
    step = 512
    for ref, off, width, scale in segs:
        for c0 in range(0, width, step):
            cw = min(step, width - c0)
            r = _dot(xb, w_ref[:, off + c0:off + c0 + cw])
            if scale != 1.0:
                r = r * scale
            ref[:, c0:c0 + cw] = r.astype(ref.dtype)


def _inproj(x2d, g_mix, w_packed):
    t = x2d.shape[0]
    tm = _tile(t, 512)
    widths = (D_INNER, CONV_DIM, ATTN_WIDTH, KV_DUP, KV_DUP, D_MODEL, D_MODEL)
    out_shape = [jax.ShapeDtypeStruct((t, w), BF16) for w in widths] + [jax.ShapeDtypeStruct((t, LANES), F32)]
    out_specs = [pl.BlockSpec((tm, w), lambda i: (i, 0)) for w in widths] + [pl.BlockSpec((tm, LANES), lambda i: (i, 0))]
    return pl.pallas_call(
        _inproj_body,
        grid=(t // tm,),
        in_specs=[pl.BlockSpec((tm, D_MODEL), lambda i: (i, 0)),
                  _const_spec((1, D_MODEL)),
                  _const_spec((D_MODEL, PACKED_COLS))],
        out_specs=out_specs,
        out_shape=out_shape,
        compiler_params=_params(("parallel",), 56),
        name="prompt_inproj",
    )(x2d, g_mix, w_packed)


def _mixer_body(xbc_ref, z_ref, dt_ref, q_ref, k_ref, v_ref, gs_ref, ga_ref, x_ref,
                convw_ref, convb_ref, dtb_ref, alog_ref, dskip_ref, gssd_ref, sinks_ref, wdown_ref, wmix_ref,
                h1_ref, pstate_ref, pconv_ref, pk_ref, pv_ref,
                cbuf, act, ybuf, att, kprev, vprev, state):
    c = pl.program_id(1)
    last = pl.num_programs(1) - 1
    L = CHUNK

    @pl.when(c == 0)
    def _():
        cbuf[0:SUBLANES, :] = jnp.zeros((SUBLANES, CONV_DIM), F32)
        kprev[...] = jnp.zeros_like(kprev)
        vprev[...] = jnp.zeros_like(vprev)
        state[...] = jnp.zeros_like(state)

    cbuf[SUBLANES:SUBLANES + L, :] = xbc_ref[...].astype(F32)
    cstep = 512
    for c0 in range(0, CONV_DIM, cstep):
        cs = slice(c0, c0 + cstep)
        acc = convb_ref[:, cs] + convw_ref[3:4, cs] * cbuf[8:8 + L, cs]
        acc = acc + convw_ref[2:3, cs] * cbuf[7:7 + L, cs]
        acc = acc + convw_ref[1:2, cs] * cbuf[6:6 + L, cs]
        acc = acc + convw_ref[0:1, cs] * cbuf[5:5 + L, cs]
        act[:, cs] = _silu(acc)

    @pl.when(c == last)
    def _():
        pconv_ref[...] = cbuf[SUBLANES + L - 3:SUBLANES + L, :]

    cbuf[0:SUBLANES, :] = cbuf[L:L + SUBLANES, :]

    dt = _softplus(dt_ref[...] + dtb_ref[...])
    a_neg = -jnp.exp(alog_ref[...])
    da = dt * a_neg
    ri = lax.broadcasted_iota(jnp.int32, (L, L), 0)
    ci = lax.broadcasted_iota(jnp.int32, (L, L), 1)
    causal = ri >= ci
    tri = jnp.where(causal, 1.0, 0.0).astype(F32)
    acum = _dot(tri, da, precision=HIGHEST)
    acum_t = acum.T
    dt_t = dt.T
    eacum = jnp.exp(acum)
    a_last = acum[L - 1:L, :]
    w_tail = jnp.exp(a_last - acum) * dt
    e_last = jnp.exp(a_last)
    lane = lax.broadcasted_iota(jnp.int32, (L, LANES), 1)
    lo_half = lane < SSD_HEAD_DIM
    lane1 = lax.broadcasted_iota(jnp.int32, (1, LANES), 1)
    lo_half1 = lane1 < SSD_HEAD_DIM

    for g in range(SSD_GROUPS):
        bg = act[:, D_INNER + g * SSD_STATE:D_INNER + (g + 1) * SSD_STATE]
        cg = act[:, D_INNER + (SSD_GROUPS + g) * SSD_STATE:D_INNER + (SSD_GROUPS + g + 1) * SSD_STATE]
        bgb = bg.astype(BF16)
        cgb = cg.astype(BF16)
        cb = _dot_nt(cgb, bgb)
        hg = state[g]
        yoff = _dot(cgb, hg.astype(BF16))
        bgt = bg.T.astype(BF16)
        for j in range(SSD_HPG // 2):
            h0 = g * SSD_HPG + 2 * j
            h1 = h0 + 1
            col = (g * SSD_HPG + 2 * j) * SSD_HEAD_DIM
            xs_pair = act[:, col:col + LANES]
            xpb = xs_pair.astype(BF16)
            ys = []
            for h in (h0, h1):
                seg = acum[:, h:h + 1] - acum_t[h:h + 1, :]
                decay = jnp.exp(jnp.where(causal, seg, NEG_INF))
                lmat = cb * decay * dt_t[h:h + 1, :]
                ys.append(_dot(lmat.astype(BF16), xpb))
            ydiag = jnp.where(lo_half, ys[0], ys[1])
            esc = jnp.where(lo_half, eacum[:, h0:h0 + 1], eacum[:, h1:h1 + 1])
            ybuf[:, col:col + LANES] = ydiag + yoff[:, 2 * j * SSD_HEAD_DIM:2 * j * SSD_HEAD_DIM + LANES] * esc
            wsc = jnp.where(lo_half, w_tail[:, h0:h0 + 1], w_tail[:, h1:h1 + 1])
            upd = _dot(bgt, (xs_pair * wsc).astype(BF16))
            dsc = jnp.where(lo_half1, e_last[:, h0:h0 + 1], e_last[:, h1:h1 + 1])
            pc = slice(2 * j * SSD_HEAD_DIM, 2 * j * SSD_HEAD_DIM + LANES)
            state[g, :, pc] = state[g, :, pc] * dsc + upd

    @pl.when(c == last)
    def _():
        for g in range(SSD_GROUPS):
            pstate_ref[g] = state[g].T

    gw = D_INNER // SSD_GROUPS
    for g in range(SSD_GROUPS):
        gs_ = slice(g * gw, (g + 1) * gw)
        yg = (ybuf[:, gs_] + dskip_ref[:, gs_] * act[:, gs_]) * _silu(z_ref[:, gs_].astype(F32))
        ybuf[:, gs_] = _rms(yg, gssd_ref[:, gs_])
    y_ssd = _dot(ybuf[...].astype(BF16), wdown_ref[...])

    nd_c = jnp.where(causal, (ci - ri).astype(F32), -MASKED_DIST)
    nd_p = jnp.where(ci >= ri + jnp.where(c > 0, 0, L), (ci - ri - L).astype(F32), -MASKED_DIST)
    keep = (jnp.where(lo_half1, 1.0, 0.0).astype(BF16), jnp.where(lo_half1, 0.0, 1.0).astype(BF16))
    for kv in range(N_KV):
        kd_p = kprev[:, kv * LANES:(kv + 1) * LANES]
        kd_c = k_ref[:, kv * LANES:(kv + 1) * LANES]
        vd_p = vprev[:, kv * LANES:(kv + 1) * LANES]
        vd_c = v_ref[:, kv * LANES:(kv + 1) * LANES]
        heads = tuple(range(kv * GQA, (kv + 1) * GQA))
        pcs = [slice((kv * GQA + 2 * j) * HEAD_DIM, (kv * GQA + 2 * j) * HEAD_DIM + LANES) for j in range(GQA // 2)]
        qs = jnp.concatenate([q_ref[:, pcs[h % GQA // 2]] * keep[h % 2] for h in heads], axis=0)
        s_p = _dot_nt(qs, kd_p) + jnp.concatenate([ALIBI_SLOPES[h] * nd_p for h in heads], axis=0)
        s_c = _dot_nt(qs, kd_c) + jnp.concatenate([ALIBI_SLOPES[h] * nd_c for h in heads], axis=0)
        sink = jnp.concatenate([jnp.full((L, 1), sinks_ref[h], F32) for h in heads], axis=0)
        m = jnp.maximum(jnp.max(jnp.maximum(s_p, s_c), axis=-1, keepdims=True), sink)
        p_p = jnp.exp(s_p - m)
        p_c = jnp.exp(s_c - m)
        denom = jnp.sum(p_p + p_c, axis=-1, keepdims=True) + jnp.exp(sink - m)
        o = (_dot(p_p.astype(BF16), vd_p) + _dot(p_c.astype(BF16), vd_c)) / denom
        for j in range(GQA // 2):
            att[:, pcs[j]] = jnp.where(lo_half, o[2 * j * L:(2 * j + 1) * L], o[(2 * j + 1) * L:(2 * j + 2) * L])
    kprev[...] = k_ref[...]
    vprev[...] = v_ref[...]

    @pl.when(c == last)
    def _():
        pk_ref[...] = k_ref[...].astype(F32)
        pv_ref[...] = v_ref[...].astype(F32)

    merged = _sigmoid(gs_ref[...].astype(F32)) * y_ssd + _sigmoid(ga_ref[...].astype(F32)) * att[...]
    h1_ref[...] = x_ref[...] + _dot(merged.astype(BF16), wmix_ref[...])


def _mixer(x, z, xbc, q, k, v, gs, ga, dt, conv_w, conv_b, dtb, alog, dskip_x, g_ssd, sinks, w_down, w_mix):
    b, seq, _ = x.shape
    nc = seq // CHUNK

    def blk(width):
        return pl.BlockSpec((None, CHUNK, width), lambda i, j: (i, j, 0))

    def per_b(*shape):
        nd = len(shape)
        return pl.BlockSpec((None,) + shape, lambda i, j: (i,) + (0,) * nd)

    in_specs = [blk(CONV_DIM), blk(D_INNER), blk(LANES), blk(ATTN_WIDTH), blk(KV_DUP), blk(KV_DUP),
                blk(D_MODEL), blk(D_MODEL), blk(D_MODEL),
                _const_spec((CONV_K, CONV_DIM)), _const_spec((1, CONV_DIM)), _const_spec((1, LANES)),
                _const_spec((1, LANES)), _const_spec((1, D_INNER)), _const_spec((1, D_INNER)),
                pl.BlockSpec(memory_space=pltpu.SMEM),
                _const_spec((D_INNER, D_MODEL)), _const_spec((D_MODEL, D_MODEL))]
    out_shape = [jax.ShapeDtypeStruct((b, seq, D_MODEL), F32),
                 jax.ShapeDtypeStruct((b, SSD_GROUPS, SSD_HPG * SSD_HEAD_DIM, SSD_STATE), F32),
                 jax.ShapeDtypeStruct((b, CONV_K - 1, CONV_DIM), F32),
                 jax.ShapeDtypeStruct((b, WINDOW, KV_DUP), F32),
                 jax.ShapeDtypeStruct((b, WINDOW, KV_DUP), F32)]
    out_specs = [blk(D_MODEL), per_b(SSD_GROUPS, SSD_HPG * SSD_HEAD_DIM, SSD_STATE), per_b(CONV_K - 1, CONV_DIM),
                 per_b(WINDOW, KV_DUP), per_b(WINDOW, KV_DUP)]
    scratch = [pltpu.VMEM((CHUNK + 2 * SUBLANES, CONV_DIM), F32),
               pltpu.VMEM((CHUNK, CONV_DIM), F32),
               pltpu.VMEM((CHUNK, D_INNER), F32),
               pltpu.VMEM((CHUNK, ATTN_WIDTH), F32),
               pltpu.VMEM((CHUNK, KV_DUP), BF16), pltpu.VMEM((CHUNK, KV_DUP), BF16),
               pltpu.VMEM((SSD_GROUPS, SSD_STATE, SSD_HPG * SSD_HEAD_DIM), F32)]
    return pl.pallas_call(
        _mixer_body,
        grid=(b, nc),
        in_specs=in_specs,
        out_specs=out_specs,
        out_shape=out_shape,
        scratch_shapes=scratch,
        compiler_params=_params(("parallel", "arbitrary"), 48),
        name="prompt_mixer",
    )(xbc, z, dt, q, k, v, gs, ga, x, conv_w, conv_b, dtb, alog, dskip_x, g_ssd, sinks, w_down, w_mix)


def _memkv_body(m_ref, g_ref, w_ref, o_ref):
    o_ref[...] = _dot(_rms(m_ref[...], g_ref[...]).astype(BF16), w_ref[...])


def _memkv(mem2d, g_mem, w_ckv):
    t = mem2d.shape[0]
    tm = _tile(t, 256)
    return pl.pallas_call(
        _memkv_body,
        grid=(t // tm,),
        in_specs=[pl.BlockSpec((tm, D_MODEL), lambda i: (i, 0)), _const_spec((1, D_MODEL)),
                  _const_spec((D_MODEL, 2 * C_WIDTH))],
        out_specs=pl.BlockSpec((tm, 2 * C_WIDTH), lambda i: (i, 0)),
        out_shape=jax.ShapeDtypeStruct((t, 2 * C_WIDTH), F32),
        compiler_params=_params(("parallel",), 32),
        name="memory_kv",
    )(mem2d, g_mem, w_ckv)


def _route(logits):
    rows = logits.shape[0]
    lane = lax.broadcasted_iota(jnp.int32, (rows, ROUTE_COLS), 1).astype(F32)
    big = 1e9
    is_g = lane < N_GROUPS
    lg = jnp.where(is_g, logits, NEG_INF)
    gmax = jnp.max(lg, axis=-1, keepdims=True)
    grp = jnp.min(jnp.where(lg == gmax, lane, big), axis=-1, keepdims=True)
    p_grp = 1.0 / jnp.sum(jnp.where(is_g, jnp.exp(lg - gmax), 0.0), axis=-1, keepdims=True)
    lo = N_GROUPS + EXP_PER_GROUP * grp
    in_grp = (lane >= lo) & (lane < lo + EXP_PER_GROUP)
    le = jnp.where(in_grp, logits, NEG_INF)
    m1 = jnp.max(le, axis=-1, keepdims=True)
    i1 = jnp.min(jnp.where(le == m1, lane, big), axis=-1, keepdims=True)
    le2 = jnp.where(lane == i1, NEG_INF, le)
    m2 = jnp.max(le2, axis=-1, keepdims=True)
    i2 = jnp.min(jnp.where(le2 == m2, lane, big), axis=-1, keepdims=True)
    t2 = jnp.exp(m2 - m1)
    g1 = p_grp / (1.0 + t2)
    g2 = p_grp * t2 / (1.0 + t2)
    info = jnp.where(lane == 0, i1 - N_GROUPS,
                     jnp.where(lane == 1, i2 - N_GROUPS,
                               jnp.where(lane == 2, g1, jnp.where(lane == 3, g2, 0.0))))
    return info


def _cross_body(h1_ref, mkv_ref, gc_ref, wcq_ref, wco_ref, gm_ref, wr_ref, br_ref,
                h2_ref, xm_ref, route_ref, obuf):
    h1 = h1_ref[...]
    xn = _rms(h1, gc_ref[...]).astype(BF16)
    qc = _dot(xn, wcq_ref[...])
    scale = C_HEAD_DIM ** -0.5
    for h in range(C_HEADS):
        hs = slice(h * C_HEAD_DIM, (h + 1) * C_HEAD_DIM)
        mk = mkv_ref[:, hs].astype(BF16)
        mv = mkv_ref[:, C_WIDTH + h * C_HEAD_DIM:C_WIDTH + (h + 1) * C_HEAD_DIM].astype(BF16)
        s = _dot_nt(qc[:, hs].astype(BF16), mk) * scale
        m = jnp.max(s, axis=-1, keepdims=True)
        p = jnp.exp(s - m)
        obuf[:, hs] = _dot(p.astype(BF16), mv) / jnp.sum(p, axis=-1, keepdims=True)
    h2 = h1 + _dot(obuf[...].astype(BF16), wco_ref[...])
    h2_ref[...] = h2
    xm = _rms(h2, gm_ref[...])
    xm_ref[...] = xm
    logits = _dot(xm.astype(BF16), wr_ref[...]) + br_ref[...]
    route_ref[...] = _route(logits).T[0:SUBLANES, :]


def _cross(h1, mkv, g_cross, w_cq, w_co, g_moe, w_r, b_r):
    b, seq, _ = h1.shape
    tq = _tile(seq, 512)
    nq = seq // tq
    return pl.pallas_call(
        _cross_body,
        grid=(b, nq),
        in_specs=[pl.BlockSpec((None, tq, D_MODEL), lambda i, j: (i, j, 0)),
                  pl.BlockSpec((None, MEM_LEN, 2 * C_WIDTH), lambda i, j: (i, 0, 0)),
                  _const_spec((1, D_MODEL)), _const_spec((D_MODEL, C_WIDTH)), _const_spec((C_WIDTH, D_MODEL)),
                  _const_spec((1, D_MODEL)), _const_spec((D_MODEL, ROUTE_COLS)), _const_spec((1, ROUTE_COLS))],
        out_specs=[pl.BlockSpec((None, tq, D_MODEL), lambda i, j: (i, j, 0)),
                   pl.BlockSpec((None, tq, D_MODEL), lambda i, j: (i, j, 0)),
                   pl.BlockSpec((SUBLANES, tq), lambda i, j: (0, i * nq + j))],
        out_shape=[jax.ShapeDtypeStruct((b, seq, D_MODEL), F32),
                   jax.ShapeDtypeStruct((b, seq, D_MODEL), F32),
                   jax.ShapeDtypeStruct((SUBLANES, b * seq), F32)],
        scratch_shapes=[pltpu.VMEM((tq, C_WIDTH), F32)],
        compiler_params=_params(("parallel", "parallel"), 48),
        name="prompt_cross_route",
    )(h1, mkv, g_cross, w_cq, w_co, g_moe, w_r, b_r)


RANK_TILE = 512


def _rank_body(route_ref, rank_ref, count_ref, carry):
    i = pl.program_id(0)

    @pl.when(i == 0)
    def _():
        carry[...] = jnp.zeros_like(carry)

    e1 = route_ref[0:1, :]
    e2 = route_ref[1:2, :]
    eid = lax.broadcasted_iota(jnp.int32, (N_EXPERTS, RANK_TILE), 0).astype(F32)
    is1 = e1 == eid
    is2 = e2 == eid
    onehot = jnp.where(is1 | is2, 1.0, 0.0)
    si = lax.broadcasted_iota(jnp.int32, (RANK_TILE, RANK_TILE), 0)
    ti = lax.broadcasted_iota(jnp.int32, (RANK_TILE, RANK_TILE), 1)
    before = jnp.where(si < ti, 1.0, 0.0).astype(BF16)
    prefix = _dot(onehot.astype(BF16), before) + carry[:, 0:1]
    r1 = jnp.sum(jnp.where(is1, prefix, 0.0), axis=0, keepdims=True)
    r2 = jnp.sum(jnp.where(is2, prefix, 0.0), axis=0, keepdims=True)
    row = lax.broadcasted_iota(jnp.int32, (SUBLANES, RANK_TILE), 0)
    rank_ref[...] = jnp.where(row == 0, r1, jnp.where(row == 1, r2, 0.0))
    carry[...] = carry[...] + jnp.sum(onehot, axis=1, keepdims=True)
    count_ref[...] = carry[...]


def _rank(route):
    tp = route.shape[1]
    return pl.pallas_call(
        _rank_body,
        grid=(tp // RANK_TILE,),
        in_specs=[pl.BlockSpec((SUBLANES, RANK_TILE), lambda i: (0, i))],
        out_specs=[pl.BlockSpec((SUBLANES, RANK_TILE), lambda i: (0, i)),
                   pl.BlockSpec((N_EXPERTS, LANES), lambda i: (0, 0))],
        out_shape=[jax.ShapeDtypeStruct((SUBLANES, tp), F32), jax.ShapeDtypeStruct((N_EXPERTS, LANES), F32)],
        scratch_shapes=[pltpu.VMEM((N_EXPERTS, LANES), F32)],
        compiler_params=_params(("arbitrary",), 32),
        name="moe_rank",
    )(route)


def _dest_body(route_ref, rank_ref, offs_ref, dest_ref):
    e1 = route_ref[0:1, :]
    e2 = route_ref[1:2, :]
    eid = lax.broadcasted_iota(jnp.int32, (N_EXPERTS, RANK_TILE), 0).astype(F32)
    offs = offs_ref[:, 0:1]
    d1 = jnp.sum(jnp.where(e1 == eid, offs, 0.0), axis=0, keepdims=True) + rank_ref[0:1, :]
    d2 = jnp.sum(jnp.where(e2 == eid, offs, 0.0), axis=0, keepdims=True) + rank_ref[1:2, :]
    row = lax.broadcasted_iota(jnp.int32, (SUBLANES, RANK_TILE), 0)
    dest_ref[...] = jnp.where(row == 0, d1, jnp.where(row == 1, d2, 0.0)).astype(jnp.int32)


def _dest(route, rank, offs):
    tp = route.shape[1]
    return pl.pallas_call(
        _dest_body,
        grid=(tp // RANK_TILE,),
        in_specs=[pl.BlockSpec((SUBLANES, RANK_TILE), lambda i: (0, i)),
                  pl.BlockSpec((SUBLANES, RANK_TILE), lambda i: (0, i)),
                  pl.BlockSpec((N_EXPERTS, LANES), lambda i: (0, 0))],
        out_specs=pl.BlockSpec((SUBLANES, RANK_TILE), lambda i: (0, i)),
        out_shape=jax.ShapeDtypeStruct((SUBLANES, tp), jnp.int32),
        compiler_params=_params(("parallel",), 32),
        name="moe_dest",
    )(route, rank, offs)


DMA_UNROLL = 8


def _row_copy(src, dst, s_row, d_row, sem):
    return pltpu.make_async_copy(src.at[pl.ds(s_row, 1)], dst.at[pl.ds(d_row, 1)], sem)


def _dispatch_body(td, dest_ref, xm_ref, xb_in_ref, xb_ref, sem):
    del xb_in_ref
    base = pl.program_id(0) * td

    def issue(t, carry):
        row = base + t
        _row_copy(xm_ref, xb_ref, t, dest_ref[2 * row], sem).start()
        _row_copy(xm_ref, xb_ref, t, dest_ref[2 * row + 1], sem).start()
        return carry

    lax.fori_loop(0, td, issue, 0, unroll=DMA_UNROLL)

    def drain(t, carry):
        _row_copy(xm_ref, xb_ref, 0, 0, sem).wait()
        _row_copy(xm_ref, xb_ref, 0, 0, sem).wait()
        return carry

    lax.fori_loop(0, td, drain, 0, unroll=DMA_UNROLL)


def _dispatch(dest_flat, xm2d, xb):
    t = xm2d.shape[0]
    td = _tile(t, 256)
    return pl.pallas_call(
        functools.partial(_dispatch_body, td),
        grid_spec=pltpu.PrefetchScalarGridSpec(
            num_scalar_prefetch=1,
            grid=(t // td,),
            in_specs=[pl.BlockSpec((td, D_MODEL), lambda i, d: (i, 0)), pl.BlockSpec(memory_space=pl.ANY)],
            out_specs=pl.BlockSpec(memory_space=pl.ANY),
            scratch_shapes=[pltpu.SemaphoreType.DMA],
        ),
        out_shape=jax.ShapeDtypeStruct(xb.shape, xb.dtype),
        input_output_aliases={2: 0},
        compiler_params=_params(("arbitrary",), 32),
        name="moe_dispatch",
    )(dest_flat, xm2d, xb)


def _expert_body(be_ref, nused_ref, xb_ref, wg_ref, wu_ref, wd_ref, yb_ref):
    i = pl.program_id(0)

    @pl.when(i < nused_ref[0])
    def _():
        x = xb_ref[...].astype(BF16)
        hmid = _silu(_dot(x, wg_ref[...])) * _dot(x, wu_ref[...])
        yb_ref[...] = _dot(hmid.astype(BF16), wd_ref[...])

    @pl.when(i >= nused_ref[0])
    def _():
        yb_ref[...] = jnp.zeros_like(yb_ref)


def _experts(block_e, n_used, xb, wg, wu, wd):
    rows = xb.shape[0]
    nb = rows // MOE_ROWS

    def xmap(i, be, nu):
        return (jnp.minimum(i, nu[0] - 1), 0)

    def wmap(i, be, nu):
        return (be[i], 0, 0)

    return pl.pallas_call(
        _expert_body,
        grid_spec=pltpu.PrefetchScalarGridSpec(
            num_scalar_prefetch=2,
            grid=(nb,),
            in_specs=[pl.BlockSpec((MOE_ROWS, D_MODEL), xmap),
                      pl.BlockSpec((None, D_MODEL, D_EXPERT), wmap),
                      pl.BlockSpec((None, D_MODEL, D_EXPERT), wmap),
                      pl.BlockSpec((None, D_EXPERT, D_MODEL), wmap)],
            out_specs=pl.BlockSpec((MOE_ROWS, D_MODEL), lambda i, be, nu: (i, 0)),
        ),
        out_shape=jax.ShapeDtypeStruct((rows, D_MODEL), F32),
        compiler_params=_params(("arbitrary",), 40),
        name="moe_experts",
    )(block_e, n_used, xb, wg, wu, wd)


def _combine_body(tc, dest_ref, yb_ref, h2_ref, gate_ref, gf_ref, y_ref, buf, sem):
    base = pl.program_id(0) * tc

    def issue(t, carry):
        row = base + t
        pltpu.make_async_copy(yb_ref.at[pl.ds(dest_ref[2 * row], 1)], buf.at[0, pl.ds(t, 1)], sem).start()
        pltpu.make_async_copy(yb_ref.at[pl.ds(dest_ref[2 * row + 1], 1)], buf.at[1, pl.ds(t, 1)], sem).start()
        return carry

    lax.fori_loop(0, tc, issue, 0, unroll=DMA_UNROLL)

    def drain(t, carry):
        pltpu.make_async_copy(yb_ref.at[pl.ds(0, 1)], buf.at[0, pl.ds(0, 1)], sem).wait()
        pltpu.make_async_copy(yb_ref.at[pl.ds(0, 1)], buf.at[1, pl.ds(0, 1)], sem).wait()
        return carry

    lax.fori_loop(0, tc, drain, 0, unroll=DMA_UNROLL)
    g = gate_ref[...]
    out = h2_ref[...] + (g[:, 0:1] * buf[0] + g[:, 1:2] * buf[1])
    y_ref[...] = _rms(out, gf_ref[...])


def _combine(dest_flat, yb, h2_2d, gates, g_final):
    t = h2_2d.shape[0]
    tc = _tile(t, 256)
    return pl.pallas_call(
        functools.partial(_combine_body, tc),
        grid_spec=pltpu.PrefetchScalarGridSpec(
            num_scalar_prefetch=1,
            grid=(t // tc,),
            in_specs=[pl.BlockSpec(memory_space=pl.ANY),
                      pl.BlockSpec((tc, D_MODEL), lambda i, d: (i, 0)),
                      pl.BlockSpec((tc, 2), lambda i, d: (i, 0)),
                      pl.BlockSpec((1, D_MODEL), lambda i, d: (0, 0))],
            out_specs=pl.BlockSpec((tc, D_MODEL), lambda i, d: (i, 0)),
            scratch_shapes=[pltpu.VMEM((2, tc, D_MODEL), F32), pltpu.SemaphoreType.DMA],
        ),
        out_shape=jax.ShapeDtypeStruct((t, D_MODEL), F32),
        compiler_params=_params(("arbitrary",), 32),
        name="moe_combine",
    )(dest_flat, yb, h2_2d, gates, g_final)


def _s_inproj_body(x_ref, g_ref, w_ref, o_ref):
    o_ref[...] = _dot(_rms(x_ref[...], g_ref[...]).astype(BF16), w_ref[...])


def _s_inproj(x, g_mix, w_packed_f32):
    n = x.shape[0]
    tn = PACKED_COLS // 2
    return pl.pallas_call(
        _s_inproj_body,
        grid=(PACKED_COLS // tn,),
        in_specs=[pl.BlockSpec((n, D_MODEL), lambda j: (0, 0)), pl.BlockSpec((1, D_MODEL), lambda j: (0, 0)),
                  pl.BlockSpec((D_MODEL, tn), lambda j: (0, j))],
        out_specs=pl.BlockSpec((n, tn), lambda j: (0, j)),
        out_shape=jax.ShapeDtypeStruct((n, PACKED_COLS), F32),
        compiler_params=_params(("parallel",), 48),
        name="sample_inproj",
    )(x, g_mix, w_packed_f32)


def _s_conv_body(proj_ref, cs_ref, convw_ref, convb_ref, dtb_ref, alog_ref, xexp_ref,
                 act_ref, ncs_ref, dtx_ref, e_ref):
    step = 512
    for c0 in range(0, CONV_DIM, step):
        cs = slice(c0, c0 + step)
        s0 = cs_ref[:, c0:c0 + step]
        s1 = cs_ref[:, CONV_DIM + c0:CONV_DIM + c0 + step]
        s2 = cs_ref[:, 2 * CONV_DIM + c0:2 * CONV_DIM + c0 + step]
        xn = proj_ref[:, SEG_XBC + c0:SEG_XBC + c0 + step]
        acc = convb_ref[:, cs] + convw_ref[0:1, cs] * s0
        acc = acc + convw_ref[1:2, cs] * s1
        acc = acc + convw_ref[2:3, cs] * s2
        acc = acc + convw_ref[3:4, cs] * xn
        act_ref[:, cs] = _silu(acc)
        ncs_ref[:, c0:c0 + step] = s1
        ncs_ref[:, CONV_DIM + c0:CONV_DIM + c0 + step] = s2
        ncs_ref[:, 2 * CONV_DIM + c0:2 * CONV_DIM + c0 + step] = xn
    dt = _softplus(proj_ref[:, SEG_DT:SEG_DT + LANES] + dtb_ref[...])
    e_ref[...] = jnp.exp(dt * (-jnp.exp(alog_ref[...])))
    dtx_ref[...] = _dot(dt, xexp_ref[...], precision=HIGHEST) * act_ref[:, 0:D_INNER]


def _s_conv(proj, conv_state2d, conv_w, conv_b, dtb, alog, xexp):
    n = proj.shape[0]
    return pl.pallas_call(
        _s_conv_body,
        out_shape=[jax.ShapeDtypeStruct((n, CONV_DIM), F32), jax.ShapeDtypeStruct((n, 3 * CONV_DIM), F32),
                   jax.ShapeDtypeStruct((n, D_INNER), F32), jax.ShapeDtypeStruct((n, LANES), F32)],
        compiler_params=pltpu.CompilerParams(vmem_limit_bytes=48 * 1024 * 1024),
        name="sample_conv",
    )(proj, conv_state2d, conv_w, conv_b, dtb, alog, xexp)


def _s_state_body(bb, e_ref, st_ref, dtxt_ref, b_ref, c_ref, so_ref, yt_ref):
    base = pl.program_id(0) * bb
    lane = lax.broadcasted_iota(jnp.int32, (SSD_HEAD_DIM, LANES), 1)

    def one(bl, carry):
        yacc = jnp.zeros((SSD_HEAD_DIM, LANES), F32)
        for h in range(SSD_HEADS):
            g = h // SSD_HPG
            brow = b_ref[bl, :, g * SSD_STATE:(g + 1) * SSD_STATE]
            crow = c_ref[bl, :, g * SSD_STATE:(g + 1) * SSD_STATE]
            xcol = dtxt_ref[bl, :, h:h + 1]
            hn = st_ref[bl, h] * e_ref[base + bl, h] + xcol * brow
            so_ref[bl, h] = hn
            ycol = jnp.sum(hn * crow, axis=-1, keepdims=True)
            yacc = jnp.where(lane == h, ycol, yacc)
        yt_ref[bl] = yacc
        return carry

    lax.fori_loop(0, bb, one, 0)


def _s_state(e, state, dtxt, bmat, cmat):
    n = state.shape[0]
    bb = _tile(n, 4, 1)
    sblk = (bb, SSD_HEADS, SSD_HEAD_DIM, SSD_STATE)
    bmat = bmat.reshape(n, 1, SSD_GROUPS * SSD_STATE)
    cmat = cmat.reshape(n, 1, SSD_GROUPS * SSD_STATE)
    return pl.pallas_call(
        functools.partial(_s_state_body, bb),
        grid=(n // bb,),
        in_specs=[pl.BlockSpec(memory_space=pltpu.SMEM),
                  pl.BlockSpec(sblk, lambda i: (i, 0, 0, 0)),
                  pl.BlockSpec((bb, SSD_HEAD_DIM, SSD_HEADS), lambda i: (i, 0, 0)),
                  pl.BlockSpec((bb, 1, SSD_GROUPS * SSD_STATE), lambda i: (i, 0, 0)),
                  pl.BlockSpec((bb, 1, SSD_GROUPS * SSD_STATE), lambda i: (i, 0, 0))],
        out_specs=[pl.BlockSpec(sblk, lambda i: (i, 0, 0, 0)),
                   pl.BlockSpec((bb, SSD_HEAD_DIM, LANES), lambda i: (i, 0, 0))],
        out_shape=[jax.ShapeDtypeStruct(state.shape, F32), jax.ShapeDtypeStruct((n, SSD_HEAD_DIM, LANES), F32)],
        compiler_params=_params(("parallel",), 40),
        name="sample_ssd_state",
    )(e, state, dtxt, bmat, cmat)


def _s_attn_body(bb, qexp_ref, ck_ref, cv_ref, kn_ref, vn_ref, sink_ref, slope_ref,
                 y_ref, ok_ref, ov_ref):
    W = WINDOW
    lane = lax.broadcasted_iota(jnp.int32, (W, LANES), 1)
    lane1 = lax.broadcasted_iota(jnp.int32, (1, LANES), 1)
    jrow = lax.broadcasted_iota(jnp.int32, (W, LANES), 0)
    bias = slope_ref[...] * (W - jrow).astype(F32)
    sink = sink_ref[...]
    lo_half = lane1 < HEAD_DIM

    def one(bl, carry):
        kn = kn_ref[bl]
        vn = vn_ref[bl]
        kmat = _bf16_round(ck_ref[bl])
        vmat = _bf16_round(cv_ref[bl])
        knr = _bf16_round(kn)
        vnr = _bf16_round(vn)
        s = jnp.zeros((W, LANES), F32)
        sn = jnp.zeros((1, LANES), F32)
        for h in range(N_HEADS):
            qrow = _bf16_round(qexp_ref[bl, h:h + 1, :])
            s = jnp.where(lane == h, jnp.sum(kmat * qrow, axis=-1, keepdims=True), s)
            sn = jnp.where(lane1 == h, jnp.sum(knr * qrow, axis=-1, keepdims=True), sn)
        s = s * (HEAD_DIM ** -0.5) - bias
        sn = sn * (HEAD_DIM ** -0.5)
        m = jnp.maximum(jnp.maximum(jnp.max(s, axis=0, keepdims=True), sn), sink)
        p = jnp.exp(s - m)
        pn = jnp.exp(sn - m)
        denom = jnp.sum(p, axis=0, keepdims=True) + pn + jnp.exp(sink - m)
        p = _bf16_round(p / denom)
        pn = _bf16_round(pn / denom)
        for j in range(N_HEADS // 2):
            outs = []
            for h in (2 * j, 2 * j + 1):
                kv = h // GQA
                cs = slice((kv // 2) * LANES, (kv // 2 + 1) * LANES)
                o = jnp.sum(p[:, h:h + 1] * vmat[:, cs], axis=0, keepdims=True) + pn[:, h:h + 1] * vnr[:, cs]
                if (kv % 2) != (h % 2):
                    o = pltpu.roll(o, HEAD_DIM, axis=1)
                outs.append(o)
            y_ref[bl, :, j * LANES:(j + 1) * LANES] = jnp.where(lo_half, outs[0], outs[1])
        ok_ref[bl, 0:W - 1, :] = ck_ref[bl, 1:W, :]
        ok_ref[bl, W - 1:W, :] = kn
        ov_ref[bl, 0:W - 1, :] = cv_ref[bl, 1:W, :]
        ov_ref[bl, W - 1:W, :] = vn
        return carry

    lax.fori_loop(0, bb, one, 0)


def _s_attn(qexp, ck, cv, kn, vn, sink_row, slope_row):
    n = ck.shape[0]
    bb = _tile(n, 8, 1)
    cblk = pl.BlockSpec((bb, WINDOW, KV_WIDTH), lambda i: (i, 0, 0))
    rblk = pl.BlockSpec((bb, 1, KV_WIDTH), lambda i: (i, 0, 0))
    y, ok, ov = pl.pallas_call(
        functools.partial(_s_attn_body, bb),
        grid=(n // bb,),
        in_specs=[pl.BlockSpec((bb, N_HEADS, KV_WIDTH), lambda i: (i, 0, 0)), cblk, cblk, rblk, rblk,
                  pl.BlockSpec((1, LANES), lambda i: (0, 0)), pl.BlockSpec((1, LANES), lambda i: (0, 0))],
        out_specs=[pl.BlockSpec((bb, 1, ATTN_WIDTH), lambda i: (i, 0, 0)), cblk, cblk],
        out_shape=[jax.ShapeDtypeStruct((n, 1, ATTN_WIDTH), F32), jax.ShapeDtypeStruct(ck.shape, F32),
                   jax.ShapeDtypeStruct(cv.shape, F32)],
        compiler_params=_params(("parallel",), 32),
        name="sample_window_attn",
    )(qexp, ck, cv, kn.reshape(n, 1, KV_WIDTH), vn.reshape(n, 1, KV_WIDTH), sink_row, slope_row)
    return y.reshape(n, ATTN_WIDTH), ok, ov


def _s_post_body(y_ref, act_ref, proj_ref, att_ref, x_ref, dskip_ref, gssd_ref, wdown_ref, wmix_ref,
                 gc_ref, wcq_ref, h1_ref, qc_ref, ybuf):
    gw = D_INNER // SSD_GROUPS
    for g in range(SSD_GROUPS):
        gs_ = slice(g * gw, (g + 1) * gw)
        yg = (y_ref[:, gs_] + dskip_ref[:, gs_] * act_ref[:, gs_]) * _silu(proj_ref[:, SEG_Z + g * gw:SEG_Z + (g + 1) * gw])
        ybuf[:, gs_] = _rms(yg, gssd_ref[:, gs_])
    y_ssd = _dot(ybuf[...].astype(BF16), wdown_ref[...])
    merged = (_sigmoid(proj_ref[:, SEG_GS:SEG_GS + D_MODEL]) * y_ssd
              + _sigmoid(proj_ref[:, SEG_GA:SEG_GA + D_MODEL]) * att_ref[...])
    h1 = x_ref[...] + _dot(merged.astype(BF16), wmix_ref[...])
    h1_ref[...] = h1
    qc_ref[...] = _bf16_round(_dot(_rms(h1, gc_ref[...]).astype(BF16), wcq_ref[...]))


def _s_post(y, act, proj, att, x, dskip_x, g_ssd, w_down, w_mix, g_cross, w_cq):
    n = x.shape[0]
    return pl.pallas_call(
        _s_post_body,
        out_shape=[jax.ShapeDtypeStruct((n, D_MODEL), F32), jax.ShapeDtypeStruct((n, C_WIDTH), F32)],
        scratch_shapes=[pltpu.VMEM((n, D_INNER), F32)],
        compiler_params=pltpu.CompilerParams(vmem_limit_bytes=48 * 1024 * 1024),
        name="sample_post_mixer",
    )(y, act, proj, att, x, dskip_x, g_ssd, w_down, w_mix, g_cross, w_cq)


def _s_cross_body(bb, qc_ref, mk_ref, mv_ref, o_ref):
    lane = lax.broadcasted_iota(jnp.int32, (MEM_LEN, LANES), 1)
    scale = C_HEAD_DIM ** -0.5

    def one(bl, carry):
        kmat = _bf16_round(mk_ref[bl])
        vmat = _bf16_round(mv_ref[bl])
        q = qc_ref[bl]
        s = jnp.zeros((MEM_LEN, LANES), F32)
        for h in range(C_HEADS):
            hs = slice(h * C_HEAD_DIM, (h + 1) * C_HEAD_DIM)
            s = jnp.where(lane == h, jnp.sum(kmat[:, hs] * q[:, hs], axis=-1, keepdims=True), s)
        s = s * scale
        m = jnp.max(s, axis=0, keepdims=True)
        p = jnp.exp(s - m)
        p = _bf16_round(p / jnp.sum(p, axis=0, keepdims=True))
        for h in range(C_HEADS):
            hs = slice(h * C_HEAD_DIM, (h + 1) * C_HEAD_DIM)
            o_ref[bl, :, hs] = jnp.sum(p[:, h:h + 1] * vmat[:, hs], axis=0, keepdims=True)
        return carry

    lax.fori_loop(0, bb, one, 0)


def _s_cross(qc, mk, mv):
    n = qc.shape[0]
    bb = _tile(n, 8, 1)
    mblk = pl.BlockSpec((bb, MEM_LEN, C_WIDTH), lambda i: (i, 0, 0))
    rblk = pl.BlockSpec((bb, 1, C_WIDTH), lambda i: (i, 0, 0))
    return pl.pallas_call(
        functools.partial(_s_cross_body, bb),
        grid=(n // bb,),
        in_specs=[rblk, mblk, mblk],
        out_specs=rblk,
        out_shape=jax.ShapeDtypeStruct((n, 1, C_WIDTH), F32),
        compiler_params=_params(("parallel",), 40),
        name="sample_cross_attn",
    )(qc.reshape(n, 1, C_WIDTH), mk, mv).reshape(n, C_WIDTH)


def _s_route_body(o_ref, h1_ref, wco_ref, gm_ref, wr_ref, br_ref, h2_ref, xm_ref, route_ref):
    h2 = h1_ref[...] + _dot(o_ref[...].astype(BF16), wco_ref[...])
    h2_ref[...] = h2
    xm = _rms(h2, gm_ref[...])
    xm_ref[...] = xm
    logits = _dot(xm.astype(BF16), wr_ref[...]) + br_ref[...]
    route_ref[...] = _route(logits).T[0:SUBLANES, :]


def _s_route(o, h1, w_co, g_moe, w_r, b_r):
    n = o.shape[0]
    return pl.pallas_call(
        _s_route_body,
        out_shape=[jax.ShapeDtypeStruct((n, D_MODEL), F32), jax.ShapeDtypeStruct((n, D_MODEL), F32),
                   jax.ShapeDtypeStruct((SUBLANES, n), F32)],
        compiler_params=pltpu.CompilerParams(vmem_limit_bytes=32 * 1024 * 1024),
        name="sample_cross_out_route",
    )(o, h1, w_co, g_moe, w_r, b_r)


def _pack_in_weights(w_in):
    cuts = np.cumsum((D_INNER, CONV_DIM, SSD_HEADS, ATTN_WIDTH, KV_WIDTH, KV_WIDTH, D_MODEL, D_MODEL))[:-1]
    z, xbc, dt, q, k, v, gs, ga = jnp.split(w_in, [int(c) for c in cuts], axis=1)
    dt = jnp.pad(dt, ((0, 0), (0, PACKED_COLS - SEG_DT - SSD_HEADS)))

    def dup(w):
        w = w.reshape(w.shape[0], N_KV, 1, HEAD_DIM)
        return jnp.broadcast_to(w, (w.shape[0], N_KV, 2, HEAD_DIM)).reshape(w.shape[0], KV_DUP)

    return jnp.concatenate([z, xbc, q, dup(k), dup(v), gs, ga, dt], axis=1)


def _undup(x):
    lead = x.shape[:-1]
    return x.reshape(lead + (N_KV, 2, HEAD_DIM))[..., 0, :].reshape(lead + (KV_WIDTH,))


def _head_expand_matrix():
    m = np.zeros((LANES, D_INNER), np.float32)
    for h in range(SSD_HEADS):
        m[h, h * SSD_HEAD_DIM:(h + 1) * SSD_HEAD_DIM] = 1.0
    return m


def _row(v, width=None):
    v = v.reshape(1, -1)
    if width is not None and v.shape[1] < width:
        v = jnp.pad(v, ((0, 0), (0, width - v.shape[1])))
    return v


def kernel(x_prompt, x_sample, state_ssd, state_conv, cache_win_k, cache_win_v, cache_mem_k, cache_mem_v, mem_prompt, g_mix, w_in, conv_w, conv_b, dt_bias, a_log, d_skip, g_ssd, w_ssd_down, attn_sinks, w_mix_out, g_cross, g_mem, w_cq, w_ckv, w_co, g_moe, w_route_group, b_route_group, w_route_expert, b_route_expert, w_e_gate, w_e_up, w_e_down, g_final):
    assert g_mix.shape[0] == 1, "single layer"
    b, seq, _ = x_prompt.shape
    n_s = x_sample.shape[0]
    t_p = b * seq
    assert seq % CHUNK == 0 and x_sample.shape[1] == 1

    w_packed_bf = _pack_in_weights(w_in[0].astype(BF16))
    w_down_bf, w_mix_bf = w_ssd_down[0].astype(BF16), w_mix_out[0].astype(BF16)
    w_cq_bf, w_co_bf = w_cq[0].astype(BF16), w_co[0].astype(BF16)
    g_mix_r, g_cross_r, g_mem_r, g_moe_r, g_final_r = (_row(g_mix[0]), _row(g_cross[0]), _row(g_mem[0]),
                                                        _row(g_moe[0]), _row(g_final))
    conv_b_r = _row(conv_b[0])
    dtb_r = _row(dt_bias[0], LANES)
    alog_r = _row(a_log[0], LANES)
    dskip_x = _row(jnp.repeat(d_skip[0], SSD_HEAD_DIM))
    g_ssd_r = _row(g_ssd[0])
    w_r = jnp.pad(jnp.concatenate([w_route_group[0], w_route_expert[0]], axis=1),
                  ((0, 0), (0, ROUTE_COLS - N_GROUPS - N_EXPERTS))).astype(BF16)
    b_r = _row(jnp.concatenate([b_route_group[0], b_route_expert[0]]), ROUTE_COLS)

    z, xbc, q, k, v, gs, ga, dt = _inproj(x_prompt.reshape(t_p, D_MODEL), g_mix_r, w_packed_bf)
    r3 = lambda a: a.reshape(b, seq, a.shape[-1])
    h1, p_state, p_conv, p_wk, p_wv = _mixer(
        x_prompt, r3(z), r3(xbc), r3(q), r3(k), r3(v), r3(gs), r3(ga), r3(dt),
        conv_w[0], conv_b_r, dtb_r, alog_r, dskip_x, g_ssd_r, attn_sinks[0], w_down_bf, w_mix_bf)
    mkv = _memkv(mem_prompt.reshape(b * MEM_LEN, D_MODEL), g_mem_r, w_ckv[0].astype(BF16))
    h2_p, xm_p, route_p = _cross(h1, mkv.reshape(b, MEM_LEN, 2 * C_WIDTH), g_cross_r,
                                 w_cq_bf, w_co_bf, g_moe_r, w_r, b_r)

    xs2 = x_sample.reshape(n_s, D_MODEL)
    proj = _s_inproj(xs2, g_mix_r, w_packed_bf)
    xexp = jnp.asarray(_head_expand_matrix())
    act_s, s_conv, dtx, e_s = _s_conv(proj, state_conv[0].reshape(n_s, 3 * CONV_DIM), conv_w[0], conv_b_r,
                                      dtb_r, alog_r, xexp)
    dtxt = dtx.reshape(n_s, SSD_HEADS, SSD_HEAD_DIM).transpose(0, 2, 1)
    s_state, yt = _s_state(e_s, state_ssd[0], dtxt, act_s[:, D_INNER:D_INNER + SSD_GROUPS * SSD_STATE],
                           act_s[:, D_INNER + SSD_GROUPS * SSD_STATE:])
    y_s = yt[:, :, :SSD_HEADS].transpose(0, 2, 1).reshape(n_s, D_INNER)
    q_s = proj[:, SEG_Q:SEG_Q + ATTN_WIDTH].reshape(n_s, N_KV, GQA, 1, HEAD_DIM)
    kv_eye = jnp.eye(N_KV, dtype=F32).reshape(1, N_KV, 1, N_KV, 1)
    qexp = (q_s * kv_eye).reshape(n_s, N_HEADS, KV_WIDTH)
    slope_row = _row(jnp.asarray(ALIBI_SLOPES, F32), LANES)
    att_s, s_wk, s_wv = _s_attn(qexp, cache_win_k[0].reshape(n_s, WINDOW, KV_WIDTH),
                                cache_win_v[0].reshape(n_s, WINDOW, KV_WIDTH),
                                _undup(proj[:, SEG_K:SEG_K + KV_DUP]), _undup(proj[:, SEG_V:SEG_V + KV_DUP]),
                                _row(attn_sinks[0], LANES), slope_row)
    h1_s, qc_s = _s_post(y_s, act_s, proj, att_s, xs2, dskip_x, g_ssd_r, w_down_bf, w_mix_bf, g_cross_r, w_cq_bf)
    o_s = _s_cross(qc_s, cache_mem_k[0].reshape(n_s, MEM_LEN, C_WIDTH), cache_mem_v[0].reshape(n_s, MEM_LEN, C_WIDTH))
    h2_s, xm_s, route_s = _s_route(o_s, h1_s, w_co_bf, g_moe_r, w_r, b_r)

    t_all = t_p + n_s
    t_pad = -(-t_all // RANK_TILE) * RANK_TILE
    route_all = jnp.concatenate([route_p, route_s, jnp.full((SUBLANES, t_pad - t_all), -1.0, F32)], axis=1)
    rank, counts = _rank(route_all)
    cnt = counts[:, 0].astype(jnp.int32)
    padded = (cnt + MOE_ROWS - 1) // MOE_ROWS * MOE_ROWS
    pad_end = jnp.cumsum(padded)
    offs = (pad_end - padded).astype(F32)
    nb = -(-(2 * t_all) // MOE_ROWS) + N_EXPERTS
    block_e = jnp.minimum(jnp.searchsorted(pad_end, jnp.arange(nb, dtype=jnp.int32) * MOE_ROWS, side='right'),
                          N_EXPERTS - 1).astype(jnp.int32)
    n_used = (pad_end[-1] // MOE_ROWS).astype(jnp.int32).reshape(1)
    dest = _dest(route_all, rank, jnp.broadcast_to(offs[:, None], (N_EXPERTS, LANES)))
    dest_p = dest[0:2, :t_p].T.reshape(-1)
    dest_s = dest[0:2, t_p:t_all].T.reshape(-1)
    xb = jnp.zeros((nb * MOE_ROWS, D_MODEL), F32)
    xb = _dispatch(dest_p, xm_p.reshape(t_p, D_MODEL), xb)
    xb = _dispatch(dest_s, xm_s, xb)
    yb = _experts(block_e, n_used, xb, w_e_gate[0].astype(BF16), w_e_up[0].astype(BF16), w_e_down[0].astype(BF16))
    y_p = _combine(dest_p, yb, h2_p.reshape(t_p, D_MODEL), route_p[2:4, :].T, g_final_r)
    y_smp = _combine(dest_s, yb, h2_s, route_s[2:4, :].T, g_final_r)

    return (y_p.reshape(b, seq, D_MODEL), y_smp.reshape(n_s, 1, D_MODEL),
            p_state.reshape(1, b, SSD_HEADS, SSD_HEAD_DIM, SSD_STATE), p_conv[None],
            _undup(p_wk).reshape(1, b, WINDOW, N_KV, HEAD_DIM), _undup(p_wv).reshape(1, b, WINDOW, N_KV, HEAD_DIM),
            mkv[:, :C_WIDTH].reshape(1, b, MEM_LEN, C_HEADS, C_HEAD_DIM),
            mkv[:, C_WIDTH:].reshape(1, b, MEM_LEN, C_HEADS, C_HEAD_DIM),
            s_state[None], s_conv.reshape(1, n_s, CONV_K - 1, CONV_DIM),
            s_wk.reshape(1, n_s, WINDOW, N_KV, HEAD_DIM), s_wv.reshape(1, n_s, WINDOW, N_KV, HEAD_DIM))
```

```python
import functools
import math

import jax
import jax.numpy as jnp
import numpy as np
from jax import lax
from jax.experimental import pallas as pl
from jax.experimental.pallas import tpu as pltpu

F32 = jnp.float32
BF16 = jnp.bfloat16
HIGHEST = lax.Precision.HIGHEST

D_MODEL = 1024
D_INNER = 2048
SSD_HEAD_DIM = 64
SSD_HEADS = 32
SSD_GROUPS = 4
SSD_HPG = 8
SSD_STATE = 128
CONV_K = 4
CONV_DIM = 3072
CHUNK = 128
HEAD_DIM = 64
N_HEADS = 16
N_KV = 4
GQA = 4
ATTN_WIDTH = 1024
KV_WIDTH = 256
WINDOW = 128
MEM_LEN = 256
C_HEADS = 4
C_HEAD_DIM = 128
C_WIDTH = 512
N_GROUPS = 4
EXP_PER_GROUP = 8
N_EXPERTS = 32
D_EXPERT = 512
EPS = 1e-6
NEG_INF = -1e30
LANES = 128
SUBLANES = 8

KV_DUP = 2 * KV_WIDTH
SEG_Z, SEG_XBC, SEG_Q, SEG_K, SEG_V, SEG_GS, SEG_GA, SEG_DT = (
    0, 2048, 5120, 6144, 6656, 7168, 8192, 9216)
PACKED_COLS = 9472
MASKED_DIST = 1e32
ROUTE_COLS = 128

MOE_ROWS = 256
ALIBI_SLOPES = tuple(2.0 ** (-8.0 * (h + 1) / N_HEADS) for h in range(N_HEADS))


def _tile(n, pref, mult=SUBLANES):
    if n <= pref:
        return n
    for t in range(pref, 0, -1):
        if n % t == 0 and t % mult == 0:
            return t
    return n


def _params(sem, vmem_mb):
    return pltpu.CompilerParams(dimension_semantics=sem, vmem_limit_bytes=vmem_mb * 1024 * 1024)


def _const_spec(shape):
    nd = len(shape)
    return pl.BlockSpec(shape, lambda *_: (0,) * nd, pipeline_mode=pl.Buffered(1))


def _sigmoid(x):
    return 1.0 / (1.0 + jnp.exp(-x))


def _silu(x):
    return x * _sigmoid(x)


def _softplus(x):
    return jnp.maximum(x, 0.0) + jnp.log1p(jnp.exp(-jnp.abs(x)))


def _rms(x, g):
    return x * lax.rsqrt(jnp.mean(x * x, axis=-1, keepdims=True) + EPS) * g


def _bf16_round(x):
    return x.astype(BF16).astype(F32)


def _dot(a, b, precision=None):
    return jnp.dot(a, b, preferred_element_type=F32, precision=precision)


def _dot_nt(a, b, precision=None):
    return lax.dot_general(a, b, (((1,), (1,)), ((), ())), preferred_element_type=F32, precision=precision)


def _inproj_body(x_ref, g_ref, w_ref, z_ref, xbc_ref, q_ref, k_ref, v_ref, gs_ref, ga_ref, dt_ref):
    xb = _rms(x_ref[...], g_ref[...]).astype(BF16)
    segs = ((z_ref, SEG_Z, D_INNER, 1.0), (xbc_ref, SEG_XBC, CONV_DIM, 1.0),
            (q_ref, SEG_Q, ATTN_WIDTH, HEAD_DIM ** -0.5), (k_ref, SEG_K, KV_DUP, 1.0),
            (v_ref, SEG_V, KV_DUP, 1.0), (gs_ref, SEG_GS, D_MODEL, 1.0), (ga_ref, SEG_GA, D_MODEL, 1.0),
            (dt_ref, SEG_DT, LANES, 1.0))
    step = 512
    for ref, off, width, scale in segs:
        for c0 in range(0, width, step):
            cw = min(step, width - c0)
            r = _dot(xb, w_ref[:, off + c0:off + c0 + cw])
            if scale != 1.0:
                r = r * scale
            ref[:, c0:c0 + cw] = r.astype(ref.dtype)


def _inproj(x2d, g_mix, w_packed):
    t = x2d.shape[0]
    tm = _tile(t, 512)
    widths = (D_INNER, CONV_DIM, ATTN_WIDTH, KV_DUP, KV_DUP, D_MODEL, D_MODEL)
    out_shape = [jax.ShapeDtypeStruct((t, w), BF16) for w in widths] + [jax.ShapeDtypeStruct((t, LANES), F32)]
    out_specs = [pl.BlockSpec((tm, w), lambda i: (i, 0)) for w in widths] + [pl.BlockSpec((tm, LANES), lambda i: (i, 0))]
    return pl.pallas_call(
        _inproj_body,
        grid=(t // tm,),
        in_specs=[pl.BlockSpec((tm, D_MODEL), lambda i: (i, 0)),
                  _const_spec((1, D_MODEL)),
                  _const_spec((D_MODEL, PACKED_COLS))],
        out_specs=out_specs,
        out_shape=out_shape,
        compiler_params=_params(("parallel",), 56),
        name="prompt_inproj",
    )(x2d, g_mix, w_packed)


def _mixer_body(xbc_ref, z_ref, dt_ref, q_ref, k_ref, v_ref, gs_ref, ga_ref, x_ref,
                convw_ref, convb_ref, dtb_ref, alog_ref, dskip_ref, gssd_ref, sinks_ref, wdown_ref, wmix_ref,
                h1_ref, pstate_ref, pconv_ref, pk_ref, pv_ref,
                cbuf, act, ybuf, att, kprev, vprev, state):
    c = pl.program_id(1)
    last = pl.num_programs(1) - 1
    L = CHUNK

    @pl.when(c == 0)
    def _():
        cbuf[0:SUBLANES, :] = jnp.zeros((SUBLANES, CONV_DIM), F32)
        kprev[...] = jnp.zeros_like(kprev)
        vprev[...] = jnp.zeros_like(vprev)
        state[...] = jnp.zeros_like(state)

    cbuf[SUBLANES:SUBLANES + L, :] = xbc_ref[...].astype(F32)
    cstep = 512
    for c0 in range(0, CONV_DIM, cstep):
        cs = slice(c0, c0 + cstep)
        acc = convb_ref[:, cs] + convw_ref[3:4, cs] * cbuf[8:8 + L, cs]
        acc = acc + convw_ref[2:3, cs] * cbuf[7:7 + L, cs]
        acc = acc + convw_ref[1:2, cs] * cbuf[6:6 + L, cs]
        acc = acc + convw_ref[0:1, cs] * cbuf[5:5 + L, cs]
        act[:, cs] = _silu(acc)

    @pl.when(c == last)
    def _():
        pconv_ref[...] = cbuf[SUBLANES + L - 3:SUBLANES + L, :]

    cbuf[0:SUBLANES, :] = cbuf[L:L + SUBLANES, :]

    dt = _softplus(dt_ref[...] + dtb_ref[...])
    a_neg = -jnp.exp(alog_ref[...])
    da = dt * a_neg
    ri = lax.broadcasted_iota(jnp.int32, (L, L), 0)
    ci = lax.broadcasted_iota(jnp.int32, (L, L), 1)
    causal = ri >= ci
    tri = jnp.where(causal, 1.0, 0.0).astype(F32)
    acum = _dot(tri, da, precision=HIGHEST)
    acum_t = acum.T
    dt_t = dt.T
    eacum = jnp.exp(acum)
    a_last = acum[L - 1:L, :]
    w_tail = jnp.exp(a_last - acum) * dt
    e_last = jnp.exp(a_last)
    lane = lax.broadcasted_iota(jnp.int32, (L, LANES), 1)
    lo_half = lane < SSD_HEAD_DIM
    lane1 = lax.broadcasted_iota(jnp.int32, (1, LANES), 1)
    lo_half1 = lane1 < SSD_HEAD_DIM

    for g in range(SSD_GROUPS):
        bg = act[:, D_INNER + g * SSD_STATE:D_INNER + (g + 1) * SSD_STATE]
        cg = act[:, D_INNER + (SSD_GROUPS + g) * SSD_STATE:D_INNER + (SSD_GROUPS + g + 1) * SSD_STATE]
        bgb = bg.astype(BF16)
        cgb = cg.astype(BF16)
        cb = _dot_nt(cgb, bgb)
        hg = state[g]
        yoff = _dot(cgb, hg.astype(BF16))
        bgt = bg.T.astype(BF16)
        heads = tuple(range(g * SSD_HPG, (g + 1) * SSD_HPG))
        segs = [acum[:, h:h + 1] - acum_t[h:h + 1, :] for h in heads]
        decays = [jnp.exp(jnp.where(causal, s_, NEG_INF)) for s_ in segs]
        lmats = [(cb * d_ * dt_t[h:h + 1, :]).astype(BF16) for d_, h in zip(decays, heads)]
        xw, dsc = [], []
        for j in range(SSD_HPG // 2):
            h0, h1 = heads[2 * j], heads[2 * j + 1]
            col = h0 * SSD_HEAD_DIM
            xs_pair = act[:, col:col + LANES]
            y2 = _dot(jnp.concatenate([lmats[2 * j], lmats[2 * j + 1]], axis=0), xs_pair.astype(BF16))
            ydiag = jnp.where(lo_half, y2[0:L], y2[L:2 * L])
            esc = jnp.where(lo_half, eacum[:, h0:h0 + 1], eacum[:, h1:h1 + 1])
            ybuf[:, col:col + LANES] = ydiag + yoff[:, 2 * j * SSD_HEAD_DIM:2 * j * SSD_HEAD_DIM + LANES] * esc
            wsc = jnp.where(lo_half, w_tail[:, h0:h0 + 1], w_tail[:, h1:h1 + 1])
            xw.append((xs_pair * wsc).astype(BF16))
            dsc.append(jnp.where(lo_half1, e_last[:, h0:h0 + 1], e_last[:, h1:h1 + 1]))
        upd = _dot(bgt, jnp.concatenate(xw, axis=1))
        state[g] = hg * jnp.concatenate(dsc, axis=1) + upd

    @pl.when(c == last)
    def _():
        for g in range(SSD_GROUPS):
            pstate_ref[g] = state[g].T

    gw = D_INNER // SSD_GROUPS
    for g in range(SSD_GROUPS):
        gs_ = slice(g * gw, (g + 1) * gw)
        yg = (ybuf[:, gs_] + dskip_ref[:, gs_] * act[:, gs_]) * _silu(z_ref[:, gs_].astype(F32))
        ybuf[:, gs_] = _rms(yg, gssd_ref[:, gs_])
    y_ssd = _dot(ybuf[...].astype(BF16), wdown_ref[...])

    nd_c = jnp.where(causal, (ci - ri).astype(F32), -MASKED_DIST)
    nd_p = jnp.where(ci >= ri + jnp.where(c > 0, 0, L), (ci - ri - L).astype(F32), -MASKED_DIST)
    keep = (jnp.where(lo_half1, 1.0, 0.0).astype(BF16), jnp.where(lo_half1, 0.0, 1.0).astype(BF16))
    for kv in range(N_KV):
        kd_p = kprev[:, kv * LANES:(kv + 1) * LANES]
        kd_c = k_ref[:, kv * LANES:(kv + 1) * LANES]
        vd_p = vprev[:, kv * LANES:(kv + 1) * LANES]
        vd_c = v_ref[:, kv * LANES:(kv + 1) * LANES]
        heads = tuple(range(kv * GQA, (kv + 1) * GQA))
        pcs = [slice((kv * GQA + 2 * j) * HEAD_DIM, (kv * GQA + 2 * j) * HEAD_DIM + LANES) for j in range(GQA // 2)]
        qs = jnp.concatenate([q_ref[:, pcs[h % GQA // 2]] * keep[h % 2] for h in heads], axis=0)
        s_p = _dot_nt(qs, kd_p) + jnp.concatenate([ALIBI_SLOPES[h] * nd_p for h in heads], axis=0)
        s_c = _dot_nt(qs, kd_c) + jnp.concatenate([ALIBI_SLOPES[h] * nd_c for h in heads], axis=0)
        sink = jnp.concatenate([jnp.full((L, 1), sinks_ref[h], F32) for h in heads], axis=0)
        m = jnp.maximum(jnp.max(jnp.maximum(s_p, s_c), axis=-1, keepdims=True), sink)
        p_p = jnp.exp(s_p - m)
        p_c = jnp.exp(s_c - m)
        denom = jnp.sum(p_p + p_c, axis=-1, keepdims=True) + jnp.exp(sink - m)
        o = (_dot(p_p.astype(BF16), vd_p) + _dot(p_c.astype(BF16), vd_c)) / denom
        for j in range(GQA // 2):
            att[:, pcs[j]] = jnp.where(lo_half, o[2 * j * L:(2 * j + 1) * L], o[(2 * j + 1) * L:(2 * j + 2) * L])
    kprev[...] = k_ref[...]
    vprev[...] = v_ref[...]

    @pl.when(c == last)
    def _():
        pk_ref[...] = k_ref[...].astype(F32)
        pv_ref[...] = v_ref[...].astype(F32)

    merged = _sigmoid(gs_ref[...].astype(F32)) * y_ssd + _sigmoid(ga_ref[...].astype(F32)) * att[...]
    h1_ref[...] = x_ref[...] + _dot(merged.astype(BF16), wmix_ref[...])


def _mixer(x, z, xbc, q, k, v, gs, ga, dt, conv_w, conv_b, dtb, alog, dskip_x, g_ssd, sinks, w_down, w_mix):
    b, seq, _ = x.shape
    nc = seq // CHUNK

    def blk(width):
        return pl.BlockSpec((None, CHUNK, width), lambda i, j: (i, j, 0))

    def per_b(*shape):
        nd = len(shape)
        return pl.BlockSpec((None,) + shape, lambda i, j: (i,) + (0,) * nd)

    in_specs = [blk(CONV_DIM), blk(D_INNER), blk(LANES), blk(ATTN_WIDTH), blk(KV_DUP), blk(KV_DUP),
                blk(D_MODEL), blk(D_MODEL), blk(D_MODEL),
                _const_spec((CONV_K, CONV_DIM)), _const_spec((1, CONV_DIM)), _const_spec((1, LANES)),
                _const_spec((1, LANES)), _const_spec((1, D_INNER)), _const_spec((1, D_INNER)),
                pl.BlockSpec(memory_space=pltpu.SMEM),
                _const_spec((D_INNER, D_MODEL)), _const_spec((D_MODEL, D_MODEL))]
    out_shape = [jax.ShapeDtypeStruct((b, seq, D_MODEL), F32),
                 jax.ShapeDtypeStruct((b, SSD_GROUPS, SSD_HPG * SSD_HEAD_DIM, SSD_STATE), F32),
                 jax.ShapeDtypeStruct((b, CONV_K - 1, CONV_DIM), F32),
                 jax.ShapeDtypeStruct((b, WINDOW, KV_DUP), F32),
                 jax.ShapeDtypeStruct((b, WINDOW, KV_DUP), F32)]
    out_specs = [blk(D_MODEL), per_b(SSD_GROUPS, SSD_HPG * SSD_HEAD_DIM, SSD_STATE), per_b(CONV_K - 1, CONV_DIM),
                 per_b(WINDOW, KV_DUP), per_b(WINDOW, KV_DUP)]
    scratch = [pltpu.VMEM((CHUNK + 2 * SUBLANES, CONV_DIM), F32),
               pltpu.VMEM((CHUNK, CONV_DIM), F32),
               pltpu.VMEM((CHUNK, D_INNER), F32),
               pltpu.VMEM((CHUNK, ATTN_WIDTH), F32),
               pltpu.VMEM((CHUNK, KV_DUP), BF16), pltpu.VMEM((CHUNK, KV_DUP), BF16),
               pltpu.VMEM((SSD_GROUPS, SSD_STATE, SSD_HPG * SSD_HEAD_DIM), F32)]
    return pl.pallas_call(
        _mixer_body,
        grid=(b, nc),
        in_specs=in_specs,
        out_specs=out_specs,
        out_shape=out_shape,
        scratch_shapes=scratch,
        compiler_params=_params(("parallel", "arbitrary"), 48),
        name="prompt_mixer",
    )(xbc, z, dt, q, k, v, gs, ga, x, conv_w, conv_b, dtb, alog, dskip_x, g_ssd, sinks, w_down, w_mix)


def _memkv_body(m_ref, g_ref, w_ref, o_ref):
    o_ref[...] = _dot(_rms(m_ref[...], g_ref[...]).astype(BF16), w_ref[...])


def _memkv(mem2d, g_mem, w_ckv):
    t = mem2d.shape[0]
    tm = _tile(t, 256)
    return pl.pallas_call(
        _memkv_body,
        grid=(t // tm,),
        in_specs=[pl.BlockSpec((tm, D_MODEL), lambda i: (i, 0)), _const_spec((1, D_MODEL)),
                  _const_spec((D_MODEL, 2 * C_WIDTH))],
        out_specs=pl.BlockSpec((tm, 2 * C_WIDTH), lambda i: (i, 0)),
        out_shape=jax.ShapeDtypeStruct((t, 2 * C_WIDTH), F32),
        compiler_params=_params(("parallel",), 32),
        name="memory_kv",
    )(mem2d, g_mem, w_ckv)


def _route(logits):
    rows = logits.shape[0]
    lane = lax.broadcasted_iota(jnp.int32, (rows, ROUTE_COLS), 1).astype(F32)
    big = 1e9
    is_g = lane < N_GROUPS
    lg = jnp.where(is_g, logits, NEG_INF)
    gmax = jnp.max(lg, axis=-1, keepdims=True)
    grp = jnp.min(jnp.where(lg == gmax, lane, big), axis=-1, keepdims=True)
    p_grp = 1.0 / jnp.sum(jnp.where(is_g, jnp.exp(lg - gmax), 0.0), axis=-1, keepdims=True)
    lo = N_GROUPS + EXP_PER_GROUP * grp
    in_grp = (lane >= lo) & (lane < lo + EXP_PER_GROUP)
    le = jnp.where(in_grp, logits, NEG_INF)
    m1 = jnp.max(le, axis=-1, keepdims=True)
    i1 = jnp.min(jnp.where(le == m1, lane, big), axis=-1, keepdims=True)
    le2 = jnp.where(lane == i1, NEG_INF, le)
    m2 = jnp.max(le2, axis=-1, keepdims=True)
    i2 = jnp.min(jnp.where(le2 == m2, lane, big), axis=-1, keepdims=True)
    t2 = jnp.exp(m2 - m1)
    g1 = p_grp / (1.0 + t2)
    g2 = p_grp * t2 / (1.0 + t2)
    info = jnp.where(lane == 0, i1 - N_GROUPS,
                     jnp.where(lane == 1, i2 - N_GROUPS,
                               jnp.where(lane == 2, g1, jnp.where(lane == 3, g2, 0.0))))
    return info


def _cross_body(h1_ref, mkv_ref, gc_ref, wcq_ref, wco_ref, gm_ref, wr_ref, br_ref,
                h2_ref, xm_ref, route_ref, obuf):
    h1 = h1_ref[...]
    xn = _rms(h1, gc_ref[...]).astype(BF16)
    qc = _dot(xn, wcq_ref[...])
    scale = C_HEAD_DIM ** -0.5
    for h in range(C_HEADS):
        hs = slice(h * C_HEAD_DIM, (h + 1) * C_HEAD_DIM)
        mk = mkv_ref[:, hs].astype(BF16)
        mv = mkv_ref[:, C_WIDTH + h * C_HEAD_DIM:C_WIDTH + (h + 1) * C_HEAD_DIM].astype(BF16)
        s = _dot_nt(qc[:, hs].astype(BF16), mk) * scale
        m = jnp.max(s, axis=-1, keepdims=True)
        p = jnp.exp(s - m)
        obuf[:, hs] = _dot(p.astype(BF16), mv) / jnp.sum(p, axis=-1, keepdims=True)
    h2 = h1 + _dot(obuf[...].astype(BF16), wco_ref[...])
    h2_ref[...] = h2
    xm = _rms(h2, gm_ref[...])
    xm_ref[...] = xm
    logits = _dot(xm.astype(BF16), wr_ref[...]) + br_ref[...]
    route_ref[...] = _route(logits).T[0:SUBLANES, :]


def _cross(h1, mkv, g_cross, w_cq, w_co, g_moe, w_r, b_r):
    b, seq, _ = h1.shape
    tq = _tile(seq, 512)
    nq = seq // tq
    return pl.pallas_call(
        _cross_body,
        grid=(b, nq),
        in_specs=[pl.BlockSpec((None, tq, D_MODEL), lambda i, j: (i, j, 0)),
                  pl.BlockSpec((None, MEM_LEN, 2 * C_WIDTH), lambda i, j: (i, 0, 0)),
                  _const_spec((1, D_MODEL)), _const_spec((D_MODEL, C_WIDTH)), _const_spec((C_WIDTH, D_MODEL)),
                  _const_spec((1, D_MODEL)), _const_spec((D_MODEL, ROUTE_COLS)), _const_spec((1, ROUTE_COLS))],
        out_specs=[pl.BlockSpec((None, tq, D_MODEL), lambda i, j: (i, j, 0)),
                   pl.BlockSpec((None, tq, D_MODEL), lambda i, j: (i, j, 0)),
                   pl.BlockSpec((SUBLANES, tq), lambda i, j: (0, i * nq + j))],
        out_shape=[jax.ShapeDtypeStruct((b, seq, D_MODEL), F32),
                   jax.ShapeDtypeStruct((b, seq, D_MODEL), F32),
                   jax.ShapeDtypeStruct((SUBLANES, b * seq), F32)],
        scratch_shapes=[pltpu.VMEM((tq, C_WIDTH), F32)],
        compiler_params=_params(("parallel", "parallel"), 48),
        name="prompt_cross_route",
    )(h1, mkv, g_cross, w_cq, w_co, g_moe, w_r, b_r)


RANK_TILE = 512


def _rank_body(route_ref, rank_ref, count_ref, carry):
    i = pl.program_id(0)

    @pl.when(i == 0)
    def _():
        carry[...] = jnp.zeros_like(carry)

    e1 = route_ref[0:1, :]
    e2 = route_ref[1:2, :]
    eid = lax.broadcasted_iota(jnp.int32, (N_EXPERTS, RANK_TILE), 0).astype(F32)
    is1 = e1 == eid
    is2 = e2 == eid
    onehot = jnp.where(is1 | is2, 1.0, 0.0)
    si = lax.broadcasted_iota(jnp.int32, (RANK_TILE, RANK_TILE), 0)
    ti = lax.broadcasted_iota(jnp.int32, (RANK_TILE, RANK_TILE), 1)
    before = jnp.where(si < ti, 1.0, 0.0).astype(BF16)
    prefix = _dot(onehot.astype(BF16), before) + carry[:, 0:1]
    r1 = jnp.sum(jnp.where(is1, prefix, 0.0), axis=0, keepdims=True)
    r2 = jnp.sum(jnp.where(is2, prefix, 0.0), axis=0, keepdims=True)
    row = lax.broadcasted_iota(jnp.int32, (SUBLANES, RANK_TILE), 0)
    rank_ref[...] = jnp.where(row == 0, r1, jnp.where(row == 1, r2, 0.0))
    carry[...] = carry[...] + jnp.sum(onehot, axis=1, keepdims=True)
    count_ref[...] = carry[...]


def _rank(route):
    tp = route.shape[1]
    return pl.pallas_call(
        _rank_body,
        grid=(tp // RANK_TILE,),
        in_specs=[pl.BlockSpec((SUBLANES, RANK_TILE), lambda i: (0, i))],
        out_specs=[pl.BlockSpec((SUBLANES, RANK_TILE), lambda i: (0, i)),
                   pl.BlockSpec((N_EXPERTS, LANES), lambda i: (0, 0))],
        out_shape=[jax.ShapeDtypeStruct((SUBLANES, tp), F32), jax.ShapeDtypeStruct((N_EXPERTS, LANES), F32)],
        scratch_shapes=[pltpu.VMEM((N_EXPERTS, LANES), F32)],
        compiler_params=_params(("arbitrary",), 32),
        name="moe_rank",
    )(route)


def _dest_body(route_ref, rank_ref, offs_ref, dest_ref):
    e1 = route_ref[0:1, :]
    e2 = route_ref[1:2, :]
    eid = lax.broadcasted_iota(jnp.int32, (N_EXPERTS, RANK_TILE), 0).astype(F32)
    offs = offs_ref[:, 0:1]
    d1 = jnp.sum(jnp.where(e1 == eid, offs, 0.0), axis=0, keepdims=True) + rank_ref[0:1, :]
    d2 = jnp.sum(jnp.where(e2 == eid, offs, 0.0), axis=0, keepdims=True) + rank_ref[1:2, :]
    row = lax.broadcasted_iota(jnp.int32, (SUBLANES, RANK_TILE), 0)
    dest_ref[...] = jnp.where(row == 0, d1, jnp.where(row == 1, d2, 0.0)).astype(jnp.int32)


def _dest(route, rank, offs):
    tp = route.shape[1]
    return pl.pallas_call(
        _dest_body,
        grid=(tp // RANK_TILE,),
        in_specs=[pl.BlockSpec((SUBLANES, RANK_TILE), lambda i: (0, i)),
                  pl.BlockSpec((SUBLANES, RANK_TILE), lambda i: (0, i)),
                  pl.BlockSpec((N_EXPERTS, LANES), lambda i: (0, 0))],
        out_specs=pl.BlockSpec((SUBLANES, RANK_TILE), lambda i: (0, i)),
        out_shape=jax.ShapeDtypeStruct((SUBLANES, tp), jnp.int32),
        compiler_params=_params(("parallel",), 32),
        name="moe_dest",
    )(route, rank, offs)


DMA_UNROLL = 8


def _row_copy(src, dst, s_row, d_row, sem):
    return pltpu.make_async_copy(src.at[pl.ds(s_row, 1)], dst.at[pl.ds(d_row, 1)], sem)


def _dispatch_body(td, dest_ref, xm_ref, xb_in_ref, xb_ref, sem):
    del xb_in_ref
    base = pl.program_id(0) * td

    def issue(t, carry):
        row = base + t
        _row_copy(xm_ref, xb_ref, t, dest_ref[2 * row], sem).start()
        _row_copy(xm_ref, xb_ref, t, dest_ref[2 * row + 1], sem).start()
        return carry

    lax.fori_loop(0, td, issue, 0, unroll=DMA_UNROLL)

    def drain(t, carry):
        _row_copy(xm_ref, xb_ref, 0, 0, sem).wait()
        _row_copy(xm_ref, xb_ref, 0, 0, sem).wait()
        return carry

    lax.fori_loop(0, td, drain, 0, unroll=DMA_UNROLL)


def _dispatch(dest_flat, xm2d, xb):
    t = xm2d.shape[0]
    td = _tile(t, 256)
    return pl.pallas_call(
        functools.partial(_dispatch_body, td),
        grid_spec=pltpu.PrefetchScalarGridSpec(
            num_scalar_prefetch=1,
            grid=(t // td,),
            in_specs=[pl.BlockSpec((td, D_MODEL), lambda i, d: (i, 0)), pl.BlockSpec(memory_space=pl.ANY)],
            out_specs=pl.BlockSpec(memory_space=pl.ANY),
            scratch_shapes=[pltpu.SemaphoreType.DMA],
        ),
        out_shape=jax.ShapeDtypeStruct(xb.shape, xb.dtype),
        input_output_aliases={2: 0},
        compiler_params=_params(("arbitrary",), 32),
        name="moe_dispatch",
    )(dest_flat, xm2d, xb)


def _expert_body(be_ref, nused_ref, xb_ref, wg_ref, wu_ref, wd_ref, yb_ref, wg_bf, wu_bf, wd_bf):
    i = pl.program_id(0)

    @pl.when(jnp.logical_or(i == 0, be_ref[i] != be_ref[jnp.maximum(i - 1, 0)]))
    def _():
        wg_bf[...] = wg_ref[...].astype(BF16)
        wu_bf[...] = wu_ref[...].astype(BF16)
        wd_bf[...] = wd_ref[...].astype(BF16)

    @pl.when(i < nused_ref[0])
    def _():
        x = xb_ref[...].astype(BF16)
        hmid = _silu(_dot(x, wg_bf[...])) * _dot(x, wu_bf[...])
        yb_ref[...] = _dot(hmid.astype(BF16), wd_bf[...])

    @pl.when(i >= nused_ref[0])
    def _():
        yb_ref[...] = jnp.zeros_like(yb_ref)


def _experts(block_e, n_used, xb, wg, wu, wd):
    rows = xb.shape[0]
    nb = rows // MOE_ROWS

    def xmap(i, be, nu):
        return (jnp.minimum(i, nu[0] - 1), 0)

    def wmap(i, be, nu):
        return (be[i], 0, 0)

    return pl.pallas_call(
        _expert_body,
        grid_spec=pltpu.PrefetchScalarGridSpec(
            num_scalar_prefetch=2,
            grid=(nb,),
            in_specs=[pl.BlockSpec((MOE_ROWS, D_MODEL), xmap),
                      pl.BlockSpec((None, D_MODEL, D_EXPERT), wmap),
                      pl.BlockSpec((None, D_MODEL, D_EXPERT), wmap),
                      pl.BlockSpec((None, D_EXPERT, D_MODEL), wmap)],
            out_specs=pl.BlockSpec((MOE_ROWS, D_MODEL), lambda i, be, nu: (i, 0)),
            scratch_shapes=[pltpu.VMEM((D_MODEL, D_EXPERT), BF16), pltpu.VMEM((D_MODEL, D_EXPERT), BF16),
                            pltpu.VMEM((D_EXPERT, D_MODEL), BF16)],
        ),
        out_shape=jax.ShapeDtypeStruct((rows, D_MODEL), F32),
        compiler_params=_params(("arbitrary",), 48),
        name="moe_experts",
    )(block_e, n_used, xb, wg, wu, wd)


def _combine_body(tc, dest_ref, yb_ref, h2_ref, gate_ref, gf_ref, y_ref, buf, sem):
    base = pl.program_id(0) * tc

    def issue(t, carry):
        row = base + t
        pltpu.make_async_copy(yb_ref.at[pl.ds(dest_ref[2 * row], 1)], buf.at[0, pl.ds(t, 1)], sem).start()
        pltpu.make_async_copy(yb_ref.at[pl.ds(dest_ref[2 * row + 1], 1)], buf.at[1, pl.ds(t, 1)], sem).start()
        return carry

    lax.fori_loop(0, tc, issue, 0, unroll=DMA_UNROLL)

    def drain(t, carry):
        pltpu.make_async_copy(yb_ref.at[pl.ds(0, 1)], buf.at[0, pl.ds(0, 1)], sem).wait()
        pltpu.make_async_copy(yb_ref.at[pl.ds(0, 1)], buf.at[1, pl.ds(0, 1)], sem).wait()
        return carry

    lax.fori_loop(0, tc, drain, 0, unroll=DMA_UNROLL)
    g = gate_ref[...]
    out = h2_ref[...] + (g[:, 0:1] * buf[0] + g[:, 1:2] * buf[1])
    y_ref[...] = _rms(out, gf_ref[...])


def _combine(dest_flat, yb, h2_2d, gates, g_final):
    t = h2_2d.shape[0]
    tc = _tile(t, 256)
    return pl.pallas_call(
        functools.partial(_combine_body, tc),
        grid_spec=pltpu.PrefetchScalarGridSpec(
            num_scalar_prefetch=1,
            grid=(t // tc,),
            in_specs=[pl.BlockSpec(memory_space=pl.ANY),
                      pl.BlockSpec((tc, D_MODEL), lambda i, d: (i, 0)),
                      pl.BlockSpec((tc, 2), lambda i, d: (i, 0)),
                      pl.BlockSpec((1, D_MODEL), lambda i, d: (0, 0))],
            out_specs=pl.BlockSpec((tc, D_MODEL), lambda i, d: (i, 0)),
            scratch_shapes=[pltpu.VMEM((2, tc, D_MODEL), F32), pltpu.SemaphoreType.DMA],
        ),
        out_shape=jax.ShapeDtypeStruct((t, D_MODEL), F32),
        compiler_params=_params(("arbitrary",), 32),
        name="moe_combine",
    )(dest_flat, yb, h2_2d, gates, g_final)


def _s_inproj_body(x_ref, g_ref, w_ref, o_ref):
    o_ref[...] = _dot(_rms(x_ref[...], g_ref[...]).astype(BF16), w_ref[...])


def _s_inproj(x, g_mix, w_packed_f32):
    n = x.shape[0]
    tn = PACKED_COLS // 2
    return pl.pallas_call(
        _s_inproj_body,
        grid=(PACKED_COLS // tn,),
        in_specs=[pl.BlockSpec((n, D_MODEL), lambda j: (0, 0)), pl.BlockSpec((1, D_MODEL), lambda j: (0, 0)),
                  pl.BlockSpec((D_MODEL, tn), lambda j: (0, j))],
        out_specs=pl.BlockSpec((n, tn), lambda j: (0, j)),
        out_shape=jax.ShapeDtypeStruct((n, PACKED_COLS), F32),
        compiler_params=_params(("parallel",), 48),
        name="sample_inproj",
    )(x, g_mix, w_packed_f32)


def _s_conv_body(proj_ref, cs_ref, convw_ref, convb_ref, dtb_ref, alog_ref, xexp_ref,
                 act_ref, ncs_ref, dtx_ref, e_ref):
    step = 512
    for c0 in range(0, CONV_DIM, step):
        cs = slice(c0, c0 + step)
        s0 = cs_ref[:, c0:c0 + step]
        s1 = cs_ref[:, CONV_DIM + c0:CONV_DIM + c0 + step]
        s2 = cs_ref[:, 2 * CONV_DIM + c0:2 * CONV_DIM + c0 + step]
        xn = proj_ref[:, SEG_XBC + c0:SEG_XBC + c0 + step]
        acc = convb_ref[:, cs] + convw_ref[0:1, cs] * s0
        acc = acc + convw_ref[1:2, cs] * s1
        acc = acc + convw_ref[2:3, cs] * s2
        acc = acc + convw_ref[3:4, cs] * xn
        act_ref[:, cs] = _silu(acc)
        ncs_ref[:, c0:c0 + step] = s1
        ncs_ref[:, CONV_DIM + c0:CONV_DIM + c0 + step] = s2
        ncs_ref[:, 2 * CONV_DIM + c0:2 * CONV_DIM + c0 + step] = xn
    dt = _softplus(proj_ref[:, SEG_DT:SEG_DT + LANES] + dtb_ref[...])
    e_ref[...] = jnp.exp(dt * (-jnp.exp(alog_ref[...])))
    dtx_ref[...] = _dot(dt, xexp_ref[...], precision=HIGHEST) * act_ref[:, 0:D_INNER]


def _s_conv(proj, conv_state2d, conv_w, conv_b, dtb, alog, xexp):
    n = proj.shape[0]
    return pl.pallas_call(
        _s_conv_body,
        out_shape=[jax.ShapeDtypeStruct((n, CONV_DIM), F32), jax.ShapeDtypeStruct((n, 3 * CONV_DIM), F32),
                   jax.ShapeDtypeStruct((n, D_INNER), F32), jax.ShapeDtypeStruct((n, LANES), F32)],
        compiler_params=pltpu.CompilerParams(vmem_limit_bytes=48 * 1024 * 1024),
        name="sample_conv",
    )(proj, conv_state2d, conv_w, conv_b, dtb, alog, xexp)


def _s_state_body(bb, e_ref, st_ref, dtxt_ref, b_ref, c_ref, so_ref, yt_ref):
    base = pl.program_id(0) * bb
    lane = lax.broadcasted_iota(jnp.int32, (SSD_HEAD_DIM, LANES), 1)

    for bl in range(bb):
        for h in range(SSD_HEADS):
            g = h // SSD_HPG
            brow = b_ref[bl, :, g * SSD_STATE:(g + 1) * SSD_STATE]
            xcol = dtxt_ref[bl, :, h:h + 1]
            so_ref[bl, h] = st_ref[bl, h] * e_ref[base + bl, h] + xcol * brow
        yacc = jnp.zeros((SSD_HEAD_DIM, LANES), F32)
        for h in range(SSD_HEADS):
            g = h // SSD_HPG
            crow = c_ref[bl, :, g * SSD_STATE:(g + 1) * SSD_STATE]
            yacc = jnp.where(lane == h, jnp.sum(so_ref[bl, h] * crow, axis=-1, keepdims=True), yacc)
        yt_ref[bl] = yacc


def _s_state(e, state, dtxt, bmat, cmat):
    n = state.shape[0]
    bb = _tile(n, 2, 1)
    sblk = (bb, SSD_HEADS, SSD_HEAD_DIM, SSD_STATE)
    bmat = bmat.reshape(n, 1, SSD_GROUPS * SSD_STATE)
    cmat = cmat.reshape(n, 1, SSD_GROUPS * SSD_STATE)
    return pl.pallas_call(
        functools.partial(_s_state_body, bb),
        grid=(n // bb,),
        in_specs=[pl.BlockSpec(memory_space=pltpu.SMEM),
                  pl.BlockSpec(sblk, lambda i: (i, 0, 0, 0)),
                  pl.BlockSpec((bb, SSD_HEAD_DIM, SSD_HEADS), lambda i: (i, 0, 0)),
                  pl.BlockSpec((bb, 1, SSD_GROUPS * SSD_STATE), lambda i: (i, 0, 0)),
                  pl.BlockSpec((bb, 1, SSD_GROUPS * SSD_STATE), lambda i: (i, 0, 0))],
        out_specs=[pl.BlockSpec(sblk, lambda i: (i, 0, 0, 0)),
                   pl.BlockSpec((bb, SSD_HEAD_DIM, LANES), lambda i: (i, 0, 0))],
        out_shape=[jax.ShapeDtypeStruct(state.shape, F32), jax.ShapeDtypeStruct((n, SSD_HEAD_DIM, LANES), F32)],
        compiler_params=_params(("parallel",), 40),
        name="sample_ssd_state",
    )(e, state, dtxt, bmat, cmat)


def _s_attn_body(bb, qexp_ref, ck_ref, cv_ref, kn_ref, vn_ref, sink_ref, slope_ref,
                 y_ref, ok_ref, ov_ref):
    W = WINDOW
    lane = lax.broadcasted_iota(jnp.int32, (W, LANES), 1)
    lane1 = lax.broadcasted_iota(jnp.int32, (1, LANES), 1)
    jrow = lax.broadcasted_iota(jnp.int32, (W, LANES), 0)
    bias = slope_ref[...] * (W - jrow).astype(F32)
    sink = sink_ref[...]
    lo_half = lane1 < HEAD_DIM

    for bl in range(bb):
        kn = kn_ref[bl]
        vn = vn_ref[bl]
        kmat = _bf16_round(ck_ref[bl])
        vmat = _bf16_round(cv_ref[bl])
        knr = _bf16_round(kn)
        vnr = _bf16_round(vn)
        s = jnp.zeros((W, LANES), F32)
        sn = jnp.zeros((1, LANES), F32)
        for h in range(N_HEADS):
            qrow = _bf16_round(qexp_ref[bl, h:h + 1, :])
            s = jnp.where(lane == h, jnp.sum(kmat * qrow, axis=-1, keepdims=True), s)
            sn = jnp.where(lane1 == h, jnp.sum(knr * qrow, axis=-1, keepdims=True), sn)
        s = s * (HEAD_DIM ** -0.5) - bias
        sn = sn * (HEAD_DIM ** -0.5)
        m = jnp.maximum(jnp.maximum(jnp.max(s, axis=0, keepdims=True), sn), sink)
        p = jnp.exp(s - m)
        pn = jnp.exp(sn - m)
        denom = jnp.sum(p, axis=0, keepdims=True) + pn + jnp.exp(sink - m)
        p = _bf16_round(p / denom)
        pn = _bf16_round(pn / denom)
        for j in range(N_HEADS // 2):
            outs = []
            for h in (2 * j, 2 * j + 1):
                kv = h // GQA
                cs = slice((kv // 2) * LANES, (kv // 2 + 1) * LANES)
                o = jnp.sum(p[:, h:h + 1] * vmat[:, cs], axis=0, keepdims=True) + pn[:, h:h + 1] * vnr[:, cs]
                if (kv % 2) != (h % 2):
                    o = pltpu.roll(o, HEAD_DIM, axis=1)
                outs.append(o)
            y_ref[bl, :, j * LANES:(j + 1) * LANES] = jnp.where(lo_half, outs[0], outs[1])
        ok_ref[bl, 0:W - 1, :] = ck_ref[bl, 1:W, :]
        ok_ref[bl, W - 1:W, :] = kn
        ov_ref[bl, 0:W - 1, :] = cv_ref[bl, 1:W, :]
        ov_ref[bl, W - 1:W, :] = vn


def _s_attn(qexp, ck, cv, kn, vn, sink_row, slope_row):
    n = ck.shape[0]
    bb = _tile(n, 2, 1)
    cblk = pl.BlockSpec((bb, WINDOW, KV_WIDTH), lambda i: (i, 0, 0))
    rblk = pl.BlockSpec((bb, 1, KV_WIDTH), lambda i: (i, 0, 0))
    y, ok, ov = pl.pallas_call(
        functools.partial(_s_attn_body, bb),
        grid=(n // bb,),
        in_specs=[pl.BlockSpec((bb, N_HEADS, KV_WIDTH), lambda i: (i, 0, 0)), cblk, cblk, rblk, rblk,
                  pl.BlockSpec((1, LANES), lambda i: (0, 0)), pl.BlockSpec((1, LANES), lambda i: (0, 0))],
        out_specs=[pl.BlockSpec((bb, 1, ATTN_WIDTH), lambda i: (i, 0, 0)), cblk, cblk],
        out_shape=[jax.ShapeDtypeStruct((n, 1, ATTN_WIDTH), F32), jax.ShapeDtypeStruct(ck.shape, F32),
                   jax.ShapeDtypeStruct(cv.shape, F32)],
        compiler_params=_params(("parallel",), 32),
        name="sample_window_attn",
    )(qexp, ck, cv, kn.reshape(n, 1, KV_WIDTH), vn.reshape(n, 1, KV_WIDTH), sink_row, slope_row)
    return y.reshape(n, ATTN_WIDTH), ok, ov


def _s_post_body(y_ref, act_ref, proj_ref, att_ref, x_ref, dskip_ref, gssd_ref, wdown_ref, wmix_ref,
                 gc_ref, wcq_ref, h1_ref, qc_ref, ybuf):
    gw = D_INNER // SSD_GROUPS
    for g in range(SSD_GROUPS):
        gs_ = slice(g * gw, (g + 1) * gw)
        yg = (y_ref[:, gs_] + dskip_ref[:, gs_] * act_ref[:, gs_]) * _silu(proj_ref[:, SEG_Z + g * gw:SEG_Z + (g + 1) * gw])
        ybuf[:, gs_] = _rms(yg, gssd_ref[:, gs_])
    y_ssd = _dot(ybuf[...].astype(BF16), wdown_ref[...])
    merged = (_sigmoid(proj_ref[:, SEG_GS:SEG_GS + D_MODEL]) * y_ssd
              + _sigmoid(proj_ref[:, SEG_GA:SEG_GA + D_MODEL]) * att_ref[...])
    h1 = x_ref[...] + _dot(merged.astype(BF16), wmix_ref[...])
    h1_ref[...] = h1
    qc_ref[...] = _bf16_round(_dot(_rms(h1, gc_ref[...]).astype(BF16), wcq_ref[...]))


def _s_post(y, act, proj, att, x, dskip_x, g_ssd, w_down, w_mix, g_cross, w_cq):
    n = x.shape[0]
    return pl.pallas_call(
        _s_post_body,
        out_shape=[jax.ShapeDtypeStruct((n, D_MODEL), F32), jax.ShapeDtypeStruct((n, C_WIDTH), F32)],
        scratch_shapes=[pltpu.VMEM((n, D_INNER), F32)],
        compiler_params=pltpu.CompilerParams(vmem_limit_bytes=48 * 1024 * 1024),
        name="sample_post_mixer",
    )(y, act, proj, att, x, dskip_x, g_ssd, w_down, w_mix, g_cross, w_cq)


def _s_cross_body(bb, qc_ref, mk_ref, mv_ref, o_ref):
    lane = lax.broadcasted_iota(jnp.int32, (MEM_LEN, LANES), 1)
    scale = C_HEAD_DIM ** -0.5

    for bl in range(bb):
        q = qc_ref[bl]
        s = jnp.zeros((MEM_LEN, LANES), F32)
        for h in range(C_HEADS):
            hs = slice(h * C_HEAD_DIM, (h + 1) * C_HEAD_DIM)
            kmat = _bf16_round(mk_ref[bl, :, h, :])
            s = jnp.where(lane == h, jnp.sum(kmat * q[:, hs], axis=-1, keepdims=True), s)
        s = s * scale
        m = jnp.max(s, axis=0, keepdims=True)
        p = jnp.exp(s - m)
        p = _bf16_round(p / jnp.sum(p, axis=0, keepdims=True))
        for h in range(C_HEADS):
            hs = slice(h * C_HEAD_DIM, (h + 1) * C_HEAD_DIM)
            vmat = _bf16_round(mv_ref[bl, :, h, :])
            o_ref[bl, :, hs] = jnp.sum(p[:, h:h + 1] * vmat, axis=0, keepdims=True)


def _s_cross(qc, mk, mv):
    n = qc.shape[0]
    bb = _tile(n, 4, 1)
    mblk = pl.BlockSpec((bb, MEM_LEN, C_HEADS, C_HEAD_DIM), lambda i: (i, 0, 0, 0))
    rblk = pl.BlockSpec((bb, 1, C_WIDTH), lambda i: (i, 0, 0))
    return pl.pallas_call(
        functools.partial(_s_cross_body, bb),
        grid=(n // bb,),
        in_specs=[rblk, mblk, mblk],
        out_specs=rblk,
        out_shape=jax.ShapeDtypeStruct((n, 1, C_WIDTH), F32),
        compiler_params=_params(("parallel",), 40),
        name="sample_cross_attn",
    )(qc.reshape(n, 1, C_WIDTH), mk, mv).reshape(n, C_WIDTH)


def _s_route_body(o_ref, h1_ref, wco_ref, gm_ref, wr_ref, br_ref, h2_ref, xm_ref, route_ref):
    h2 = h1_ref[...] + _dot(o_ref[...].astype(BF16), wco_ref[...])
    h2_ref[...] = h2
    xm = _rms(h2, gm_ref[...])
    xm_ref[...] = xm
    logits = _dot(xm.astype(BF16), wr_ref[...]) + br_ref[...]
    route_ref[...] = _route(logits).T[0:SUBLANES, :]


def _s_route(o, h1, w_co, g_moe, w_r, b_r):
    n = o.shape[0]
    return pl.pallas_call(
        _s_route_body,
        out_shape=[jax.ShapeDtypeStruct((n, D_MODEL), F32), jax.ShapeDtypeStruct((n, D_MODEL), F32),
                   jax.ShapeDtypeStruct((SUBLANES, n), F32)],
        compiler_params=pltpu.CompilerParams(vmem_limit_bytes=32 * 1024 * 1024),
        name="sample_cross_out_route",
    )(o, h1, w_co, g_moe, w_r, b_r)


def _pack_in_weights(w_in):
    cuts = np.cumsum((D_INNER, CONV_DIM, SSD_HEADS, ATTN_WIDTH, KV_WIDTH, KV_WIDTH, D_MODEL, D_MODEL))[:-1]
    z, xbc, dt, q, k, v, gs, ga = jnp.split(w_in, [int(c) for c in cuts], axis=1)
    dt = jnp.pad(dt, ((0, 0), (0, PACKED_COLS - SEG_DT - SSD_HEADS)))

    def dup(w):
        w = w.reshape(w.shape[0], N_KV, 1, HEAD_DIM)
        return jnp.broadcast_to(w, (w.shape[0], N_KV, 2, HEAD_DIM)).reshape(w.shape[0], KV_DUP)

    return jnp.concatenate([z, xbc, q, dup(k), dup(v), gs, ga, dt], axis=1)


def _undup(x):
    lead = x.shape[:-1]
    return x.reshape(lead + (N_KV, 2, HEAD_DIM))[..., 0, :].reshape(lead + (KV_WIDTH,))


def _head_expand_matrix():
    m = np.zeros((LANES, D_INNER), np.float32)
    for h in range(SSD_HEADS):
        m[h, h * SSD_HEAD_DIM:(h + 1) * SSD_HEAD_DIM] = 1.0
    return m


def _row(v, width=None):
    v = v.reshape(1, -1)
    if width is not None and v.shape[1] < width:
        v = jnp.pad(v, ((0, 0), (0, width - v.shape[1])))
    return v


def kernel(x_prompt, x_sample, state_ssd, state_conv, cache_win_k, cache_win_v, cache_mem_k, cache_mem_v, mem_prompt, g_mix, w_in, conv_w, conv_b, dt_bias, a_log, d_skip, g_ssd, w_ssd_down, attn_sinks, w_mix_out, g_cross, g_mem, w_cq, w_ckv, w_co, g_moe, w_route_group, b_route_group, w_route_expert, b_route_expert, w_e_gate, w_e_up, w_e_down, g_final):
    assert g_mix.shape[0] == 1, "single layer"
    b, seq, _ = x_prompt.shape
    n_s = x_sample.shape[0]
    t_p = b * seq
    assert seq % CHUNK == 0 and x_sample.shape[1] == 1

    w_packed_bf = _pack_in_weights(w_in[0].astype(BF16))
    w_down_bf, w_mix_bf = w_ssd_down[0].astype(BF16), w_mix_out[0].astype(BF16)
    w_cq_bf, w_co_bf = w_cq[0].astype(BF16), w_co[0].astype(BF16)
    g_mix_r, g_cross_r, g_mem_r, g_moe_r, g_final_r = (_row(g_mix[0]), _row(g_cross[0]), _row(g_mem[0]),
                                                        _row(g_moe[0]), _row(g_final))
    conv_b_r = _row(conv_b[0])
    dtb_r = _row(dt_bias[0], LANES)
    alog_r = _row(a_log[0], LANES)
    dskip_x = _row(jnp.repeat(d_skip[0], SSD_HEAD_DIM))
    g_ssd_r = _row(g_ssd[0])
    w_r = jnp.pad(jnp.concatenate([w_route_group[0], w_route_expert[0]], axis=1),
                  ((0, 0), (0, ROUTE_COLS - N_GROUPS - N_EXPERTS))).astype(BF16)
    b_r = _row(jnp.concatenate([b_route_group[0], b_route_expert[0]]), ROUTE_COLS)

    z, xbc, q, k, v, gs, ga, dt = _inproj(x_prompt.reshape(t_p, D_MODEL), g_mix_r, w_packed_bf)
    r3 = lambda a: a.reshape(b, seq, a.shape[-1])
    h1, p_state, p_conv, p_wk, p_wv = _mixer(
        x_prompt, r3(z), r3(xbc), r3(q), r3(k), r3(v), r3(gs), r3(ga), r3(dt),
        conv_w[0], conv_b_r, dtb_r, alog_r, dskip_x, g_ssd_r, attn_sinks[0], w_down_bf, w_mix_bf)
    mkv = _memkv(mem_prompt.reshape(b * MEM_LEN, D_MODEL), g_mem_r, w_ckv[0].astype(BF16))
    h2_p, xm_p, route_p = _cross(h1, mkv.reshape(b, MEM_LEN, 2 * C_WIDTH), g_cross_r,
                                 w_cq_bf, w_co_bf, g_moe_r, w_r, b_r)

    xs2 = x_sample.reshape(n_s, D_MODEL)
    proj = _s_inproj(xs2, g_mix_r, w_packed_bf)
    xexp = jnp.asarray(_head_expand_matrix())
    act_s, s_conv, dtx, e_s = _s_conv(proj, state_conv[0].reshape(n_s, 3 * CONV_DIM), conv_w[0], conv_b_r,
                                      dtb_r, alog_r, xexp)
    dtxt = dtx.reshape(n_s, SSD_HEADS, SSD_HEAD_DIM).transpose(0, 2, 1)
    s_state, yt = _s_state(e_s, state_ssd[0], dtxt, act_s[:, D_INNER:D_INNER + SSD_GROUPS * SSD_STATE],
                           act_s[:, D_INNER + SSD_GROUPS * SSD_STATE:])
    y_s = yt[:, :, :SSD_HEADS].transpose(0, 2, 1).reshape(n_s, D_INNER)
    slope_row = _row(jnp.asarray(ALIBI_SLOPES, F32), LANES)
    q_s = proj[:, SEG_Q:SEG_Q + ATTN_WIDTH].reshape(n_s, N_KV, GQA, 1, HEAD_DIM)
    kv_eye = jnp.eye(N_KV, dtype=F32).reshape(1, N_KV, 1, N_KV, 1)
    qexp = (q_s * kv_eye).reshape(n_s, N_HEADS, KV_WIDTH)
    att_s, s_wk, s_wv = _s_attn(qexp, cache_win_k[0].reshape(n_s, WINDOW, KV_WIDTH),
                                cache_win_v[0].reshape(n_s, WINDOW, KV_WIDTH),
                                _undup(proj[:, SEG_K:SEG_K + KV_DUP]), _undup(proj[:, SEG_V:SEG_V + KV_DUP]),
                                _row(attn_sinks[0], LANES), slope_row)
    h1_s, qc_s = _s_post(y_s, act_s, proj, att_s, xs2, dskip_x, g_ssd_r, w_down_bf, w_mix_bf, g_cross_r, w_cq_bf)
    o_s = _s_cross(qc_s, cache_mem_k[0], cache_mem_v[0])
    h2_s, xm_s, route_s = _s_route(o_s, h1_s, w_co_bf, g_moe_r, w_r, b_r)

    t_all = t_p + n_s
    t_pad = -(-t_all // RANK_TILE) * RANK_TILE
    route_all = jnp.concatenate([route_p, route_s, jnp.full((SUBLANES, t_pad - t_all), -1.0, F32)], axis=1)
    rank, counts = _rank(route_all)
    cnt = counts[:, 0].astype(jnp.int32)
    padded = (cnt + MOE_ROWS - 1) // MOE_ROWS * MOE_ROWS
    pad_end = jnp.cumsum(padded)
    offs = (pad_end - padded).astype(F32)
    nb = -(-(2 * t_all) // MOE_ROWS) + N_EXPERTS
    block_start = jnp.arange(nb, dtype=jnp.int32) * MOE_ROWS
    block_e = jnp.minimum(jnp.sum((pad_end[None, :] <= block_start[:, None]).astype(jnp.int32), axis=1),
                          N_EXPERTS - 1)
    n_used = (pad_end[-1] // MOE_ROWS).astype(jnp.int32).reshape(1)
    dest = _dest(route_all, rank, jnp.broadcast_to(offs[:, None], (N_EXPERTS, LANES)))
    dest_p = dest[0:2, :t_p].T.reshape(-1)
    dest_s = dest[0:2, t_p:t_all].T.reshape(-1)
    xb = jnp.zeros((nb * MOE_ROWS, D_MODEL), F32)
    xb = _dispatch(dest_p, xm_p.reshape(t_p, D_MODEL), xb)
    xb = _dispatch(dest_s, xm_s, xb)
    yb = _experts(block_e, n_used, xb, w_e_gate[0], w_e_up[0], w_e_down[0])
    y_p = _combine(dest_p, yb, h2_p.reshape(t_p, D_MODEL), route_p[2:4, :].T, g_final_r)
    y_smp = _combine(dest_s, yb, h2_s, route_s[2:4, :].T, g_final_r)

    return (y_p.reshape(b, seq, D_MODEL), y_smp.reshape(n_s, 1, D_MODEL),
            p_state.reshape(1, b, SSD_HEADS, SSD_HEAD_DIM, SSD_STATE), p_conv[None],
            _undup(p_wk).reshape(1, b, WINDOW, N_KV, HEAD_DIM), _undup(p_wv).reshape(1, b, WINDOW, N_KV, HEAD_DIM),
            mkv[:, :C_WIDTH].reshape(1, b, MEM_LEN, C_HEADS, C_HEAD_DIM),
            mkv[:, C_WIDTH:].reshape(1, b, MEM_LEN, C_HEADS, C_HEAD_DIM),
            s_state[None], s_conv.reshape(1, n_s, CONV_K - 1, CONV_DIM),
            s_wk.reshape(1, n_s, WINDOW, N_KV, HEAD_DIM), s_wv.reshape(1, n_s, WINDOW, N_KV, HEAD_DIM))
```

```python
import functools
import math

import jax
import jax.numpy as jnp
import numpy as np
from jax import lax
from jax.experimental import pallas as pl
from jax.experimental.pallas import tpu as pltpu

F32 = jnp.float32
BF16 = jnp.bfloat16
HIGHEST = lax.Precision.HIGHEST

D_MODEL = 1024
D_INNER = 2048
SSD_HEAD_DIM = 64
SSD_HEADS = 32
SSD_GROUPS = 4
SSD_HPG = 8
SSD_STATE = 128
CONV_K = 4
CONV_DIM = 3072
CHUNK = 128
HEAD_DIM = 64
N_HEADS = 16
N_KV = 4
GQA = 4
ATTN_WIDTH = 1024
KV_WIDTH = 256
WINDOW = 128
MEM_LEN = 256
C_HEADS = 4
C_HEAD_DIM = 128
C_WIDTH = 512
N_GROUPS = 4
EXP_PER_GROUP = 8
N_EXPERTS = 32
D_EXPERT = 512
EPS = 1e-6
NEG_INF = -1e30
LANES = 128
SUBLANES = 8

KV_DUP = 2 * KV_WIDTH
SEG_Z, SEG_XBC, SEG_Q, SEG_K, SEG_V, SEG_GS, SEG_GA, SEG_DT = (
    0, 2048, 5120, 6144, 6656, 7168, 8192, 9216)
PACKED_COLS = 9472
MASKED_DIST = 1e32
ROUTE_COLS = 128

MOE_ROWS = 256
ALIBI_SLOPES = tuple(2.0 ** (-8.0 * (h + 1) / N_HEADS) for h in range(N_HEADS))


def _tile(n, pref, mult=SUBLANES):
    if n <= pref:
        return n
    for t in range(pref, 0, -1):
        if n % t == 0 and t % mult == 0:
            return t
    return n


def _params(sem, vmem_mb):
    return pltpu.CompilerParams(dimension_semantics=sem, vmem_limit_bytes=vmem_mb * 1024 * 1024)


def _const_spec(shape):
    nd = len(shape)
    return pl.BlockSpec(shape, lambda *_: (0,) * nd, pipeline_mode=pl.Buffered(1))


def _sigmoid(x):
    return 1.0 / (1.0 + jnp.exp(-x))


def _silu(x):
    return x * _sigmoid(x)


def _sigmoid_t(x):
    return 0.5 * jnp.tanh(0.5 * x) + 0.5


def _silu_t(x):
    return x * _sigmoid_t(x)


def _softplus(x):
    return jnp.maximum(x, 0.0) + jnp.log1p(jnp.exp(-jnp.abs(x)))


def _rms(x, g):
    return x * lax.rsqrt(jnp.mean(x * x, axis=-1, keepdims=True) + EPS) * g


def _bf16_round(x):
    return x.astype(BF16).astype(F32)


def _dot(a, b, precision=None):
    return jnp.dot(a, b, preferred_element_type=F32, precision=precision)


def _dot_nt(a, b, precision=None):
    return lax.dot_general(a, b, (((1,), (1,)), ((), ())), preferred_element_type=F32, precision=precision)


def _inproj_body(x_ref, g_ref, w_ref, z_ref, xbc_ref, q_ref, k_ref, v_ref, gs_ref, ga_ref, dt_ref):
    xb = _rms(x_ref[...], g_ref[...]).astype(BF16)
    segs = ((z_ref, SEG_Z, D_INNER, 1.0), (xbc_ref, SEG_XBC, CONV_DIM, 1.0),
            (q_ref, SEG_Q, ATTN_WIDTH, HEAD_DIM ** -0.5), (k_ref, SEG_K, KV_DUP, 1.0),
            (v_ref, SEG_V, KV_DUP, 1.0), (gs_ref, SEG_GS, D_MODEL, 1.0), (ga_ref, SEG_GA, D_MODEL, 1.0),
            (dt_ref, SEG_DT, LANES, 1.0))
    step = 512
    for ref, off, width, scale in segs:
        for c0 in range(0, width, step):
            cw = min(step, width - c0)
            r = _dot(xb, w_ref[:, off + c0:off + c0 + cw])
            if scale != 1.0:
                r = r * scale
            ref[:, c0:c0 + cw] = r.astype(ref.dtype)


def _inproj(x2d, g_mix, w_packed):
    t = x2d.shape[0]
    tm = _tile(t, 512)
    widths = (D_INNER, CONV_DIM, ATTN_WIDTH, KV_DUP, KV_DUP, D_MODEL, D_MODEL)
    out_shape = [jax.ShapeDtypeStruct((t, w), BF16) for w in widths] + [jax.ShapeDtypeStruct((t, LANES), F32)]
    out_specs = [pl.BlockSpec((tm, w), lambda i: (i, 0)) for w in widths] + [pl.BlockSpec((tm, LANES), lambda i: (i, 0))]
    return pl.pallas_call(
        _inproj_body,
        grid=(t // tm,),
        in_specs=[pl.BlockSpec((tm, D_MODEL), lambda i: (i, 0)),
                  _const_spec((1, D_MODEL)),
                  _const_spec((D_MODEL, PACKED_COLS))],
        out_specs=out_specs,
        out_shape=out_shape,
        compiler_params=_params(("parallel",), 56),
        name="prompt_inproj",
    )(x2d, g_mix, w_packed)


def _mixer_body(xbc_ref, z_ref, dt_ref, q_ref, k_ref, v_ref, gs_ref, ga_ref, x_ref,
                convw_ref, convb_ref, dtb_ref, alog_ref, dskip_ref, gssd_ref, sinks_ref, wdown_ref, wmix_ref,
                h1_ref, pstate_ref, pconv_ref, pk_ref, pv_ref,
                cbuf, act, ybuf, att, kprev, vprev, state):
    c = pl.program_id(1)
    last = pl.num_programs(1) - 1
    L = CHUNK

    @pl.when(c == 0)
    def _():
        cbuf[0:SUBLANES, :] = jnp.zeros((SUBLANES, CONV_DIM), F32)
        kprev[...] = jnp.zeros_like(kprev)
        vprev[...] = jnp.zeros_like(vprev)
        state[...] = jnp.zeros_like(state)

    cbuf[SUBLANES:SUBLANES + L, :] = xbc_ref[...].astype(F32)
    cstep = 512
    for c0 in range(0, CONV_DIM, cstep):
        cs = slice(c0, c0 + cstep)
        acc = convb_ref[:, cs] + convw_ref[3:4, cs] * cbuf[8:8 + L, cs]
        acc = acc + convw_ref[2:3, cs] * cbuf[7:7 + L, cs]
        acc = acc + convw_ref[1:2, cs] * cbuf[6:6 + L, cs]
        acc = acc + convw_ref[0:1, cs] * cbuf[5:5 + L, cs]
        act[:, cs] = _silu_t(acc)

    cbuf[0:SUBLANES, :] = cbuf[L:L + SUBLANES, :]

    dt = _softplus(dt_ref[...] + dtb_ref[...])
    a_neg = -jnp.exp(alog_ref[...])
    da = dt * a_neg
    ri = lax.broadcasted_iota(jnp.int32, (L, L), 0)
    ci = lax.broadcasted_iota(jnp.int32, (L, L), 1)
    causal = ri >= ci
    tri = jnp.where(causal, 1.0, 0.0).astype(F32)
    acum = _dot(tri, da, precision=HIGHEST)
    acum_t = acum.T
    dt_t = dt.T
    eacum = jnp.exp(acum)
    a_last = acum[L - 1:L, :]
    w_tail = jnp.exp(a_last - acum) * dt
    e_last = jnp.exp(a_last)
    lane = lax.broadcasted_iota(jnp.int32, (L, LANES), 1)
    lo_half = lane < SSD_HEAD_DIM
    lane1 = lax.broadcasted_iota(jnp.int32, (1, LANES), 1)
    lo_half1 = lane1 < SSD_HEAD_DIM

    def ssd_group(g):
        bg = act[:, D_INNER + g * SSD_STATE:D_INNER + (g + 1) * SSD_STATE]
        cg = act[:, D_INNER + (SSD_GROUPS + g) * SSD_STATE:D_INNER + (SSD_GROUPS + g + 1) * SSD_STATE]
        bgb = bg.astype(BF16)
        cgb = cg.astype(BF16)
        cb = _dot_nt(cgb, bgb)
        hg = state[g]
        yoff = _dot(cgb, hg.astype(BF16))
        bgt = bg.T.astype(BF16)
        heads = tuple(range(g * SSD_HPG, (g + 1) * SSD_HPG))
        segs = [acum[:, h:h + 1] - acum_t[h:h + 1, :] for h in heads]
        decays = [jnp.exp(jnp.where(causal, s_, NEG_INF)) for s_ in segs]
        lmats = [(cb * d_ * dt_t[h:h + 1, :]).astype(BF16) for d_, h in zip(decays, heads)]
        xw, dsc = [], []
        for j in range(SSD_HPG // 2):
            h0, h1 = heads[2 * j], heads[2 * j + 1]
            col = h0 * SSD_HEAD_DIM
            xs_pair = act[:, col:col + LANES]
            y2 = _dot(jnp.concatenate([lmats[2 * j], lmats[2 * j + 1]], axis=0), xs_pair.astype(BF16))
            ydiag = jnp.where(lo_half, y2[0:L], y2[L:2 * L])
            esc = jnp.where(lo_half, eacum[:, h0:h0 + 1], eacum[:, h1:h1 + 1])
            ybuf[:, col:col + LANES] = ydiag + yoff[:, 2 * j * SSD_HEAD_DIM:2 * j * SSD_HEAD_DIM + LANES] * esc
            wsc = jnp.where(lo_half, w_tail[:, h0:h0 + 1], w_tail[:, h1:h1 + 1])
            xw.append((xs_pair * wsc).astype(BF16))
            dsc.append(jnp.where(lo_half1, e_last[:, h0:h0 + 1], e_last[:, h1:h1 + 1]))
        upd = _dot(bgt, jnp.concatenate(xw, axis=1))
        state[g] = hg * jnp.concatenate(dsc, axis=1) + upd
        gw = D_INNER // SSD_GROUPS
        gs_ = slice(g * gw, (g + 1) * gw)
        yg = (ybuf[:, gs_] + dskip_ref[:, gs_] * act[:, gs_]) * _silu_t(z_ref[:, gs_].astype(F32))
        ybuf[:, gs_] = _rms(yg, gssd_ref[:, gs_])

    nd_c = jnp.where(causal, (ci - ri).astype(F32), -MASKED_DIST)
    nd_p = jnp.where(ci >= ri + jnp.where(c > 0, 0, L), (ci - ri - L).astype(F32), -MASKED_DIST)
    keep = (jnp.where(lo_half1, 1.0, 0.0).astype(BF16), jnp.where(lo_half1, 0.0, 1.0).astype(BF16))
    def attn_group(kv):
        kd_p = kprev[:, kv * LANES:(kv + 1) * LANES]
        kd_c = k_ref[:, kv * LANES:(kv + 1) * LANES]
        vd_p = vprev[:, kv * LANES:(kv + 1) * LANES]
        vd_c = v_ref[:, kv * LANES:(kv + 1) * LANES]
        heads = tuple(range(kv * GQA, (kv + 1) * GQA))
        pcs = [slice((kv * GQA + 2 * j) * HEAD_DIM, (kv * GQA + 2 * j) * HEAD_DIM + LANES) for j in range(GQA // 2)]
        qs = jnp.concatenate([q_ref[:, pcs[h % GQA // 2]] * keep[h % 2] for h in heads], axis=0)
        s_p = _dot_nt(qs, kd_p) + jnp.concatenate([ALIBI_SLOPES[h] * nd_p for h in heads], axis=0)
        s_c = _dot_nt(qs, kd_c) + jnp.concatenate([ALIBI_SLOPES[h] * nd_c for h in heads], axis=0)
        sink = jnp.concatenate([jnp.full((L, 1), sinks_ref[h], F32) for h in heads], axis=0)
        m = jnp.maximum(jnp.max(jnp.maximum(s_p, s_c), axis=-1, keepdims=True), sink)
        p_p = jnp.exp(s_p - m)
        p_c = jnp.exp(s_c - m)
        denom = jnp.sum(p_p + p_c, axis=-1, keepdims=True) + jnp.exp(sink - m)
        o = (_dot(p_p.astype(BF16), vd_p) + _dot(p_c.astype(BF16), vd_c)) / denom
        for j in range(GQA // 2):
            att[:, pcs[j]] = jnp.where(lo_half, o[2 * j * L:(2 * j + 1) * L], o[(2 * j + 1) * L:(2 * j + 2) * L])

    for g in range(SSD_GROUPS):
        ssd_group(g)
        attn_group(g)
    kprev[...] = k_ref[...]
    vprev[...] = v_ref[...]

    y_ssd = _dot(ybuf[...].astype(BF16), wdown_ref[...])
    merged = _sigmoid_t(gs_ref[...].astype(F32)) * y_ssd + _sigmoid_t(ga_ref[...].astype(F32)) * att[...]
    h1_ref[...] = x_ref[...] + _dot(merged.astype(BF16), wmix_ref[...])

    @pl.when(c == last)
    def _():
        pconv_ref[...] = cbuf[L + SUBLANES - 3:L + SUBLANES, :]
        for g in range(SSD_GROUPS):
            pstate_ref[g] = state[g].T
        pk_ref[...] = k_ref[...].astype(F32)
        pv_ref[...] = v_ref[...].astype(F32)


def _mixer(x, z, xbc, q, k, v, gs, ga, dt, conv_w, conv_b, dtb, alog, dskip_x, g_ssd, sinks, w_down, w_mix):
    b, seq, _ = x.shape
    nc = seq // CHUNK

    def blk(width):
        return pl.BlockSpec((None, CHUNK, width), lambda i, j: (i, j, 0))

    def per_b(*shape):
        nd = len(shape)
        return pl.BlockSpec((None,) + shape, lambda i, j: (i,) + (0,) * nd)

    in_specs = [blk(CONV_DIM), blk(D_INNER), blk(LANES), blk(ATTN_WIDTH), blk(KV_DUP), blk(KV_DUP),
                blk(D_MODEL), blk(D_MODEL), blk(D_MODEL),
                _const_spec((CONV_K, CONV_DIM)), _const_spec((1, CONV_DIM)), _const_spec((1, LANES)),
                _const_spec((1, LANES)), _const_spec((1, D_INNER)), _const_spec((1, D_INNER)),
                pl.BlockSpec(memory_space=pltpu.SMEM),
                _const_spec((D_INNER, D_MODEL)), _const_spec((D_MODEL, D_MODEL))]
    out_shape = [jax.ShapeDtypeStruct((b, seq, D_MODEL), F32),
                 jax.ShapeDtypeStruct((b, SSD_GROUPS, SSD_HPG * SSD_HEAD_DIM, SSD_STATE), F32),
                 jax.ShapeDtypeStruct((b, CONV_K - 1, CONV_DIM), F32),
                 jax.ShapeDtypeStruct((b, WINDOW, KV_DUP), F32),
                 jax.ShapeDtypeStruct((b, WINDOW, KV_DUP), F32)]
    out_specs = [blk(D_MODEL), per_b(SSD_GROUPS, SSD_HPG * SSD_HEAD_DIM, SSD_STATE), per_b(CONV_K - 1, CONV_DIM),
                 per_b(WINDOW, KV_DUP), per_b(WINDOW, KV_DUP)]
    scratch = [pltpu.VMEM((CHUNK + 2 * SUBLANES, CONV_DIM), F32),
               pltpu.VMEM((CHUNK, CONV_DIM), F32),
               pltpu.VMEM((CHUNK, D_INNER), F32),
               pltpu.VMEM((CHUNK, ATTN_WIDTH), F32),
               pltpu.VMEM((CHUNK, KV_DUP), BF16), pltpu.VMEM((CHUNK, KV_DUP), BF16),
               pltpu.VMEM((SSD_GROUPS, SSD_STATE, SSD_HPG * SSD_HEAD_DIM), F32)]
    return pl.pallas_call(
        _mixer_body,
        grid=(b, nc),
        in_specs=in_specs,
        out_specs=out_specs,
        out_shape=out_shape,
        scratch_shapes=scratch,
        compiler_params=_params(("parallel", "arbitrary"), 48),
        name="prompt_mixer",
    )(xbc, z, dt, q, k, v, gs, ga, x, conv_w, conv_b, dtb, alog, dskip_x, g_ssd, sinks, w_down, w_mix)


def _memkv_body(m_ref, g_ref, w_ref, o_ref):
    o_ref[...] = _dot(_rms(m_ref[...], g_ref[...]).astype(BF16), w_ref[...])


def _memkv(mem2d, g_mem, w_ckv):
    t = mem2d.shape[0]
    tm = _tile(t, 256)
    return pl.pallas_call(
        _memkv_body,
        grid=(t // tm,),
        in_specs=[pl.BlockSpec((tm, D_MODEL), lambda i: (i, 0)), _const_spec((1, D_MODEL)),
                  _const_spec((D_MODEL, 2 * C_WIDTH))],
        out_specs=pl.BlockSpec((tm, 2 * C_WIDTH), lambda i: (i, 0)),
        out_shape=jax.ShapeDtypeStruct((t, 2 * C_WIDTH), F32),
        compiler_params=_params(("parallel",), 32),
        name="memory_kv",
    )(mem2d, g_mem, w_ckv)


def _route(logits):
    rows = logits.shape[0]
    lane = lax.broadcasted_iota(jnp.int32, (rows, ROUTE_COLS), 1).astype(F32)
    big = 1e9
    is_g = lane < N_GROUPS
    lg = jnp.where(is_g, logits, NEG_INF)
    gmax = jnp.max(lg, axis=-1, keepdims=True)
    grp = jnp.min(jnp.where(lg == gmax, lane, big), axis=-1, keepdims=True)
    p_grp = 1.0 / jnp.sum(jnp.where(is_g, jnp.exp(lg - gmax), 0.0), axis=-1, keepdims=True)
    lo = N_GROUPS + EXP_PER_GROUP * grp
    in_grp = (lane >= lo) & (lane < lo + EXP_PER_GROUP)
    le = jnp.where(in_grp, logits, NEG_INF)
    m1 = jnp.max(le, axis=-1, keepdims=True)
    i1 = jnp.min(jnp.where(le == m1, lane, big), axis=-1, keepdims=True)
    le2 = jnp.where(lane == i1, NEG_INF, le)
    m2 = jnp.max(le2, axis=-1, keepdims=True)
    i2 = jnp.min(jnp.where(le2 == m2, lane, big), axis=-1, keepdims=True)
    t2 = jnp.exp(m2 - m1)
    g1 = p_grp / (1.0 + t2)
    g2 = p_grp * t2 / (1.0 + t2)
    info = jnp.where(lane == 0, i1 - N_GROUPS,
                     jnp.where(lane == 1, i2 - N_GROUPS,
                               jnp.where(lane == 2, g1, jnp.where(lane == 3, g2, 0.0))))
    return info


def _cross_body(h1_ref, mkv_ref, gc_ref, wcq_ref, wco_ref, gm_ref, wr_ref, br_ref,
                h2_ref, xm_ref, route_ref, obuf):
    h1 = h1_ref[...]
    xn = _rms(h1, gc_ref[...]).astype(BF16)
    qc = _dot(xn, wcq_ref[...])
    scale = C_HEAD_DIM ** -0.5
    for h in range(C_HEADS):
        hs = slice(h * C_HEAD_DIM, (h + 1) * C_HEAD_DIM)
        mk = mkv_ref[:, hs].astype(BF16)
        mv = mkv_ref[:, C_WIDTH + h * C_HEAD_DIM:C_WIDTH + (h + 1) * C_HEAD_DIM].astype(BF16)
        s = _dot_nt(qc[:, hs].astype(BF16), mk) * scale
        m = jnp.max(s, axis=-1, keepdims=True)
        p = jnp.exp(s - m)
        obuf[:, hs] = _dot(p.astype(BF16), mv) / jnp.sum(p, axis=-1, keepdims=True)
    h2 = h1 + _dot(obuf[...].astype(BF16), wco_ref[...])
    h2_ref[...] = h2
    xm = _rms(h2, gm_ref[...])
    xm_ref[...] = xm
    logits = _dot(xm.astype(BF16), wr_ref[...]) + br_ref[...]
    route_ref[...] = _route(logits).T[0:SUBLANES, :]


def _cross(h1, mkv, g_cross, w_cq, w_co, g_moe, w_r, b_r):
    b, seq, _ = h1.shape
    tq = _tile(seq, 1024)
    nq = seq // tq
    return pl.pallas_call(
        _cross_body,
        grid=(b, nq),
        in_specs=[pl.BlockSpec((None, tq, D_MODEL), lambda i, j: (i, j, 0)),
                  pl.BlockSpec((None, MEM_LEN, 2 * C_WIDTH), lambda i, j: (i, 0, 0)),
                  _const_spec((1, D_MODEL)), _const_spec((D_MODEL, C_WIDTH)), _const_spec((C_WIDTH, D_MODEL)),
                  _const_spec((1, D_MODEL)), _const_spec((D_MODEL, ROUTE_COLS)), _const_spec((1, ROUTE_COLS))],
        out_specs=[pl.BlockSpec((None, tq, D_MODEL), lambda i, j: (i, j, 0)),
                   pl.BlockSpec((None, tq, D_MODEL), lambda i, j: (i, j, 0)),
                   pl.BlockSpec((SUBLANES, tq), lambda i, j: (0, i * nq + j))],
        out_shape=[jax.ShapeDtypeStruct((b, seq, D_MODEL), F32),
                   jax.ShapeDtypeStruct((b, seq, D_MODEL), F32),
                   jax.ShapeDtypeStruct((SUBLANES, b * seq), F32)],
        scratch_shapes=[pltpu.VMEM((tq, C_WIDTH), F32)],
        compiler_params=_params(("parallel", "parallel"), 48),
        name="prompt_cross_route",
    )(h1, mkv, g_cross, w_cq, w_co, g_moe, w_r, b_r)


RANK_TILE = 512


def _rank_body(route_ref, rank_ref, count_ref, carry):
    i = pl.program_id(0)

    @pl.when(i == 0)
    def _():
        carry[...] = jnp.zeros_like(carry)

    e1 = route_ref[0:1, :]
    e2 = route_ref[1:2, :]
    eid = lax.broadcasted_iota(jnp.int32, (N_EXPERTS, RANK_TILE), 0).astype(F32)
    is1 = e1 == eid
    is2 = e2 == eid
    onehot = jnp.where(is1 | is2, 1.0, 0.0)
    si = lax.broadcasted_iota(jnp.int32, (RANK_TILE, RANK_TILE), 0)
    ti = lax.broadcasted_iota(jnp.int32, (RANK_TILE, RANK_TILE), 1)
    before = jnp.where(si < ti, 1.0, 0.0).astype(BF16)
    prefix = _dot(onehot.astype(BF16), before) + carry[:, 0:1]
    r1 = jnp.sum(jnp.where(is1, prefix, 0.0), axis=0, keepdims=True)
    r2 = jnp.sum(jnp.where(is2, prefix, 0.0), axis=0, keepdims=True)
    row = lax.broadcasted_iota(jnp.int32, (SUBLANES, RANK_TILE), 0)
    rank_ref[...] = jnp.where(row == 0, r1, jnp.where(row == 1, r2, 0.0))
    carry[...] = carry[...] + jnp.sum(onehot, axis=1, keepdims=True)
    count_ref[...] = carry[...]


def _rank(route):
    tp = route.shape[1]
    return pl.pallas_call(
        _rank_body,
        grid=(tp // RANK_TILE,),
        in_specs=[pl.BlockSpec((SUBLANES, RANK_TILE), lambda i: (0, i))],
        out_specs=[pl.BlockSpec((SUBLANES, RANK_TILE), lambda i: (0, i)),
                   pl.BlockSpec((N_EXPERTS, LANES), lambda i: (0, 0))],
        out_shape=[jax.ShapeDtypeStruct((SUBLANES, tp), F32), jax.ShapeDtypeStruct((N_EXPERTS, LANES), F32)],
        scratch_shapes=[pltpu.VMEM((N_EXPERTS, LANES), F32)],
        compiler_params=_params(("arbitrary",), 32),
        name="moe_rank",
    )(route)


def _dest_body(route_ref, rank_ref, offs_ref, dest_ref):
    e1 = route_ref[0:1, :]
    e2 = route_ref[1:2, :]
    eid = lax.broadcasted_iota(jnp.int32, (N_EXPERTS, RANK_TILE), 0).astype(F32)
    offs = offs_ref[:, 0:1]
    d1 = jnp.sum(jnp.where(e1 == eid, offs, 0.0), axis=0, keepdims=True) + rank_ref[0:1, :]
    d2 = jnp.sum(jnp.where(e2 == eid, offs, 0.0), axis=0, keepdims=True) + rank_ref[1:2, :]
    row = lax.broadcasted_iota(jnp.int32, (SUBLANES, RANK_TILE), 0)
    dest_ref[...] = jnp.where(row == 0, d1, jnp.where(row == 1, d2, 0.0)).astype(jnp.int32)


def _dest(route, rank, offs):
    tp = route.shape[1]
    return pl.pallas_call(
        _dest_body,
        grid=(tp // RANK_TILE,),
        in_specs=[pl.BlockSpec((SUBLANES, RANK_TILE), lambda i: (0, i)),
                  pl.BlockSpec((SUBLANES, RANK_TILE), lambda i: (0, i)),
                  pl.BlockSpec((N_EXPERTS, LANES), lambda i: (0, 0))],
        out_specs=pl.BlockSpec((SUBLANES, RANK_TILE), lambda i: (0, i)),
        out_shape=jax.ShapeDtypeStruct((SUBLANES, tp), jnp.int32),
        compiler_params=_params(("parallel",), 32),
        name="moe_dest",
    )(route, rank, offs)


DMA_UNROLL = 8


def _row_copy(src, dst, s_row, d_row, sem):
    return pltpu.make_async_copy(src.at[pl.ds(s_row, 1)], dst.at[pl.ds(d_row, 1)], sem)


def _dispatch_body(td, dest_ref, xm_ref, xb_in_ref, xb_ref, sem):
    del xb_in_ref
    base = pl.program_id(0) * td

    def issue(g8, carry):
        t0 = pl.multiple_of(g8 * DMA_UNROLL, DMA_UNROLL)
        for k in range(DMA_UNROLL):
            row = base + t0 + k
            _row_copy(xm_ref, xb_ref, t0 + k, dest_ref[2 * row], sem).start()
            _row_copy(xm_ref, xb_ref, t0 + k, dest_ref[2 * row + 1], sem).start()
        return carry

    lax.fori_loop(0, td // DMA_UNROLL, issue, 0)

    def drain(g8, carry):
        for _ in range(2 * DMA_UNROLL):
            _row_copy(xm_ref, xb_ref, 0, 0, sem).wait()
        return carry

    lax.fori_loop(0, td // DMA_UNROLL, drain, 0)


def _dispatch(dest_flat, xm2d, xb):
    t = xm2d.shape[0]
    td = _tile(t, 256)
    return pl.pallas_call(
        functools.partial(_dispatch_body, td),
        grid_spec=pltpu.PrefetchScalarGridSpec(
            num_scalar_prefetch=1,
            grid=(t // td,),
            in_specs=[pl.BlockSpec((td, D_MODEL), lambda i, d: (i, 0)), pl.BlockSpec(memory_space=pl.ANY)],
            out_specs=pl.BlockSpec(memory_space=pl.ANY),
            scratch_shapes=[pltpu.SemaphoreType.DMA],
        ),
        out_shape=jax.ShapeDtypeStruct(xb.shape, xb.dtype),
        input_output_aliases={2: 0},
        compiler_params=_params(("arbitrary",), 32),
        name="moe_dispatch",
    )(dest_flat, xm2d, xb)


def _expert_body(be_ref, nused_ref, xb_ref, wg_ref, wu_ref, wd_ref, yb_ref, wg_bf, wu_bf, wd_bf):
    i = pl.program_id(0)

    @pl.when(jnp.logical_or(i == 0, be_ref[i] != be_ref[jnp.maximum(i - 1, 0)]))
    def _():
        wg_bf[...] = wg_ref[...].astype(BF16)
        wu_bf[...] = wu_ref[...].astype(BF16)
        wd_bf[...] = wd_ref[...].astype(BF16)

    @pl.when(i < nused_ref[0])
    def _():
        x = xb_ref[...].astype(BF16)
        hmid = _silu(_dot(x, wg_bf[...])) * _dot(x, wu_bf[...])
        yb_ref[...] = _dot(hmid.astype(BF16), wd_bf[...])

    @pl.when(i >= nused_ref[0])
    def _():
        yb_ref[...] = jnp.zeros_like(yb_ref)


def _experts(block_e, n_used, xb, wg, wu, wd):
    rows = xb.shape[0]
    nb = rows // MOE_ROWS

    def xmap(i, be, nu):
        return (jnp.minimum(i, nu[0] - 1), 0)

    def wmap(i, be, nu):
        return (be[i], 0, 0)

    return pl.pallas_call(
        _expert_body,
        grid_spec=pltpu.PrefetchScalarGridSpec(
            num_scalar_prefetch=2,
            grid=(nb,),
            in_specs=[pl.BlockSpec((MOE_ROWS, D_MODEL), xmap),
                      pl.BlockSpec((None, D_MODEL, D_EXPERT), wmap),
                      pl.BlockSpec((None, D_MODEL, D_EXPERT), wmap),
                      pl.BlockSpec((None, D_EXPERT, D_MODEL), wmap)],
            out_specs=pl.BlockSpec((MOE_ROWS, D_MODEL), lambda i, be, nu: (i, 0)),
            scratch_shapes=[pltpu.VMEM((D_MODEL, D_EXPERT), BF16), pltpu.VMEM((D_MODEL, D_EXPERT), BF16),
                            pltpu.VMEM((D_EXPERT, D_MODEL), BF16)],
        ),
        out_shape=jax.ShapeDtypeStruct((rows, D_MODEL), F32),
        compiler_params=_params(("arbitrary",), 48),
        name="moe_experts",
    )(block_e, n_used, xb, wg, wu, wd)


def _combine_body(tc, dest_ref, yb_ref, h2_ref, gate_ref, gf_ref, y_ref, buf, sem):
    i = pl.program_id(0)

    def issue(step, slot):
        base = step * tc

        def grp(g8, carry):
            t0 = pl.multiple_of(g8 * DMA_UNROLL, DMA_UNROLL)
            for k in range(DMA_UNROLL):
                row = base + t0 + k
                for e in range(2):
                    pltpu.make_async_copy(yb_ref.at[pl.ds(dest_ref[2 * row + e], 1)],
                                          buf.at[slot, e, pl.ds(t0 + k, 1)], sem.at[slot]).start()
            return carry

        lax.fori_loop(0, tc // DMA_UNROLL, grp, 0)

    @pl.when(i == 0)
    def _():
        issue(0, 0)

    @pl.when(i + 1 < pl.num_programs(0))
    def _():
        issue(i + 1, (i + 1) % 2)

    slot = i % 2

    def drain(g8, carry):
        for _ in range(2 * DMA_UNROLL):
            pltpu.make_async_copy(yb_ref.at[pl.ds(0, 1)], buf.at[slot, 0, pl.ds(0, 1)], sem.at[slot]).wait()
        return carry

    lax.fori_loop(0, tc // DMA_UNROLL, drain, 0)
    g = gate_ref[...]
    out = h2_ref[...] + (g[:, 0:1] * buf[slot, 0] + g[:, 1:2] * buf[slot, 1])
    y_ref[...] = _rms(out, gf_ref[...])


def _combine(dest_flat, yb, h2_2d, gates, g_final):
    t = h2_2d.shape[0]
    tc = _tile(t, 256)
    return pl.pallas_call(
        functools.partial(_combine_body, tc),
        grid_spec=pltpu.PrefetchScalarGridSpec(
            num_scalar_prefetch=1,
            grid=(t // tc,),
            in_specs=[pl.BlockSpec(memory_space=pl.ANY),
                      pl.BlockSpec((tc, D_MODEL), lambda i, d: (i, 0)),
                      pl.BlockSpec((tc, 2), lambda i, d: (i, 0)),
                      pl.BlockSpec((1, D_MODEL), lambda i, d: (0, 0))],
            out_specs=pl.BlockSpec((tc, D_MODEL), lambda i, d: (i, 0)),
            scratch_shapes=[pltpu.VMEM((2, 2, tc, D_MODEL), F32), pltpu.SemaphoreType.DMA((2,))],
        ),
        out_shape=jax.ShapeDtypeStruct((t, D_MODEL), F32),
        compiler_params=_params(("arbitrary",), 32),
        name="moe_combine",
    )(dest_flat, yb, h2_2d, gates, g_final)


def _s_inproj_body(x_ref, g_ref, w_ref, o_ref):
    o_ref[...] = _dot(_rms(x_ref[...], g_ref[...]).astype(BF16), w_ref[...])


def _s_inproj(x, g_mix, w_packed_f32):
    n = x.shape[0]
    tn = PACKED_COLS // 2
    return pl.pallas_call(
        _s_inproj_body,
        grid=(PACKED_COLS // tn,),
        in_specs=[pl.BlockSpec((n, D_MODEL), lambda j: (0, 0)), pl.BlockSpec((1, D_MODEL), lambda j: (0, 0)),
                  pl.BlockSpec((D_MODEL, tn), lambda j: (0, j))],
        out_specs=pl.BlockSpec((n, tn), lambda j: (0, j)),
        out_shape=jax.ShapeDtypeStruct((n, PACKED_COLS), F32),
        compiler_params=_params(("parallel",), 48),
        name="sample_inproj",
    )(x, g_mix, w_packed_f32)


def _s_conv_body(proj_ref, cs_ref, convw_ref, convb_ref, dtb_ref, alog_ref, xexp_ref,
                 act_ref, ncs_ref, dtx_ref, e_ref):
    step = 512
    for c0 in range(0, CONV_DIM, step):
        cs = slice(c0, c0 + step)
        s0 = cs_ref[:, c0:c0 + step]
        s1 = cs_ref[:, CONV_DIM + c0:CONV_DIM + c0 + step]
        s2 = cs_ref[:, 2 * CONV_DIM + c0:2 * CONV_DIM + c0 + step]
        xn = proj_ref[:, SEG_XBC + c0:SEG_XBC + c0 + step]
        acc = convb_ref[:, cs] + convw_ref[0:1, cs] * s0
        acc = acc + convw_ref[1:2, cs] * s1
        acc = acc + convw_ref[2:3, cs] * s2
        acc = acc + convw_ref[3:4, cs] * xn
        act_ref[:, cs] = _silu(acc)
        ncs_ref[:, c0:c0 + step] = s1
        ncs_ref[:, CONV_DIM + c0:CONV_DIM + c0 + step] = s2
        ncs_ref[:, 2 * CONV_DIM + c0:2 * CONV_DIM + c0 + step] = xn
    dt = _softplus(proj_ref[:, SEG_DT:SEG_DT + LANES] + dtb_ref[...])
    e_ref[...] = jnp.exp(dt * (-jnp.exp(alog_ref[...])))
    dtx_ref[...] = _dot(dt, xexp_ref[...], precision=HIGHEST) * act_ref[:, 0:D_INNER]


def _s_conv(proj, conv_state2d, conv_w, conv_b, dtb, alog, xexp):
    n = proj.shape[0]
    return pl.pallas_call(
        _s_conv_body,
        out_shape=[jax.ShapeDtypeStruct((n, CONV_DIM), F32), jax.ShapeDtypeStruct((n, 3 * CONV_DIM), F32),
                   jax.ShapeDtypeStruct((n, D_INNER), F32), jax.ShapeDtypeStruct((n, LANES), F32)],
        compiler_params=pltpu.CompilerParams(vmem_limit_bytes=48 * 1024 * 1024),
        name="sample_conv",
    )(proj, conv_state2d, conv_w, conv_b, dtb, alog, xexp)


def _s_state_body(bb, e_ref, st_ref, dtxt_ref, b_ref, c_ref, so_ref, yt_ref):
    base = pl.program_id(0) * bb
    lane = lax.broadcasted_iota(jnp.int32, (SSD_HEAD_DIM, LANES), 1)

    for bl in range(bb):
        for h in range(SSD_HEADS):
            g = h // SSD_HPG
            brow = b_ref[bl, :, g * SSD_STATE:(g + 1) * SSD_STATE]
            xcol = dtxt_ref[bl, :, h:h + 1]
            so_ref[bl, h] = st_ref[bl, h] * e_ref[base + bl, h] + xcol * brow
        yacc = jnp.zeros((SSD_HEAD_DIM, LANES), F32)
        for h in range(SSD_HEADS):
            g = h // SSD_HPG
            crow = c_ref[bl, :, g * SSD_STATE:(g + 1) * SSD_STATE]
            yacc = jnp.where(lane == h, jnp.sum(so_ref[bl, h] * crow, axis=-1, keepdims=True), yacc)
        yt_ref[bl] = yacc


def _s_state(e, state, dtxt, bmat, cmat):
    n = state.shape[0]
    bb = _tile(n, 2, 1)
    sblk = (bb, SSD_HEADS, SSD_HEAD_DIM, SSD_STATE)
    bmat = bmat.reshape(n, 1, SSD_GROUPS * SSD_STATE)
    cmat = cmat.reshape(n, 1, SSD_GROUPS * SSD_STATE)
    return pl.pallas_call(
        functools.partial(_s_state_body, bb),
        grid=(n // bb,),
        in_specs=[pl.BlockSpec(memory_space=pltpu.SMEM),
                  pl.BlockSpec(sblk, lambda i: (i, 0, 0, 0)),
                  pl.BlockSpec((bb, SSD_HEAD_DIM, SSD_HEADS), lambda i: (i, 0, 0)),
                  pl.BlockSpec((bb, 1, SSD_GROUPS * SSD_STATE), lambda i: (i, 0, 0)),
                  pl.BlockSpec((bb, 1, SSD_GROUPS * SSD_STATE), lambda i: (i, 0, 0))],
        out_specs=[pl.BlockSpec(sblk, lambda i: (i, 0, 0, 0)),
                   pl.BlockSpec((bb, SSD_HEAD_DIM, LANES), lambda i: (i, 0, 0))],
        out_shape=[jax.ShapeDtypeStruct(state.shape, F32), jax.ShapeDtypeStruct((n, SSD_HEAD_DIM, LANES), F32)],
        compiler_params=_params(("parallel",), 40),
        name="sample_ssd_state",
    )(e, state, dtxt, bmat, cmat)


def _s_attn_body(bb, qexp_ref, ck_ref, cv_ref, kn_ref, vn_ref, sink_ref, slope_ref,
                 y_ref, ok_ref, ov_ref):
    W = WINDOW
    lane = lax.broadcasted_iota(jnp.int32, (W, LANES), 1)
    lane1 = lax.broadcasted_iota(jnp.int32, (1, LANES), 1)
    jrow = lax.broadcasted_iota(jnp.int32, (W, LANES), 0)
    bias = slope_ref[...] * (W - jrow).astype(F32)
    sink = sink_ref[...]
    lo_half = lane1 < HEAD_DIM

    for bl in range(bb):
        kn = kn_ref[bl]
        vn = vn_ref[bl]
        kmat = _bf16_round(ck_ref[bl])
        vmat = _bf16_round(cv_ref[bl])
        knr = _bf16_round(kn)
        vnr = _bf16_round(vn)
        s = jnp.zeros((W, LANES), F32)
        sn = jnp.zeros((1, LANES), F32)
        for h in range(N_HEADS):
            qrow = _bf16_round(qexp_ref[bl, h:h + 1, :])
            s = jnp.where(lane == h, jnp.sum(kmat * qrow, axis=-1, keepdims=True), s)
            sn = jnp.where(lane1 == h, jnp.sum(knr * qrow, axis=-1, keepdims=True), sn)
        s = s * (HEAD_DIM ** -0.5) - bias
        sn = sn * (HEAD_DIM ** -0.5)
        m = jnp.maximum(jnp.maximum(jnp.max(s, axis=0, keepdims=True), sn), sink)
        p = jnp.exp(s - m)
        pn = jnp.exp(sn - m)
        denom = jnp.sum(p, axis=0, keepdims=True) + pn + jnp.exp(sink - m)
        p = _bf16_round(p / denom)
        pn = _bf16_round(pn / denom)
        for j in range(N_HEADS // 2):
            outs = []
            for h in (2 * j, 2 * j + 1):
                kv = h // GQA
                cs = slice((kv // 2) * LANES, (kv // 2 + 1) * LANES)
                o = jnp.sum(p[:, h:h + 1] * vmat[:, cs], axis=0, keepdims=True) + pn[:, h:h + 1] * vnr[:, cs]
                if (kv % 2) != (h % 2):
                    o = pltpu.roll(o, HEAD_DIM, axis=1)
                outs.append(o)
            y_ref[bl, :, j * LANES:(j + 1) * LANES] = jnp.where(lo_half, outs[0], outs[1])
        ok_ref[bl, 0:W - 1, :] = ck_ref[bl, 1:W, :]
        ok_ref[bl, W - 1:W, :] = kn
        ov_ref[bl, 0:W - 1, :] = cv_ref[bl, 1:W, :]
        ov_ref[bl, W - 1:W, :] = vn


def _s_attn(qexp, ck, cv, kn, vn, sink_row, slope_row):
    n = ck.shape[0]
    bb = _tile(n, 2, 1)
    cblk = pl.BlockSpec((bb, WINDOW, KV_WIDTH), lambda i: (i, 0, 0))
    rblk = pl.BlockSpec((bb, 1, KV_WIDTH), lambda i: (i, 0, 0))
    y, ok, ov = pl.pallas_call(
        functools.partial(_s_attn_body, bb),
        grid=(n // bb,),
        in_specs=[pl.BlockSpec((bb, N_HEADS, KV_WIDTH), lambda i: (i, 0, 0)), cblk, cblk, rblk, rblk,
                  pl.BlockSpec((1, LANES), lambda i: (0, 0)), pl.BlockSpec((1, LANES), lambda i: (0, 0))],
        out_specs=[pl.BlockSpec((bb, 1, ATTN_WIDTH), lambda i: (i, 0, 0)), cblk, cblk],
        out_shape=[jax.ShapeDtypeStruct((n, 1, ATTN_WIDTH), F32), jax.ShapeDtypeStruct(ck.shape, F32),
                   jax.ShapeDtypeStruct(cv.shape, F32)],
        compiler_params=_params(("parallel",), 32),
        name="sample_window_attn",
    )(qexp, ck, cv, kn.reshape(n, 1, KV_WIDTH), vn.reshape(n, 1, KV_WIDTH), sink_row, slope_row)
    return y.reshape(n, ATTN_WIDTH), ok, ov


def _s_post_body(y_ref, act_ref, proj_ref, att_ref, x_ref, dskip_ref, gssd_ref, wdown_ref, wmix_ref,
                 gc_ref, wcq_ref, h1_ref, qc_ref, ybuf):
    gw = D_INNER // SSD_GROUPS
    for g in range(SSD_GROUPS):
        gs_ = slice(g * gw, (g + 1) * gw)
        yg = (y_ref[:, gs_] + dskip_ref[:, gs_] * act_ref[:, gs_]) * _silu(proj_ref[:, SEG_Z + g * gw:SEG_Z + (g + 1) * gw])
        ybuf[:, gs_] = _rms(yg, gssd_ref[:, gs_])
    y_ssd = _dot(ybuf[...].astype(BF16), wdown_ref[...])
    merged = (_sigmoid(proj_ref[:, SEG_GS:SEG_GS + D_MODEL]) * y_ssd
              + _sigmoid(proj_ref[:, SEG_GA:SEG_GA + D_MODEL]) * att_ref[...])
    h1 = x_ref[...] + _dot(merged.astype(BF16), wmix_ref[...])
    h1_ref[...] = h1
    qc_ref[...] = _bf16_round(_dot(_rms(h1, gc_ref[...]).astype(BF16), wcq_ref[...]))


def _s_post(y, act, proj, att, x, dskip_x, g_ssd, w_down, w_mix, g_cross, w_cq):
    n = x.shape[0]
    return pl.pallas_call(
        _s_post_body,
        out_shape=[jax.ShapeDtypeStruct((n, D_MODEL), F32), jax.ShapeDtypeStruct((n, C_WIDTH), F32)],
        scratch_shapes=[pltpu.VMEM((n, D_INNER), F32)],
        compiler_params=pltpu.CompilerParams(vmem_limit_bytes=48 * 1024 * 1024),
        name="sample_post_mixer",
    )(y, act, proj, att, x, dskip_x, g_ssd, w_down, w_mix, g_cross, w_cq)


def _s_cross_body(bb, qc_ref, mk_ref, mv_ref, o_ref):
    lane = lax.broadcasted_iota(jnp.int32, (MEM_LEN, LANES), 1)
    scale = C_HEAD_DIM ** -0.5

    for bl in range(bb):
        q = qc_ref[bl]
        s = jnp.zeros((MEM_LEN, LANES), F32)
        for h in range(C_HEADS):
            hs = slice(h * C_HEAD_DIM, (h + 1) * C_HEAD_DIM)
            kmat = _bf16_round(mk_ref[bl, :, h, :])
            s = jnp.where(lane == h, jnp.sum(kmat * q[:, hs], axis=-1, keepdims=True), s)
        s = s * scale
        m = jnp.max(s, axis=0, keepdims=True)
        p = jnp.exp(s - m)
        p = _bf16_round(p / jnp.sum(p, axis=0, keepdims=True))
        for h in range(C_HEADS):
            hs = slice(h * C_HEAD_DIM, (h + 1) * C_HEAD_DIM)
            vmat = _bf16_round(mv_ref[bl, :, h, :])
            o_ref[bl, :, hs] = jnp.sum(p[:, h:h + 1] * vmat, axis=0, keepdims=True)


def _s_cross(qc, mk, mv):
    n = qc.shape[0]
    bb = _tile(n, 4, 1)
    mblk = pl.BlockSpec((bb, MEM_LEN, C_HEADS, C_HEAD_DIM), lambda i: (i, 0, 0, 0))
    rblk = pl.BlockSpec((bb, 1, C_WIDTH), lambda i: (i, 0, 0))
    return pl.pallas_call(
        functools.partial(_s_cross_body, bb),
        grid=(n // bb,),
        in_specs=[rblk, mblk, mblk],
        out_specs=rblk,
        out_shape=jax.ShapeDtypeStruct((n, 1, C_WIDTH), F32),
        compiler_params=_params(("parallel",), 40),
        name="sample_cross_attn",
    )(qc.reshape(n, 1, C_WIDTH), mk, mv).reshape(n, C_WIDTH)


def _s_route_body(o_ref, h1_ref, wco_ref, gm_ref, wr_ref, br_ref, h2_ref, xm_ref, route_ref):
    h2 = h1_ref[...] + _dot(o_ref[...].astype(BF16), wco_ref[...])
    h2_ref[...] = h2
    xm = _rms(h2, gm_ref[...])
    xm_ref[...] = xm
    logits = _dot(xm.astype(BF16), wr_ref[...]) + br_ref[...]
    route_ref[...] = _route(logits).T[0:SUBLANES, :]


def _s_route(o, h1, w_co, g_moe, w_r, b_r):
    n = o.shape[0]
    return pl.pallas_call(
        _s_route_body,
        out_shape=[jax.ShapeDtypeStruct((n, D_MODEL), F32), jax.ShapeDtypeStruct((n, D_MODEL), F32),
                   jax.ShapeDtypeStruct((SUBLANES, n), F32)],
        compiler_params=pltpu.CompilerParams(vmem_limit_bytes=32 * 1024 * 1024),
        name="sample_cross_out_route",
    )(o, h1, w_co, g_moe, w_r, b_r)


def _pack_in_weights(w_in):
    cuts = np.cumsum((D_INNER, CONV_DIM, SSD_HEADS, ATTN_WIDTH, KV_WIDTH, KV_WIDTH, D_MODEL, D_MODEL))[:-1]
    z, xbc, dt, q, k, v, gs, ga = jnp.split(w_in, [int(c) for c in cuts], axis=1)
    dt = jnp.pad(dt, ((0, 0), (0, PACKED_COLS - SEG_DT - SSD_HEADS)))

    def dup(w):
        w = w.reshape(w.shape[0], N_KV, 1, HEAD_DIM)
        return jnp.broadcast_to(w, (w.shape[0], N_KV, 2, HEAD_DIM)).reshape(w.shape[0], KV_DUP)

    return jnp.concatenate([z, xbc, q, dup(k), dup(v), gs, ga, dt], axis=1)


def _undup(x):
    lead = x.shape[:-1]
    return x.reshape(lead + (N_KV, 2, HEAD_DIM))[..., 0, :].reshape(lead + (KV_WIDTH,))


def _head_expand_matrix():
    m = np.zeros((LANES, D_INNER), np.float32)
    for h in range(SSD_HEADS):
        m[h, h * SSD_HEAD_DIM:(h + 1) * SSD_HEAD_DIM] = 1.0
    return m


def _row(v, width=None):
    v = v.reshape(1, -1)
    if width is not None and v.shape[1] < width:
        v = jnp.pad(v, ((0, 0), (0, width - v.shape[1])))
    return v


def kernel(x_prompt, x_sample, state_ssd, state_conv, cache_win_k, cache_win_v, cache_mem_k, cache_mem_v, mem_prompt, g_mix, w_in, conv_w, conv_b, dt_bias, a_log, d_skip, g_ssd, w_ssd_down, attn_sinks, w_mix_out, g_cross, g_mem, w_cq, w_ckv, w_co, g_moe, w_route_group, b_route_group, w_route_expert, b_route_expert, w_e_gate, w_e_up, w_e_down, g_final):
    assert g_mix.shape[0] == 1, "single layer"
    b, seq, _ = x_prompt.shape
    n_s = x_sample.shape[0]
    t_p = b * seq
    assert seq % CHUNK == 0 and x_sample.shape[1] == 1

    w_packed_bf = _pack_in_weights(w_in[0].astype(BF16))
    w_down_bf, w_mix_bf = w_ssd_down[0].astype(BF16), w_mix_out[0].astype(BF16)
    w_cq_bf, w_co_bf = w_cq[0].astype(BF16), w_co[0].astype(BF16)
    g_mix_r, g_cross_r, g_mem_r, g_moe_r, g_final_r = (_row(g_mix[0]), _row(g_cross[0]), _row(g_mem[0]),
                                                        _row(g_moe[0]), _row(g_final))
    conv_b_r = _row(conv_b[0])
    dtb_r = _row(dt_bias[0], LANES)
    alog_r = _row(a_log[0], LANES)
    dskip_x = _row(jnp.repeat(d_skip[0], SSD_HEAD_DIM))
    g_ssd_r = _row(g_ssd[0])
    w_r = jnp.pad(jnp.concatenate([w_route_group[0], w_route_expert[0]], axis=1),
                  ((0, 0), (0, ROUTE_COLS - N_GROUPS - N_EXPERTS))).astype(BF16)
    b_r = _row(jnp.concatenate([b_route_group[0], b_route_expert[0]]), ROUTE_COLS)

    z, xbc, q, k, v, gs, ga, dt = _inproj(x_prompt.reshape(t_p, D_MODEL), g_mix_r, w_packed_bf)
    r3 = lambda a: a.reshape(b, seq, a.shape[-1])
    h1, p_state, p_conv, p_wk, p_wv = _mixer(
        x_prompt, r3(z), r3(xbc), r3(q), r3(k), r3(v), r3(gs), r3(ga), r3(dt),
        conv_w[0], conv_b_r, dtb_r, alog_r, dskip_x, g_ssd_r, attn_sinks[0], w_down_bf, w_mix_bf)
    mkv = _memkv(mem_prompt.reshape(b * MEM_LEN, D_MODEL), g_mem_r, w_ckv[0].astype(BF16))
    h2_p, xm_p, route_p = _cross(h1, mkv.reshape(b, MEM_LEN, 2 * C_WIDTH), g_cross_r,
                                 w_cq_bf, w_co_bf, g_moe_r, w_r, b_r)

    xs2 = x_sample.reshape(n_s, D_MODEL)
    proj = _s_inproj(xs2, g_mix_r, w_packed_bf)
    xexp = jnp.asarray(_head_expand_matrix())
    act_s, s_conv, dtx, e_s = _s_conv(proj, state_conv[0].reshape(n_s, 3 * CONV_DIM), conv_w[0], conv_b_r,
                                      dtb_r, alog_r, xexp)
    dtxt = dtx.reshape(n_s, SSD_HEADS, SSD_HEAD_DIM).transpose(0, 2, 1)
    s_state, yt = _s_state(e_s, state_ssd[0], dtxt, act_s[:, D_INNER:D_INNER + SSD_GROUPS * SSD_STATE],
                           act_s[:, D_INNER + SSD_GROUPS * SSD_STATE:])
    y_s = yt[:, :, :SSD_HEADS].transpose(0, 2, 1).reshape(n_s, D_INNER)
    slope_row = _row(jnp.asarray(ALIBI_SLOPES, F32), LANES)
    q_s = proj[:, SEG_Q:SEG_Q + ATTN_WIDTH].reshape(n_s, N_KV, GQA, 1, HEAD_DIM)
    kv_eye = jnp.eye(N_KV, dtype=F32).reshape(1, N_KV, 1, N_KV, 1)
    qexp = (q_s * kv_eye).reshape(n_s, N_HEADS, KV_WIDTH)
    att_s, s_wk, s_wv = _s_attn(qexp, cache_win_k[0].reshape(n_s, WINDOW, KV_WIDTH),
                                cache_win_v[0].reshape(n_s, WINDOW, KV_WIDTH),
                                _undup(proj[:, SEG_K:SEG_K + KV_DUP]), _undup(proj[:, SEG_V:SEG_V + KV_DUP]),
                                _row(attn_sinks[0], LANES), slope_row)
    h1_s, qc_s = _s_post(y_s, act_s, proj, att_s, xs2, dskip_x, g_ssd_r, w_down_bf, w_mix_bf, g_cross_r, w_cq_bf)
    o_s = _s_cross(qc_s, cache_mem_k[0], cache_mem_v[0])
    h2_s, xm_s, route_s = _s_route(o_s, h1_s, w_co_bf, g_moe_r, w_r, b_r)

    t_all = t_p + n_s
    t_pad = -(-t_all // RANK_TILE) * RANK_TILE
    route_all = jnp.concatenate([route_p, route_s, jnp.full((SUBLANES, t_pad - t_all), -1.0, F32)], axis=1)
    rank, counts = _rank(route_all)
    cnt = counts[:, 0].astype(jnp.int32)
    padded = (cnt + MOE_ROWS - 1) // MOE_ROWS * MOE_ROWS
    pad_end = jnp.cumsum(padded)
    offs = (pad_end - padded).astype(F32)
    nb = -(-(2 * t_all) // MOE_ROWS) + N_EXPERTS
    block_start = jnp.arange(nb, dtype=jnp.int32) * MOE_ROWS
    block_e = jnp.minimum(jnp.sum((pad_end[None, :] <= block_start[:, None]).astype(jnp.int32), axis=1),
                          N_EXPERTS - 1)
    n_used = (pad_end[-1] // MOE_ROWS).astype(jnp.int32).reshape(1)
    dest = _dest(route_all, rank, jnp.broadcast_to(offs[:, None], (N_EXPERTS, LANES)))
    dest_p = dest[0:2, :t_p].T.reshape(-1)
    dest_s = dest[0:2, t_p:t_all].T.reshape(-1)
    xb = jnp.zeros((nb * MOE_ROWS, D_MODEL), F32)
    xb = _dispatch(dest_p, xm_p.reshape(t_p, D_MODEL), xb)
    xb = _dispatch(dest_s, xm_s, xb)
    yb = _experts(block_e, n_used, xb, w_e_gate[0], w_e_up[0], w_e_down[0])
    y_p = _combine(dest_p, yb, h2_p.reshape(t_p, D_MODEL), route_p[2:4, :].T, g_final_r)
    y_smp = _combine(dest_s, yb, h2_s, route_s[2:4, :].T, g_final_r)

    return (y_p.reshape(b, seq, D_MODEL), y_smp.reshape(n_s, 1, D_MODEL),
            p_state.reshape(1, b, SSD_HEADS, SSD_HEAD_DIM, SSD_STATE), p_conv[None],
            _undup(p_wk).reshape(1, b, WINDOW, N_KV, HEAD_DIM), _undup(p_wv).reshape(1, b, WINDOW, N_KV, HEAD_DIM),
            mkv[:, :C_WIDTH].reshape(1, b, MEM_LEN, C_HEADS, C_HEAD_DIM),
            mkv[:, C_WIDTH:].reshape(1, b, MEM_LEN, C_HEADS, C_HEAD_DIM),
            s_state[None], s_conv.reshape(1, n_s, CONV_K - 1, CONV_DIM),
            s_wk.reshape(1, n_s, WINDOW, N_KV, HEAD_DIM), s_wv.reshape(1, n_s, WINDOW, N_KV, HEAD_DIM))
```

```python
import functools
import math

import jax
import jax.numpy as jnp
import numpy as np
from jax import lax
from jax.experimental import pallas as pl
from jax.experimental.pallas import tpu as pltpu

F32 = jnp.float32
BF16 = jnp.bfloat16
HIGHEST = lax.Precision.HIGHEST

D_MODEL = 1024
D_INNER = 2048
SSD_HEAD_DIM = 64
SSD_HEADS = 32
SSD_GROUPS = 4
SSD_HPG = 8
SSD_STATE = 128
CONV_K = 4
CONV_DIM = 3072
CHUNK = 128
HEAD_DIM = 64
N_HEADS = 16
N_KV = 4
GQA = 4
ATTN_WIDTH = 1024
KV_WIDTH = 256
WINDOW = 128
MEM_LEN = 256
C_HEADS = 4
C_HEAD_DIM = 128
C_WIDTH = 512
N_GROUPS = 4
EXP_PER_GROUP = 8
N_EXPERTS = 32
D_EXPERT = 512
EPS = 1e-6
NEG_INF = -1e30
LANES = 128
SUBLANES = 8

KV_DUP = 2 * KV_WIDTH
SEG_Z, SEG_XBC, SEG_Q, SEG_K, SEG_V, SEG_GS, SEG_GA, SEG_DT = (
    0, 2048, 5120, 6144, 6656, 7168, 8192, 9216)
PACKED_COLS = 9472
MASKED_DIST = 1e32
ROUTE_COLS = 128

MOE_ROWS = 256
ALIBI_SLOPES = tuple(2.0 ** (-8.0 * (h + 1) / N_HEADS) for h in range(N_HEADS))


def _tile(n, pref, mult=SUBLANES):
    if n <= pref:
        return n
    for t in range(pref, 0, -1):
        if n % t == 0 and t % mult == 0:
            return t
    return n


def _params(sem, vmem_mb):
    return pltpu.CompilerParams(dimension_semantics=sem, vmem_limit_bytes=vmem_mb * 1024 * 1024)


def _const_spec(shape):
    nd = len(shape)
    return pl.BlockSpec(shape, lambda *_: (0,) * nd, pipeline_mode=pl.Buffered(1))


def _sigmoid(x):
    return 1.0 / (1.0 + jnp.exp(-x))


def _silu(x):
    return x * _sigmoid(x)


def _sigmoid_t(x):
    return 0.5 * jnp.tanh(0.5 * x) + 0.5


def _silu_t(x):
    return x * _sigmoid_t(x)


def _softplus(x):
    return jnp.maximum(x, 0.0) + jnp.log1p(jnp.exp(-jnp.abs(x)))


def _rms(x, g):
    return x * lax.rsqrt(jnp.mean(x * x, axis=-1, keepdims=True) + EPS) * g


def _bf16_round(x):
    return x.astype(BF16).astype(F32)


def _dot(a, b, precision=None):
    return jnp.dot(a, b, preferred_element_type=F32, precision=precision)


def _dot_nt(a, b, precision=None):
    return lax.dot_general(a, b, (((1,), (1,)), ((), ())), preferred_element_type=F32, precision=precision)


HALO = 16


def _inproj_body(tiles_per_seq, x_ref, xh_ref, g_ref, w_ref, convw_ref, convb_ref,
                 zs_ref, act_ref, q_ref, k_ref, v_ref, gs_ref, ga_ref, dt_ref, pconv_ref, cbuf, xfull):
    tm = x_ref.shape[0]
    xb = _rms(x_ref[...], g_ref[...]).astype(BF16)
    step = 512

    def project(ref, off, width, fn):
        for c0 in range(0, width, step):
            cw = min(step, width - c0)
            ref[:, c0:c0 + cw] = fn(_dot(xb, w_ref[:, off + c0:off + c0 + cw])).astype(ref.dtype)

    project(zs_ref, SEG_Z, D_INNER, _silu_t)
    project(q_ref, SEG_Q, ATTN_WIDTH, lambda r: r * HEAD_DIM ** -0.5)
    project(k_ref, SEG_K, KV_DUP, lambda r: r)
    project(v_ref, SEG_V, KV_DUP, lambda r: r)
    project(gs_ref, SEG_GS, D_MODEL, _sigmoid_t)
    project(ga_ref, SEG_GA, D_MODEL, _sigmoid_t)
    project(dt_ref, SEG_DT, LANES, lambda r: r)

    first = pl.program_id(0) % tiles_per_seq == 0
    xh = _rms(xh_ref[...], g_ref[...]) * jnp.where(first, 0.0, 1.0)
    xfull[0:HALO, :] = xh.astype(BF16)
    xfull[HALO:HALO + tm, :] = xb
    for c0 in range(0, CONV_DIM, step):
        cs = slice(c0, c0 + step)
        cbuf[...] = _dot(xfull[...], w_ref[:, SEG_XBC + c0:SEG_XBC + c0 + step])
        acc = convb_ref[:, cs] + convw_ref[3:4, cs] * cbuf[HALO:HALO + tm, :]
        acc = acc + convw_ref[2:3, cs] * cbuf[HALO - 1:HALO - 1 + tm, :]
        acc = acc + convw_ref[1:2, cs] * cbuf[HALO - 2:HALO - 2 + tm, :]
        acc = acc + convw_ref[0:1, cs] * cbuf[HALO - 3:HALO - 3 + tm, :]
        act_ref[:, cs] = _silu_t(acc).astype(act_ref.dtype)
        pconv_ref[:, cs] = cbuf[HALO + tm - SUBLANES:HALO + tm, :]


def _inproj(x2d, g_mix, w_packed, conv_w, conv_b, seq):
    t = x2d.shape[0]
    tm = _tile(seq, 512, HALO)
    tiles_per_seq = seq // tm
    widths = (D_INNER, CONV_DIM, ATTN_WIDTH, KV_DUP, KV_DUP, D_MODEL, D_MODEL)
    out_shape = ([jax.ShapeDtypeStruct((t, w), BF16) for w in widths]
                 + [jax.ShapeDtypeStruct((t, LANES), F32), jax.ShapeDtypeStruct((t // seq, SUBLANES, CONV_DIM), F32)])
    out_specs = ([pl.BlockSpec((tm, w), lambda i: (i, 0)) for w in widths]
                 + [pl.BlockSpec((tm, LANES), lambda i: (i, 0)),
                    pl.BlockSpec((None, SUBLANES, CONV_DIM), lambda i: (i // tiles_per_seq, 0, 0))])
    return pl.pallas_call(
        functools.partial(_inproj_body, tiles_per_seq),
        grid=(t // tm,),
        in_specs=[pl.BlockSpec((tm, D_MODEL), lambda i: (i, 0)),
                  pl.BlockSpec((HALO, D_MODEL), lambda i: (jnp.maximum(i * (tm // HALO) - 1, 0), 0)),
                  _const_spec((1, D_MODEL)),
                  _const_spec((D_MODEL, PACKED_COLS)),
                  _const_spec((CONV_K, CONV_DIM)), _const_spec((1, CONV_DIM))],
        out_specs=out_specs,
        out_shape=out_shape,
        scratch_shapes=[pltpu.VMEM((HALO + tm, 512), F32), pltpu.VMEM((HALO + tm, D_MODEL), BF16)],
        compiler_params=_params(("arbitrary",), 58),
        name="prompt_inproj",
    )(x2d, x2d, g_mix, w_packed, conv_w, conv_b)


def _mixer_body(act, zs_ref, dt_ref, q_ref, k_ref, v_ref, gs_ref, ga_ref, x_ref,
                dtb_ref, alog_ref, dskip_ref, gssd_ref, sinks_ref, wdown_ref, wmix_ref,
                h1_ref, pstate_ref, pk_ref, pv_ref,
                ybuf, att, kprev, vprev, state):
    c = pl.program_id(1)
    last = pl.num_programs(1) - 1
    L = CHUNK

    @pl.when(c == 0)
    def _():
        kprev[...] = jnp.zeros_like(kprev)
        vprev[...] = jnp.zeros_like(vprev)
        state[...] = jnp.zeros_like(state)

    dt = _softplus(dt_ref[...] + dtb_ref[...])
    a_neg = -jnp.exp(alog_ref[...])
    da = dt * a_neg
    ri = lax.broadcasted_iota(jnp.int32, (L, L), 0)
    ci = lax.broadcasted_iota(jnp.int32, (L, L), 1)
    causal = ri >= ci
    tri = jnp.where(causal, 1.0, 0.0).astype(F32)
    acum = _dot(tri, da, precision=HIGHEST)
    acum_t = acum.T
    dt_t = dt.T
    eacum = jnp.exp(acum)
    a_last = acum[L - 1:L, :]
    w_tail = jnp.exp(a_last - acum) * dt
    e_last = jnp.exp(a_last)
    lane = lax.broadcasted_iota(jnp.int32, (L, LANES), 1)
    lo_half = lane < SSD_HEAD_DIM
    lane1 = lax.broadcasted_iota(jnp.int32, (1, LANES), 1)
    lo_half1 = lane1 < SSD_HEAD_DIM

    def ssd_group(g):
        bgb = act[:, D_INNER + g * SSD_STATE:D_INNER + (g + 1) * SSD_STATE]
        cgb = act[:, D_INNER + (SSD_GROUPS + g) * SSD_STATE:D_INNER + (SSD_GROUPS + g + 1) * SSD_STATE]
        cb = _dot_nt(cgb, bgb)
        hg = state[g]
        yoff = _dot(cgb, hg.astype(BF16))
        bgt = bgb.astype(F32).T.astype(BF16)
        heads = tuple(range(g * SSD_HPG, (g + 1) * SSD_HPG))
        segs = [acum[:, h:h + 1] - acum_t[h:h + 1, :] for h in heads]
        decays = [jnp.exp(jnp.where(causal, s_, NEG_INF)) for s_ in segs]
        lmats = [(cb * d_ * dt_t[h:h + 1, :]).astype(BF16) for d_, h in zip(decays, heads)]
        xw, dsc = [], []
        for j in range(SSD_HPG // 2):
            h0, h1 = heads[2 * j], heads[2 * j + 1]
            col = h0 * SSD_HEAD_DIM
            xs_pair = act[:, col:col + LANES]
            y2 = _dot(jnp.concatenate([lmats[2 * j], lmats[2 * j + 1]], axis=0), xs_pair)
            ydiag = jnp.where(lo_half, y2[0:L], y2[L:2 * L])
            esc = jnp.where(lo_half, eacum[:, h0:h0 + 1], eacum[:, h1:h1 + 1])
            ybuf[:, col:col + LANES] = ydiag + yoff[:, 2 * j * SSD_HEAD_DIM:2 * j * SSD_HEAD_DIM + LANES] * esc
            wsc = jnp.where(lo_half, w_tail[:, h0:h0 + 1], w_tail[:, h1:h1 + 1])
            xw.append((xs_pair.astype(F32) * wsc).astype(BF16))
            dsc.append(jnp.where(lo_half1, e_last[:, h0:h0 + 1], e_last[:, h1:h1 + 1]))
        upd = _dot(bgt, jnp.concatenate(xw, axis=1))
        state[g] = hg * jnp.concatenate(dsc, axis=1) + upd
        gw = D_INNER // SSD_GROUPS
        gs_ = slice(g * gw, (g + 1) * gw)
        yg = (ybuf[:, gs_] + dskip_ref[:, gs_] * act[:, gs_].astype(F32)) * zs_ref[:, gs_].astype(F32)
        ybuf[:, gs_] = _rms(yg, gssd_ref[:, gs_])

    nd_c = jnp.where(causal, (ci - ri).astype(F32), -MASKED_DIST)
    nd_p = jnp.where(ci >= ri + jnp.where(c > 0, 0, L), (ci - ri - L).astype(F32), -MASKED_DIST)
    keep = (jnp.where(lo_half1, 1.0, 0.0).astype(BF16), jnp.where(lo_half1, 0.0, 1.0).astype(BF16))
    def attn_group(kv):
        kd_p = kprev[:, kv * LANES:(kv + 1) * LANES]
        kd_c = k_ref[:, kv * LANES:(kv + 1) * LANES]
        vd_p = vprev[:, kv * LANES:(kv + 1) * LANES]
        vd_c = v_ref[:, kv * LANES:(kv + 1) * LANES]
        heads = tuple(range(kv * GQA, (kv + 1) * GQA))
        pcs = [slice((kv * GQA + 2 * j) * HEAD_DIM, (kv * GQA + 2 * j) * HEAD_DIM + LANES) for j in range(GQA // 2)]
        qs = jnp.concatenate([q_ref[:, pcs[h % GQA // 2]] * keep[h % 2] for h in heads], axis=0)
        s_p = _dot_nt(qs, kd_p) + jnp.concatenate([ALIBI_SLOPES[h] * nd_p for h in heads], axis=0)
        s_c = _dot_nt(qs, kd_c) + jnp.concatenate([ALIBI_SLOPES[h] * nd_c for h in heads], axis=0)
        sink = jnp.concatenate([jnp.full((L, 1), sinks_ref[h], F32) for h in heads], axis=0)
        m = jnp.maximum(jnp.max(jnp.maximum(s_p, s_c), axis=-1, keepdims=True), sink)
        p_p = jnp.exp(s_p - m)
        p_c = jnp.exp(s_c - m)
        denom = jnp.sum(p_p + p_c, axis=-1, keepdims=True) + jnp.exp(sink - m)
        o = (_dot(p_p.astype(BF16), vd_p) + _dot(p_c.astype(BF16), vd_c)) / denom
        for j in range(GQA // 2):
            att[:, pcs[j]] = jnp.where(lo_half, o[2 * j * L:(2 * j + 1) * L], o[(2 * j + 1) * L:(2 * j + 2) * L])

    for g in range(SSD_GROUPS):
        ssd_group(g)
        attn_group(g)
    kprev[...] = k_ref[...]
    vprev[...] = v_ref[...]

    y_ssd = _dot(ybuf[...].astype(BF16), wdown_ref[...])
    merged = gs_ref[...].astype(F32) * y_ssd + ga_ref[...].astype(F32) * att[...]
    h1_ref[...] = x_ref[...] + _dot(merged.astype(BF16), wmix_ref[...])

    @pl.when(c == last)
    def _():
        for g in range(SSD_GROUPS):
            pstate_ref[g] = state[g].T
        pk_ref[...] = k_ref[...].astype(F32)
        pv_ref[...] = v_ref[...].astype(F32)


def _mixer(x, zs, act, q, k, v, gs, ga, dt, dtb, alog, dskip_x, g_ssd, sinks, w_down, w_mix):
    b, seq, _ = x.shape
    nc = seq // CHUNK

    def blk(width):
        return pl.BlockSpec((None, CHUNK, width), lambda i, j: (i, j, 0))

    def per_b(*shape):
        nd = len(shape)
        return pl.BlockSpec((None,) + shape, lambda i, j: (i,) + (0,) * nd)

    in_specs = [blk(CONV_DIM), blk(D_INNER), blk(LANES), blk(ATTN_WIDTH), blk(KV_DUP), blk(KV_DUP),
                blk(D_MODEL), blk(D_MODEL), blk(D_MODEL),
                _const_spec((1, LANES)), _const_spec((1, LANES)), _const_spec((1, D_INNER)), _const_spec((1, D_INNER)),
                pl.BlockSpec(memory_space=pltpu.SMEM),
                _const_spec((D_INNER, D_MODEL)), _const_spec((D_MODEL, D_MODEL))]
    out_shape = [jax.ShapeDtypeStruct((b, seq, D_MODEL), F32),
                 jax.ShapeDtypeStruct((b, SSD_GROUPS, SSD_HPG * SSD_HEAD_DIM, SSD_STATE), F32),
                 jax.ShapeDtypeStruct((b, WINDOW, KV_DUP), F32),
                 jax.ShapeDtypeStruct((b, WINDOW, KV_DUP), F32)]
    out_specs = [blk(D_MODEL), per_b(SSD_GROUPS, SSD_HPG * SSD_HEAD_DIM, SSD_STATE),
                 per_b(WINDOW, KV_DUP), per_b(WINDOW, KV_DUP)]
    scratch = [pltpu.VMEM((CHUNK, D_INNER), F32),
               pltpu.VMEM((CHUNK, ATTN_WIDTH), F32),
               pltpu.VMEM((CHUNK, KV_DUP), BF16), pltpu.VMEM((CHUNK, KV_DUP), BF16),
               pltpu.VMEM((SSD_GROUPS, SSD_STATE, SSD_HPG * SSD_HEAD_DIM), F32)]
    return pl.pallas_call(
        _mixer_body,
        grid=(b, nc),
        in_specs=in_specs,
        out_specs=out_specs,
        out_shape=out_shape,
        scratch_shapes=scratch,
        compiler_params=_params(("parallel", "arbitrary"), 48),
        name="prompt_mixer",
    )(act, zs, dt, q, k, v, gs, ga, x, dtb, alog, dskip_x, g_ssd, sinks, w_down, w_mix)


def _memkv_body(m_ref, g_ref, w_ref, o_ref):
    o_ref[...] = _dot(_rms(m_ref[...], g_ref[...]).astype(BF16), w_ref[...])


def _memkv(mem2d, g_mem, w_ckv):
    t = mem2d.shape[0]
    tm = _tile(t, 256)
    return pl.pallas_call(
        _memkv_body,
        grid=(t // tm,),
        in_specs=[pl.BlockSpec((tm, D_MODEL), lambda i: (i, 0)), _const_spec((1, D_MODEL)),
                  _const_spec((D_MODEL, 2 * C_WIDTH))],
        out_specs=pl.BlockSpec((tm, 2 * C_WIDTH), lambda i: (i, 0)),
        out_shape=jax.ShapeDtypeStruct((t, 2 * C_WIDTH), F32),
        compiler_params=_params(("parallel",), 32),
        name="memory_kv",
    )(mem2d, g_mem, w_ckv)


def _route(logits):
    rows = logits.shape[0]
    lane = lax.broadcasted_iota(jnp.int32, (rows, ROUTE_COLS), 1).astype(F32)
    big = 1e9
    is_g = lane < N_GROUPS
    lg = jnp.where(is_g, logits, NEG_INF)
    gmax = jnp.max(lg, axis=-1, keepdims=True)
    grp = jnp.min(jnp.where(lg == gmax, lane, big), axis=-1, keepdims=True)
    p_grp = 1.0 / jnp.sum(jnp.where(is_g, jnp.exp(lg - gmax), 0.0), axis=-1, keepdims=True)
    lo = N_GROUPS + EXP_PER_GROUP * grp
    in_grp = (lane >= lo) & (lane < lo + EXP_PER_GROUP)
    le = jnp.where(in_grp, logits, NEG_INF)
    m1 = jnp.max(le, axis=-1, keepdims=True)
    i1 = jnp.min(jnp.where(le == m1, lane, big), axis=-1, keepdims=True)
    le2 = jnp.where(lane == i1, NEG_INF, le)
    m2 = jnp.max(le2, axis=-1, keepdims=True)
    i2 = jnp.min(jnp.where(le2 == m2, lane, big), axis=-1, keepdims=True)
    t2 = jnp.exp(m2 - m1)
    g1 = p_grp / (1.0 + t2)
    g2 = p_grp * t2 / (1.0 + t2)
    info = jnp.where(lane == 0, i1 - N_GROUPS,
                     jnp.where(lane == 1, i2 - N_GROUPS,
                               jnp.where(lane == 2, g1, jnp.where(lane == 3, g2, 0.0))))
    return info


def _cross_body(h1_ref, mkv_ref, gc_ref, wcq_ref, wco_ref, gm_ref, wr_ref, br_ref,
                h2_ref, xm_ref, route_ref, obuf):
    h1 = h1_ref[...]
    xn = _rms(h1, gc_ref[...]).astype(BF16)
    qc = _dot(xn, wcq_ref[...])
    scale = C_HEAD_DIM ** -0.5
    for h in range(C_HEADS):
        hs = slice(h * C_HEAD_DIM, (h + 1) * C_HEAD_DIM)
        mk = mkv_ref[:, hs].astype(BF16)
        mv = mkv_ref[:, C_WIDTH + h * C_HEAD_DIM:C_WIDTH + (h + 1) * C_HEAD_DIM].astype(BF16)
        s = _dot_nt(qc[:, hs].astype(BF16), mk) * scale
        m = jnp.max(s, axis=-1, keepdims=True)
        p = jnp.exp(s - m)
        obuf[:, hs] = _dot(p.astype(BF16), mv) / jnp.sum(p, axis=-1, keepdims=True)
    h2 = h1 + _dot(obuf[...].astype(BF16), wco_ref[...])
    h2_ref[...] = h2
    xm = _rms(h2, gm_ref[...])
    xm_ref[...] = xm
    logits = _dot(xm.astype(BF16), wr_ref[...]) + br_ref[...]
    route_ref[...] = _route(logits).T[0:SUBLANES, :]


def _cross(h1, mkv, g_cross, w_cq, w_co, g_moe, w_r, b_r):
    b, seq, _ = h1.shape
    tq = _tile(seq, 1024)
    nq = seq // tq
    return pl.pallas_call(
        _cross_body,
        grid=(b, nq),
        in_specs=[pl.BlockSpec((None, tq, D_MODEL), lambda i, j: (i, j, 0)),
                  pl.BlockSpec((None, MEM_LEN, 2 * C_WIDTH), lambda i, j: (i, 0, 0)),
                  _const_spec((1, D_MODEL)), _const_spec((D_MODEL, C_WIDTH)), _const_spec((C_WIDTH, D_MODEL)),
                  _const_spec((1, D_MODEL)), _const_spec((D_MODEL, ROUTE_COLS)), _const_spec((1, ROUTE_COLS))],
        out_specs=[pl.BlockSpec((None, tq, D_MODEL), lambda i, j: (i, j, 0)),
                   pl.BlockSpec((None, tq, D_MODEL), lambda i, j: (i, j, 0)),
                   pl.BlockSpec((SUBLANES, tq), lambda i, j: (0, i * nq + j))],
        out_shape=[jax.ShapeDtypeStruct((b, seq, D_MODEL), F32),
                   jax.ShapeDtypeStruct((b, seq, D_MODEL), F32),
                   jax.ShapeDtypeStruct((SUBLANES, b * seq), F32)],
        scratch_shapes=[pltpu.VMEM((tq, C_WIDTH), F32)],
        compiler_params=_params(("parallel", "parallel"), 48),
        name="prompt_cross_route",
    )(h1, mkv, g_cross, w_cq, w_co, g_moe, w_r, b_r)


RANK_TILE = 512


def _rank_body(route_ref, rank_ref, count_ref, carry):
    i = pl.program_id(0)

    @pl.when(i == 0)
    def _():
        carry[...] = jnp.zeros_like(carry)

    e1 = route_ref[0:1, :]
    e2 = route_ref[1:2, :]
    eid = lax.broadcasted_iota(jnp.int32, (N_EXPERTS, RANK_TILE), 0).astype(F32)
    is1 = e1 == eid
    is2 = e2 == eid
    onehot = jnp.where(is1 | is2, 1.0, 0.0)
    si = lax.broadcasted_iota(jnp.int32, (RANK_TILE, RANK_TILE), 0)
    ti = lax.broadcasted_iota(jnp.int32, (RANK_TILE, RANK_TILE), 1)
    before = jnp.where(si < ti, 1.0, 0.0).astype(BF16)
    prefix = _dot(onehot.astype(BF16), before) + carry[:, 0:1]
    r1 = jnp.sum(jnp.where(is1, prefix, 0.0), axis=0, keepdims=True)
    r2 = jnp.sum(jnp.where(is2, prefix, 0.0), axis=0, keepdims=True)
    row = lax.broadcasted_iota(jnp.int32, (SUBLANES, RANK_TILE), 0)
    rank_ref[...] = jnp.where(row == 0, r1, jnp.where(row == 1, r2, 0.0))
    carry[...] = carry[...] + jnp.sum(onehot, axis=1, keepdims=True)
    count_ref[...] = carry[...]


def _rank(route):
    tp = route.shape[1]
    return pl.pallas_call(
        _rank_body,
        grid=(tp // RANK_TILE,),
        in_specs=[pl.BlockSpec((SUBLANES, RANK_TILE), lambda i: (0, i))],
        out_specs=[pl.BlockSpec((SUBLANES, RANK_TILE), lambda i: (0, i)),
                   pl.BlockSpec((N_EXPERTS, LANES), lambda i: (0, 0))],
        out_shape=[jax.ShapeDtypeStruct((SUBLANES, tp), F32), jax.ShapeDtypeStruct((N_EXPERTS, LANES), F32)],
        scratch_shapes=[pltpu.VMEM((N_EXPERTS, LANES), F32)],
        compiler_params=_params(("arbitrary",), 32),
        name="moe_rank",
    )(route)


def _dest_body(route_ref, rank_ref, offs_ref, dest_ref):
    e1 = route_ref[0:1, :]
    e2 = route_ref[1:2, :]
    eid = lax.broadcasted_iota(jnp.int32, (N_EXPERTS, RANK_TILE), 0).astype(F32)
    offs = offs_ref[:, 0:1]
    d1 = jnp.sum(jnp.where(e1 == eid, offs, 0.0), axis=0, keepdims=True) + rank_ref[0:1, :]
    d2 = jnp.sum(jnp.where(e2 == eid, offs, 0.0), axis=0, keepdims=True) + rank_ref[1:2, :]
    row = lax.broadcasted_iota(jnp.int32, (SUBLANES, RANK_TILE), 0)
    dest_ref[...] = jnp.where(row == 0, d1, jnp.where(row == 1, d2, 0.0)).astype(jnp.int32)


def _dest(route, rank, offs):
    tp = route.shape[1]
    return pl.pallas_call(
        _dest_body,
        grid=(tp // RANK_TILE,),
        in_specs=[pl.BlockSpec((SUBLANES, RANK_TILE), lambda i: (0, i)),
                  pl.BlockSpec((SUBLANES, RANK_TILE), lambda i: (0, i)),
                  pl.BlockSpec((N_EXPERTS, LANES), lambda i: (0, 0))],
        out_specs=pl.BlockSpec((SUBLANES, RANK_TILE), lambda i: (0, i)),
        out_shape=jax.ShapeDtypeStruct((SUBLANES, tp), jnp.int32),
        compiler_params=_params(("parallel",), 32),
        name="moe_dest",
    )(route, rank, offs)


DMA_UNROLL = 8


def _row_copy(src, dst, s_row, d_row, sem):
    return pltpu.make_async_copy(src.at[pl.ds(s_row, 1)], dst.at[pl.ds(d_row, 1)], sem)


def _dispatch_body(td, dest_ref, xm_ref, xb_in_ref, xb_ref, sem):
    del xb_in_ref
    base = pl.program_id(0) * td

    def issue(g8, carry):
        t0 = pl.multiple_of(g8 * DMA_UNROLL, DMA_UNROLL)
        for k in range(DMA_UNROLL):
            row = base + t0 + k
            _row_copy(xm_ref, xb_ref, t0 + k, dest_ref[2 * row], sem).start()
            _row_copy(xm_ref, xb_ref, t0 + k, dest_ref[2 * row + 1], sem).start()
        return carry

    lax.fori_loop(0, td // DMA_UNROLL, issue, 0)

    def drain(g8, carry):
        for _ in range(2 * DMA_UNROLL):
            _row_copy(xm_ref, xb_ref, 0, 0, sem).wait()
        return carry

    lax.fori_loop(0, td // DMA_UNROLL, drain, 0)


def _dispatch(dest_flat, xm2d, xb):
    t = xm2d.shape[0]
    td = _tile(t, 256)
    return pl.pallas_call(
        functools.partial(_dispatch_body, td),
        grid_spec=pltpu.PrefetchScalarGridSpec(
            num_scalar_prefetch=1,
            grid=(t // td,),
            in_specs=[pl.BlockSpec((td, D_MODEL), lambda i, d: (i, 0)), pl.BlockSpec(memory_space=pl.ANY)],
            out_specs=pl.BlockSpec(memory_space=pl.ANY),
            scratch_shapes=[pltpu.SemaphoreType.DMA],
        ),
        out_shape=jax.ShapeDtypeStruct(xb.shape, xb.dtype),
        input_output_aliases={2: 0},
        compiler_params=_params(("arbitrary",), 32),
        name="moe_dispatch",
    )(dest_flat, xm2d, xb)


def _expert_body(be_ref, nused_ref, xb_ref, wg_ref, wu_ref, wd_ref, yb_ref, wg_bf, wu_bf, wd_bf):
    i = pl.program_id(0)

    @pl.when(jnp.logical_or(i == 0, be_ref[i] != be_ref[jnp.maximum(i - 1, 0)]))
    def _():
        wg_bf[...] = wg_ref[...].astype(BF16)
        wu_bf[...] = wu_ref[...].astype(BF16)
        wd_bf[...] = wd_ref[...].astype(BF16)

    @pl.when(i < nused_ref[0])
    def _():
        x = xb_ref[...].astype(BF16)
        hmid = _silu(_dot(x, wg_bf[...])) * _dot(x, wu_bf[...])
        yb_ref[...] = _dot(hmid.astype(BF16), wd_bf[...])

    @pl.when(i >= nused_ref[0])
    def _():
        yb_ref[...] = jnp.zeros_like(yb_ref)


def _experts(block_e, n_used, xb, wg, wu, wd):
    rows = xb.shape[0]
    nb = rows // MOE_ROWS

    def xmap(i, be, nu):
        return (jnp.minimum(i, nu[0] - 1), 0)

    def wmap(i, be, nu):
        return (be[i], 0, 0)

    return pl.pallas_call(
        _expert_body,
        grid_spec=pltpu.PrefetchScalarGridSpec(
            num_scalar_prefetch=2,
            grid=(nb,),
            in_specs=[pl.BlockSpec((MOE_ROWS, D_MODEL), xmap),
                      pl.BlockSpec((None, D_MODEL, D_EXPERT), wmap),
                      pl.BlockSpec((None, D_MODEL, D_EXPERT), wmap),
                      pl.BlockSpec((None, D_EXPERT, D_MODEL), wmap)],
            out_specs=pl.BlockSpec((MOE_ROWS, D_MODEL), lambda i, be, nu: (i, 0)),
            scratch_shapes=[pltpu.VMEM((D_MODEL, D_EXPERT), BF16), pltpu.VMEM((D_MODEL, D_EXPERT), BF16),
                            pltpu.VMEM((D_EXPERT, D_MODEL), BF16)],
        ),
        out_shape=jax.ShapeDtypeStruct((rows, D_MODEL), F32),
        compiler_params=_params(("arbitrary",), 48),
        name="moe_experts",
    )(block_e, n_used, xb, wg, wu, wd)


def _combine_body(tc, dest_ref, yb_ref, h2_ref, gate_ref, gf_ref, y_ref, buf, sem):
    i = pl.program_id(0)

    def issue(step, slot):
        base = step * tc

        def grp(g8, carry):
            t0 = pl.multiple_of(g8 * DMA_UNROLL, DMA_UNROLL)
            for k in range(DMA_UNROLL):
                row = base + t0 + k
                for e in range(2):
                    pltpu.make_async_copy(yb_ref.at[pl.ds(dest_ref[2 * row + e], 1)],
                                          buf.at[slot, e, pl.ds(t0 + k, 1)], sem.at[slot]).start()
            return carry

        lax.fori_loop(0, tc // DMA_UNROLL, grp, 0)

    @pl.when(i == 0)
    def _():
        issue(0, 0)

    @pl.when(i + 1 < pl.num_programs(0))
    def _():
        issue(i + 1, (i + 1) % 2)

    slot = i % 2

    def drain(g8, carry):
        for _ in range(2 * DMA_UNROLL):
            pltpu.make_async_copy(yb_ref.at[pl.ds(0, 1)], buf.at[slot, 0, pl.ds(0, 1)], sem.at[slot]).wait()
        return carry

    lax.fori_loop(0, tc // DMA_UNROLL, drain, 0)
    g = gate_ref[...]
    out = h2_ref[...] + (g[:, 0:1] * buf[slot, 0] + g[:, 1:2] * buf[slot, 1])
    y_ref[...] = _rms(out, gf_ref[...])


def _combine(dest_flat, yb, h2_2d, gates, g_final):
    t = h2_2d.shape[0]
    tc = _tile(t, 256)
    return pl.pallas_call(
        functools.partial(_combine_body, tc),
        grid_spec=pltpu.PrefetchScalarGridSpec(
            num_scalar_prefetch=1,
            grid=(t // tc,),
            in_specs=[pl.BlockSpec(memory_space=pl.ANY),
                      pl.BlockSpec((tc, D_MODEL), lambda i, d: (i, 0)),
                      pl.BlockSpec((tc, 2), lambda i, d: (i, 0)),
                      pl.BlockSpec((1, D_MODEL), lambda i, d: (0, 0))],
            out_specs=pl.BlockSpec((tc, D_MODEL), lambda i, d: (i, 0)),
            scratch_shapes=[pltpu.VMEM((2, 2, tc, D_MODEL), F32), pltpu.SemaphoreType.DMA((2,))],
        ),
        out_shape=jax.ShapeDtypeStruct((t, D_MODEL), F32),
        compiler_params=_params(("arbitrary",), 32),
        name="moe_combine",
    )(dest_flat, yb, h2_2d, gates, g_final)


def _s_inproj_body(x_ref, g_ref, w_ref, o_ref):
    o_ref[...] = _dot(_rms(x_ref[...], g_ref[...]).astype(BF16), w_ref[...])


def _s_inproj(x, g_mix, w_packed_f32):
    n = x.shape[0]
    tn = PACKED_COLS // 2
    return pl.pallas_call(
        _s_inproj_body,
        grid=(PACKED_COLS // tn,),
        in_specs=[pl.BlockSpec((n, D_MODEL), lambda j: (0, 0)), pl.BlockSpec((1, D_MODEL), lambda j: (0, 0)),
                  pl.BlockSpec((D_MODEL, tn), lambda j: (0, j))],
        out_specs=pl.BlockSpec((n, tn), lambda j: (0, j)),
        out_shape=jax.ShapeDtypeStruct((n, PACKED_COLS), F32),
        compiler_params=_params(("parallel",), 48),
        name="sample_inproj",
    )(x, g_mix, w_packed_f32)


def _s_conv_body(proj_ref, cs_ref, convw_ref, convb_ref, dtb_ref, alog_ref, xexp_ref,
                 act_ref, ncs_ref, dtx_ref, e_ref):
    step = 512
    for c0 in range(0, CONV_DIM, step):
        cs = slice(c0, c0 + step)
        s0 = cs_ref[:, c0:c0 + step]
        s1 = cs_ref[:, CONV_DIM + c0:CONV_DIM + c0 + step]
        s2 = cs_ref[:, 2 * CONV_DIM + c0:2 * CONV_DIM + c0 + step]
        xn = proj_ref[:, SEG_XBC + c0:SEG_XBC + c0 + step]
        acc = convb_ref[:, cs] + convw_ref[0:1, cs] * s0
        acc = acc + convw_ref[1:2, cs] * s1
        acc = acc + convw_ref[2:3, cs] * s2
        acc = acc + convw_ref[3:4, cs] * xn
        act_ref[:, cs] = _silu(acc)
        ncs_ref[:, c0:c0 + step] = s1
        ncs_ref[:, CONV_DIM + c0:CONV_DIM + c0 + step] = s2
        ncs_ref[:, 2 * CONV_DIM + c0:2 * CONV_DIM + c0 + step] = xn
    dt = _softplus(proj_ref[:, SEG_DT:SEG_DT + LANES] + dtb_ref[...])
    e_ref[...] = jnp.exp(dt * (-jnp.exp(alog_ref[...])))
    dtx_ref[...] = _dot(dt, xexp_ref[...], precision=HIGHEST) * act_ref[:, 0:D_INNER]


def _s_conv(proj, conv_state2d, conv_w, conv_b, dtb, alog, xexp):
    n = proj.shape[0]
    return pl.pallas_call(
        _s_conv_body,
        out_shape=[jax.ShapeDtypeStruct((n, CONV_DIM), F32), jax.ShapeDtypeStruct((n, 3 * CONV_DIM), F32),
                   jax.ShapeDtypeStruct((n, D_INNER), F32), jax.ShapeDtypeStruct((n, LANES), F32)],
        compiler_params=pltpu.CompilerParams(vmem_limit_bytes=48 * 1024 * 1024),
        name="sample_conv",
    )(proj, conv_state2d, conv_w, conv_b, dtb, alog, xexp)


def _s_state_body(bb, e_ref, st_ref, dtxt_ref, b_ref, c_ref, so_ref, yt_ref):
    base = pl.program_id(0) * bb
    lane = lax.broadcasted_iota(jnp.int32, (SSD_HEAD_DIM, LANES), 1)

    for bl in range(bb):
        for h in range(SSD_HEADS):
            g = h // SSD_HPG
            brow = b_ref[bl, :, g * SSD_STATE:(g + 1) * SSD_STATE]
            xcol = dtxt_ref[bl, :, h:h + 1]
            so_ref[bl, h] = st_ref[bl, h] * e_ref[base + bl, h] + xcol * brow
        yacc = jnp.zeros((SSD_HEAD_DIM, LANES), F32)
        for h in range(SSD_HEADS):
            g = h // SSD_HPG
            crow = c_ref[bl, :, g * SSD_STATE:(g + 1) * SSD_STATE]
            yacc = jnp.where(lane == h, jnp.sum(so_ref[bl, h] * crow, axis=-1, keepdims=True), yacc)
        yt_ref[bl] = yacc


def _s_state(e, state, dtxt, bmat, cmat):
    n = state.shape[0]
    bb = _tile(n, 2, 1)
    sblk = (bb, SSD_HEADS, SSD_HEAD_DIM, SSD_STATE)
    bmat = bmat.reshape(n, 1, SSD_GROUPS * SSD_STATE)
    cmat = cmat.reshape(n, 1, SSD_GROUPS * SSD_STATE)
    return pl.pallas_call(
        functools.partial(_s_state_body, bb),
        grid=(n // bb,),
        in_specs=[pl.BlockSpec(memory_space=pltpu.SMEM),
                  pl.BlockSpec(sblk, lambda i: (i, 0, 0, 0)),
                  pl.BlockSpec((bb, SSD_HEAD_DIM, SSD_HEADS), lambda i: (i, 0, 0)),
                  pl.BlockSpec((bb, 1, SSD_GROUPS * SSD_STATE), lambda i: (i, 0, 0)),
                  pl.BlockSpec((bb, 1, SSD_GROUPS * SSD_STATE), lambda i: (i, 0, 0))],
        out_specs=[pl.BlockSpec(sblk, lambda i: (i, 0, 0, 0)),
                   pl.BlockSpec((bb, SSD_HEAD_DIM, LANES), lambda i: (i, 0, 0))],
        out_shape=[jax.ShapeDtypeStruct(state.shape, F32), jax.ShapeDtypeStruct((n, SSD_HEAD_DIM, LANES), F32)],
        compiler_params=_params(("parallel",), 40),
        name="sample_ssd_state",
    )(e, state, dtxt, bmat, cmat)


def _s_attn_body(bb, qexp_ref, ck_ref, cv_ref, kn_ref, vn_ref, sink_ref, bias_ref,
                 y_ref, ok_ref, ov_ref):
    W = WINDOW
    bias = bias_ref[...]
    sink = sink_ref[...]
    for bl in range(bb):
        kn = kn_ref[bl]
        vn = vn_ref[bl]
        qe = qexp_ref[bl].astype(BF16)
        s = _dot_nt(qe, ck_ref[bl].astype(BF16)) * (HEAD_DIM ** -0.5) - bias
        sn = jnp.sum(qe.astype(F32) * _bf16_round(kn), axis=-1, keepdims=True) * (HEAD_DIM ** -0.5)
        m = jnp.maximum(jnp.maximum(jnp.max(s, axis=-1, keepdims=True), sn), sink)
        p = jnp.exp(s - m)
        pn = jnp.exp(sn - m)
        denom = jnp.sum(p, axis=-1, keepdims=True) + pn + jnp.exp(sink - m)
        o = _dot((p / denom).astype(BF16), cv_ref[bl].astype(BF16)) + _bf16_round(pn / denom) * _bf16_round(vn)
        for h in range(N_HEADS):
            kv = h // GQA
            y_ref[bl, :, h * HEAD_DIM:(h + 1) * HEAD_DIM] = o[h:h + 1, kv * HEAD_DIM:(kv + 1) * HEAD_DIM]
        ok_ref[bl, 0:W - 1, :] = ck_ref[bl, 1:W, :]
        ok_ref[bl, W - 1:W, :] = kn
        ov_ref[bl, 0:W - 1, :] = cv_ref[bl, 1:W, :]
        ov_ref[bl, W - 1:W, :] = vn


def _s_attn(qexp, ck, cv, kn, vn, sink_col, bias_tile):
    n = ck.shape[0]
    bb = _tile(n, 8, 1)
    cblk = pl.BlockSpec((bb, WINDOW, KV_WIDTH), lambda i: (i, 0, 0))
    rblk = pl.BlockSpec((bb, 1, KV_WIDTH), lambda i: (i, 0, 0))
    y, ok, ov = pl.pallas_call(
        functools.partial(_s_attn_body, bb),
        grid=(n // bb,),
        in_specs=[pl.BlockSpec((bb, N_HEADS, KV_WIDTH), lambda i: (i, 0, 0)), cblk, cblk, rblk, rblk,
                  pl.BlockSpec((N_HEADS, 1), lambda i: (0, 0)), pl.BlockSpec((N_HEADS, WINDOW), lambda i: (0, 0))],
        out_specs=[pl.BlockSpec((bb, 1, ATTN_WIDTH), lambda i: (i, 0, 0)), cblk, cblk],
        out_shape=[jax.ShapeDtypeStruct((n, 1, ATTN_WIDTH), F32), jax.ShapeDtypeStruct(ck.shape, F32),
                   jax.ShapeDtypeStruct(cv.shape, F32)],
        compiler_params=_params(("parallel",), 32),
        name="sample_window_attn",
    )(qexp, ck, cv, kn.reshape(n, 1, KV_WIDTH), vn.reshape(n, 1, KV_WIDTH), sink_col, bias_tile)
    return y.reshape(n, ATTN_WIDTH), ok, ov


def _s_post_body(y_ref, act_ref, proj_ref, att_ref, x_ref, dskip_ref, gssd_ref, wdown_ref, wmix_ref,
                 gc_ref, wcq_ref, h1_ref, qc_ref, ybuf):
    gw = D_INNER // SSD_GROUPS
    for g in range(SSD_GROUPS):
        gs_ = slice(g * gw, (g + 1) * gw)
        yg = (y_ref[:, gs_] + dskip_ref[:, gs_] * act_ref[:, gs_]) * _silu(proj_ref[:, SEG_Z + g * gw:SEG_Z + (g + 1) * gw])
        ybuf[:, gs_] = _rms(yg, gssd_ref[:, gs_])
    y_ssd = _dot(ybuf[...].astype(BF16), wdown_ref[...])
    merged = (_sigmoid(proj_ref[:, SEG_GS:SEG_GS + D_MODEL]) * y_ssd
              + _sigmoid(proj_ref[:, SEG_GA:SEG_GA + D_MODEL]) * att_ref[...])
    h1 = x_ref[...] + _dot(merged.astype(BF16), wmix_ref[...])
    h1_ref[...] = h1
    qc_ref[...] = _bf16_round(_dot(_rms(h1, gc_ref[...]).astype(BF16), wcq_ref[...]))


def _s_post(y, act, proj, att, x, dskip_x, g_ssd, w_down, w_mix, g_cross, w_cq):
    n = x.shape[0]
    return pl.pallas_call(
        _s_post_body,
        out_shape=[jax.ShapeDtypeStruct((n, D_MODEL), F32), jax.ShapeDtypeStruct((n, C_WIDTH), F32)],
        scratch_shapes=[pltpu.VMEM((n, D_INNER), F32)],
        compiler_params=pltpu.CompilerParams(vmem_limit_bytes=48 * 1024 * 1024),
        name="sample_post_mixer",
    )(y, act, proj, att, x, dskip_x, g_ssd, w_down, w_mix, g_cross, w_cq)


def _s_cross_body(bb, qc_ref, mk_ref, mv_ref, o_ref):
    lane = lax.broadcasted_iota(jnp.int32, (MEM_LEN, LANES), 1)
    scale = C_HEAD_DIM ** -0.5

    for bl in range(bb):
        q = qc_ref[bl]
        s = jnp.zeros((MEM_LEN, LANES), F32)
        for h in range(C_HEADS):
            hs = slice(h * C_HEAD_DIM, (h + 1) * C_HEAD_DIM)
            kmat = _bf16_round(mk_ref[bl, :, h, :])
            s = jnp.where(lane == h, jnp.sum(kmat * q[:, hs], axis=-1, keepdims=True), s)
        s = s * scale
        m = jnp.max(s, axis=0, keepdims=True)
        p = jnp.exp(s - m)
        p = _bf16_round(p / jnp.sum(p, axis=0, keepdims=True))
        for h in range(C_HEADS):
            hs = slice(h * C_HEAD_DIM, (h + 1) * C_HEAD_DIM)
            vmat = _bf16_round(mv_ref[bl, :, h, :])
            o_ref[bl, :, hs] = jnp.sum(p[:, h:h + 1] * vmat, axis=0, keepdims=True)


def _s_cross(qc, mk, mv):
    n = qc.shape[0]
    bb = _tile(n, 4, 1)
    mblk = pl.BlockSpec((bb, MEM_LEN, C_HEADS, C_HEAD_DIM), lambda i: (i, 0, 0, 0))
    rblk = pl.BlockSpec((bb, 1, C_WIDTH), lambda i: (i, 0, 0))
    return pl.pallas_call(
        functools.partial(_s_cross_body, bb),
        grid=(n // bb,),
        in_specs=[rblk, mblk, mblk],
        out_specs=rblk,
        out_shape=jax.ShapeDtypeStruct((n, 1, C_WIDTH), F32),
        compiler_params=_params(("parallel",), 40),
        name="sample_cross_attn",
    )(qc.reshape(n, 1, C_WIDTH), mk, mv).reshape(n, C_WIDTH)


def _s_route_body(o_ref, h1_ref, wco_ref, gm_ref, wr_ref, br_ref, h2_ref, xm_ref, route_ref):
    h2 = h1_ref[...] + _dot(o_ref[...].astype(BF16), wco_ref[...])
    h2_ref[...] = h2
    xm = _rms(h2, gm_ref[...])
    xm_ref[...] = xm
    logits = _dot(xm.astype(BF16), wr_ref[...]) + br_ref[...]
    route_ref[...] = _route(logits).T[0:SUBLANES, :]


def _s_route(o, h1, w_co, g_moe, w_r, b_r):
    n = o.shape[0]
    return pl.pallas_call(
        _s_route_body,
        out_shape=[jax.ShapeDtypeStruct((n, D_MODEL), F32), jax.ShapeDtypeStruct((n, D_MODEL), F32),
                   jax.ShapeDtypeStruct((SUBLANES, n), F32)],
        compiler_params=pltpu.CompilerParams(vmem_limit_bytes=32 * 1024 * 1024),
        name="sample_cross_out_route",
    )(o, h1, w_co, g_moe, w_r, b_r)


def _pack_in_weights(w_in):
    cuts = np.cumsum((D_INNER, CONV_DIM, SSD_HEADS, ATTN_WIDTH, KV_WIDTH, KV_WIDTH, D_MODEL, D_MODEL))[:-1]
    z, xbc, dt, q, k, v, gs, ga = jnp.split(w_in, [int(c) for c in cuts], axis=1)
    dt = jnp.pad(dt, ((0, 0), (0, PACKED_COLS - SEG_DT - SSD_HEADS)))

    def dup(w):
        w = w.reshape(w.shape[0], N_KV, 1, HEAD_DIM)
        return jnp.broadcast_to(w, (w.shape[0], N_KV, 2, HEAD_DIM)).reshape(w.shape[0], KV_DUP)

    return jnp.concatenate([z, xbc, q, dup(k), dup(v), gs, ga, dt], axis=1)


def _undup(x):
    lead = x.shape[:-1]
    return x.reshape(lead + (N_KV, 2, HEAD_DIM))[..., 0, :].reshape(lead + (KV_WIDTH,))


def _head_expand_matrix():
    m = np.zeros((LANES, D_INNER), np.float32)
    for h in range(SSD_HEADS):
        m[h, h * SSD_HEAD_DIM:(h + 1) * SSD_HEAD_DIM] = 1.0
    return m


def _row(v, width=None):
    v = v.reshape(1, -1)
    if width is not None and v.shape[1] < width:
        v = jnp.pad(v, ((0, 0), (0, width - v.shape[1])))
    return v


def kernel(x_prompt, x_sample, state_ssd, state_conv, cache_win_k, cache_win_v, cache_mem_k, cache_mem_v, mem_prompt, g_mix, w_in, conv_w, conv_b, dt_bias, a_log, d_skip, g_ssd, w_ssd_down, attn_sinks, w_mix_out, g_cross, g_mem, w_cq, w_ckv, w_co, g_moe, w_route_group, b_route_group, w_route_expert, b_route_expert, w_e_gate, w_e_up, w_e_down, g_final):
    assert g_mix.shape[0] == 1, "single layer"
    b, seq, _ = x_prompt.shape
    n_s = x_sample.shape[0]
    t_p = b * seq
    assert seq % CHUNK == 0 and x_sample.shape[1] == 1

    w_packed_bf = _pack_in_weights(w_in[0].astype(BF16))
    w_down_bf, w_mix_bf = w_ssd_down[0].astype(BF16), w_mix_out[0].astype(BF16)
    w_cq_bf, w_co_bf = w_cq[0].astype(BF16), w_co[0].astype(BF16)
    g_mix_r, g_cross_r, g_mem_r, g_moe_r, g_final_r = (_row(g_mix[0]), _row(g_cross[0]), _row(g_mem[0]),
                                                        _row(g_moe[0]), _row(g_final))
    conv_b_r = _row(conv_b[0])
    dtb_r = _row(dt_bias[0], LANES)
    alog_r = _row(a_log[0], LANES)
    dskip_x = _row(jnp.repeat(d_skip[0], SSD_HEAD_DIM))
    g_ssd_r = _row(g_ssd[0])
    w_r = jnp.pad(jnp.concatenate([w_route_group[0], w_route_expert[0]], axis=1),
                  ((0, 0), (0, ROUTE_COLS - N_GROUPS - N_EXPERTS))).astype(BF16)
    b_r = _row(jnp.concatenate([b_route_group[0], b_route_expert[0]]), ROUTE_COLS)

    zs, act, q, k, v, gs, ga, dt, p_conv8 = _inproj(x_prompt.reshape(t_p, D_MODEL), g_mix_r, w_packed_bf,
                                                    conv_w[0], conv_b_r, seq)
    p_conv = p_conv8[:, SUBLANES - (CONV_K - 1):, :]
    r3 = lambda a: a.reshape(b, seq, a.shape[-1])
    h1, p_state, p_wk, p_wv = _mixer(
        x_prompt, r3(zs), r3(act), r3(q), r3(k), r3(v), r3(gs), r3(ga), r3(dt),
        dtb_r, alog_r, dskip_x, g_ssd_r, attn_sinks[0], w_down_bf, w_mix_bf)
    mkv = _memkv(mem_prompt.reshape(b * MEM_LEN, D_MODEL), g_mem_r, w_ckv[0].astype(BF16))
    h2_p, xm_p, route_p = _cross(h1, mkv.reshape(b, MEM_LEN, 2 * C_WIDTH), g_cross_r,
                                 w_cq_bf, w_co_bf, g_moe_r, w_r, b_r)

    xs2 = x_sample.reshape(n_s, D_MODEL)
    proj = _s_inproj(xs2, g_mix_r, w_packed_bf)
    xexp = jnp.asarray(_head_expand_matrix())
    act_s, s_conv, dtx, e_s = _s_conv(proj, state_conv[0].reshape(n_s, 3 * CONV_DIM), conv_w[0], conv_b_r,
                                      dtb_r, alog_r, xexp)
    dtxt = dtx.reshape(n_s, SSD_HEADS, SSD_HEAD_DIM).transpose(0, 2, 1)
    s_state, yt = _s_state(e_s, state_ssd[0], dtxt, act_s[:, D_INNER:D_INNER + SSD_GROUPS * SSD_STATE],
                           act_s[:, D_INNER + SSD_GROUPS * SSD_STATE:])
    y_s = yt[:, :, :SSD_HEADS].transpose(0, 2, 1).reshape(n_s, D_INNER)
    bias_tile = (jnp.asarray(ALIBI_SLOPES, F32)[:, None]
                 * (WINDOW - jnp.arange(WINDOW, dtype=jnp.int32)).astype(F32)[None, :])
    q_s = proj[:, SEG_Q:SEG_Q + ATTN_WIDTH].reshape(n_s, N_KV, GQA, 1, HEAD_DIM)
    kv_eye = jnp.eye(N_KV, dtype=F32).reshape(1, N_KV, 1, N_KV, 1)
    qexp = (q_s * kv_eye).reshape(n_s, N_HEADS, KV_WIDTH)
    att_s, s_wk, s_wv = _s_attn(qexp, cache_win_k[0].reshape(n_s, WINDOW, KV_WIDTH),
                                cache_win_v[0].reshape(n_s, WINDOW, KV_WIDTH),
                                _undup(proj[:, SEG_K:SEG_K + KV_DUP]), _undup(proj[:, SEG_V:SEG_V + KV_DUP]),
                                attn_sinks[0].reshape(N_HEADS, 1), bias_tile)
    h1_s, qc_s = _s_post(y_s, act_s, proj, att_s, xs2, dskip_x, g_ssd_r, w_down_bf, w_mix_bf, g_cross_r, w_cq_bf)
    o_s = _s_cross(qc_s, cache_mem_k[0], cache_mem_v[0])
    h2_s, xm_s, route_s = _s_route(o_s, h1_s, w_co_bf, g_moe_r, w_r, b_r)

    t_all = t_p + n_s
    t_pad = -(-t_all // RANK_TILE) * RANK_TILE
    route_all = jnp.concatenate([route_p, route_s, jnp.full((SUBLANES, t_pad - t_all), -1.0, F32)], axis=1)
    rank, counts = _rank(route_all)
    cnt = counts[:, 0].astype(jnp.int32)
    padded = (cnt + MOE_ROWS - 1) // MOE_ROWS * MOE_ROWS
    pad_end = jnp.cumsum(padded)
    offs = (pad_end - padded).astype(F32)
    nb = -(-(2 * t_all) // MOE_ROWS) + N_EXPERTS
    block_start = jnp.arange(nb, dtype=jnp.int32) * MOE_ROWS
    block_e = jnp.minimum(jnp.sum((pad_end[None, :] <= block_start[:, None]).astype(jnp.int32), axis=1),
                          N_EXPERTS - 1)
    n_used = (pad_end[-1] // MOE_ROWS).astype(jnp.int32).reshape(1)
    dest = _dest(route_all, rank, jnp.broadcast_to(offs[:, None], (N_EXPERTS, LANES)))
    dest_p = dest[0:2, :t_p].T.reshape(-1)
    dest_s = dest[0:2, t_p:t_all].T.reshape(-1)
    xb = jnp.zeros((nb * MOE_ROWS, D_MODEL), F32)
    xb = _dispatch(dest_p, xm_p.reshape(t_p, D_MODEL), xb)
    xb = _dispatch(dest_s, xm_s, xb)
    yb = _experts(block_e, n_used, xb, w_e_gate[0], w_e_up[0], w_e_down[0])
    y_p = _combine(dest_p, yb, h2_p.reshape(t_p, D_MODEL), route_p[2:4, :].T, g_final_r)
    y_smp = _combine(dest_s, yb, h2_s, route_s[2:4, :].T, g_final_r)

    return (y_p.reshape(b, seq, D_MODEL), y_smp.reshape(n_s, 1, D_MODEL),
            p_state.reshape(1, b, SSD_HEADS, SSD_HEAD_DIM, SSD_STATE), p_conv[None],
            _undup(p_wk).reshape(1, b, WINDOW, N_KV, HEAD_DIM), _undup(p_wv).reshape(1, b, WINDOW, N_KV, HEAD_DIM),
            mkv[:, :C_WIDTH].reshape(1, b, MEM_LEN, C_HEADS, C_HEAD_DIM),
            mkv[:, C_WIDTH:].reshape(1, b, MEM_LEN, C_HEADS, C_HEAD_DIM),
            s_state[None], s_conv.reshape(1, n_s, CONV_K - 1, CONV_DIM),
            s_wk.reshape(1, n_s, WINDOW, N_KV, HEAD_DIM), s_wv.reshape(1, n_s, WINDOW, N_KV, HEAD_DIM))
```

```python
import functools
import math

import jax
import jax.numpy as jnp
import numpy as np
from jax import lax
from jax.experimental import pallas as pl
from jax.experimental.pallas import tpu as pltpu

F32 = jnp.float32
BF16 = jnp.bfloat16
HIGHEST = lax.Precision.HIGHEST

D_MODEL = 1024
D_INNER = 2048
SSD_HEAD_DIM = 64
SSD_HEADS = 32
SSD_GROUPS = 4
SSD_HPG = 8
SSD_STATE = 128
CONV_K = 4
CONV_DIM = 3072
CHUNK = 128
HEAD_DIM = 64
N_HEADS = 16
N_KV = 4
GQA = 4
ATTN_WIDTH = 1024
KV_WIDTH = 256
WINDOW = 128
MEM_LEN = 256
C_HEADS = 4
C_HEAD_DIM = 128
C_WIDTH = 512
N_GROUPS = 4
EXP_PER_GROUP = 8
N_EXPERTS = 32
D_EXPERT = 512
EPS = 1e-6
NEG_INF = -1e30
LANES = 128
SUBLANES = 8

KV_DUP = 2 * KV_WIDTH
SEG_Z, SEG_XBC, SEG_Q, SEG_K, SEG_V, SEG_GS, SEG_GA, SEG_DT = (
    0, 2048, 5120, 6144, 6656, 7168, 8192, 9216)
PACKED_COLS = 9472
MASKED_DIST = 1e32
ROUTE_COLS = 128

MOE_ROWS = 256
ALIBI_SLOPES = tuple(2.0 ** (-8.0 * (h + 1) / N_HEADS) for h in range(N_HEADS))


def _tile(n, pref, mult=SUBLANES):
    if n <= pref:
        return n
    for t in range(pref, 0, -1):
        if n % t == 0 and t % mult == 0:
            return t
    return n


def _params(sem, vmem_mb):
    return pltpu.CompilerParams(dimension_semantics=sem, vmem_limit_bytes=vmem_mb * 1024 * 1024)


def _const_spec(shape):
    nd = len(shape)
    return pl.BlockSpec(shape, lambda *_: (0,) * nd, pipeline_mode=pl.Buffered(1))


def _sigmoid(x):
    return 1.0 / (1.0 + jnp.exp(-x))


def _silu(x):
    return x * _sigmoid(x)


def _sigmoid_t(x):
    return 0.5 * jnp.tanh(0.5 * x) + 0.5


def _silu_t(x):
    return x * _sigmoid_t(x)


def _softplus(x):
    return jnp.maximum(x, 0.0) + jnp.log1p(jnp.exp(-jnp.abs(x)))


def _rms(x, g):
    return x * lax.rsqrt(jnp.mean(x * x, axis=-1, keepdims=True) + EPS) * g


def _bf16_round(x):
    return x.astype(BF16).astype(F32)


PACK_W = D_MODEL // 2
HI16 = np.uint32(0xFFFF0000)


def _pack_rows(x):
    lo = pltpu.bitcast(_bf16_round(x[:, :PACK_W]), jnp.uint32) >> 16
    hi = pltpu.bitcast(_bf16_round(x[:, PACK_W:]), jnp.uint32) & HI16
    return lo | hi


def _unpack_rows(p):
    lo = pltpu.bitcast(p << 16, F32)
    hi = pltpu.bitcast(p & HI16, F32)
    return jnp.concatenate([lo, hi], axis=1).astype(BF16)


def _dot(a, b, precision=None):
    return jnp.dot(a, b, preferred_element_type=F32, precision=precision)


def _dot_nt(a, b, precision=None):
    return lax.dot_general(a, b, (((1,), (1,)), ((), ())), preferred_element_type=F32, precision=precision)


HALO = 16


def _inproj_body(tiles_per_seq, x_ref, xh_ref, g_ref, w_ref, convw_ref, convb_ref,
                 zs_ref, act_ref, q_ref, k_ref, v_ref, gs_ref, ga_ref, dt_ref, pconv_ref, cbuf, xfull):
    tm = x_ref.shape[0]
    xb = _rms(x_ref[...], g_ref[...]).astype(BF16)
    step = 512

    def plain(ref, off, width, fn):
        tasks = []
        for c0 in range(0, width, step):
            cw = min(step, width - c0)

            def task(c0=c0, cw=cw):
                ref[:, c0:c0 + cw] = fn(_dot(xb, w_ref[:, off + c0:off + c0 + cw])).astype(ref.dtype)

            tasks.append(task)
        return tasks

    light = (plain(zs_ref, SEG_Z, D_INNER, _silu_t)
             + plain(q_ref, SEG_Q, ATTN_WIDTH, lambda r: r * HEAD_DIM ** -0.5)
             + plain(k_ref, SEG_K, KV_DUP, lambda r: r) + plain(v_ref, SEG_V, KV_DUP, lambda r: r)
             + plain(gs_ref, SEG_GS, D_MODEL, _sigmoid_t) + plain(ga_ref, SEG_GA, D_MODEL, _sigmoid_t)
             + plain(dt_ref, SEG_DT, LANES, lambda r: r))

    first = pl.program_id(0) % tiles_per_seq == 0
    xh = _rms(xh_ref[...], g_ref[...]) * jnp.where(first, 0.0, 1.0)
    xfull[0:HALO, :] = xh.astype(BF16)
    xfull[HALO:HALO + tm, :] = xb

    def conv_chunk(n):
        c0 = n * step
        cs = slice(c0, c0 + step)
        cb = cbuf.at[n % 2]
        cb[...] = _dot(xfull[...], w_ref[:, SEG_XBC + c0:SEG_XBC + c0 + step])
        acc = convb_ref[:, cs] + convw_ref[3:4, cs] * cb[HALO:HALO + tm, :]
        acc = acc + convw_ref[2:3, cs] * cb[HALO - 1:HALO - 1 + tm, :]
        acc = acc + convw_ref[1:2, cs] * cb[HALO - 2:HALO - 2 + tm, :]
        acc = acc + convw_ref[0:1, cs] * cb[HALO - 3:HALO - 3 + tm, :]
        act_ref[:, cs] = _silu_t(acc).astype(act_ref.dtype)
        pconv_ref[:, cs] = cb[HALO + tm - SUBLANES:HALO + tm, :]

    n_conv = CONV_DIM // step
    per = len(light) // n_conv
    for n in range(n_conv):
        conv_chunk(n)
        for task in light[n * per:(n + 1) * per]:
            task()
    for task in light[n_conv * per:]:
        task()


def _inproj(x2d, g_mix, w_packed, conv_w, conv_b, seq):
    t = x2d.shape[0]
    tm = _tile(seq, 512, HALO)
    tiles_per_seq = seq // tm
    widths = (D_INNER, CONV_DIM, ATTN_WIDTH, KV_DUP, KV_DUP, D_MODEL, D_MODEL)
    out_shape = ([jax.ShapeDtypeStruct((t, w), BF16) for w in widths]
                 + [jax.ShapeDtypeStruct((t, LANES), F32), jax.ShapeDtypeStruct((t // seq, SUBLANES, CONV_DIM), F32)])
    out_specs = ([pl.BlockSpec((tm, w), lambda i: (i, 0)) for w in widths]
                 + [pl.BlockSpec((tm, LANES), lambda i: (i, 0)),
                    pl.BlockSpec((None, SUBLANES, CONV_DIM), lambda i: (i // tiles_per_seq, 0, 0))])
    return pl.pallas_call(
        functools.partial(_inproj_body, tiles_per_seq),
        grid=(t // tm,),
        in_specs=[pl.BlockSpec((tm, D_MODEL), lambda i: (i, 0)),
                  pl.BlockSpec((HALO, D_MODEL), lambda i: (jnp.maximum(i * (tm // HALO) - 1, 0), 0)),
                  _const_spec((1, D_MODEL)),
                  _const_spec((D_MODEL, PACKED_COLS)),
                  _const_spec((CONV_K, CONV_DIM)), _const_spec((1, CONV_DIM))],
        out_specs=out_specs,
        out_shape=out_shape,
        scratch_shapes=[pltpu.VMEM((2, HALO + tm, 512), F32), pltpu.VMEM((HALO + tm, D_MODEL), BF16)],
        compiler_params=_params(("arbitrary",), 58),
        name="prompt_inproj",
    )(x2d, x2d, g_mix, w_packed, conv_w, conv_b)


def _mixer_body(act, zs_ref, dt_ref, q_ref, k_ref, v_ref, gs_ref, ga_ref, x_ref,
                dtb_ref, alog_ref, dskip_ref, gssd_ref, sinks_ref, wdown_ref, wmix_ref,
                h1_ref, pstate_ref, pk_ref, pv_ref,
                ybuf, att, kprev, vprev, state):
    c = pl.program_id(1)
    last = pl.num_programs(1) - 1
    L = CHUNK

    @pl.when(c == 0)
    def _():
        kprev[...] = jnp.zeros_like(kprev)
        vprev[...] = jnp.zeros_like(vprev)
        state[...] = jnp.zeros_like(state)

    dt = _softplus(dt_ref[...] + dtb_ref[...])
    a_neg = -jnp.exp(alog_ref[...])
    da = dt * a_neg
    ri = lax.broadcasted_iota(jnp.int32, (L, L), 0)
    ci = lax.broadcasted_iota(jnp.int32, (L, L), 1)
    causal = ri >= ci
    tri = jnp.where(causal, 1.0, 0.0).astype(F32)
    acum = _dot(tri, da, precision=HIGHEST)
    acum_t = acum.T
    dt_t = dt.T
    eacum = jnp.exp(acum)
    a_last = acum[L - 1:L, :]
    w_tail = jnp.exp(a_last - acum) * dt
    e_last = jnp.exp(a_last)
    lane = lax.broadcasted_iota(jnp.int32, (L, LANES), 1)
    lo_half = lane < SSD_HEAD_DIM
    lane1 = lax.broadcasted_iota(jnp.int32, (1, LANES), 1)
    lo_half1 = lane1 < SSD_HEAD_DIM

    def ssd_group(g):
        bgb = act[:, D_INNER + g * SSD_STATE:D_INNER + (g + 1) * SSD_STATE]
        cgb = act[:, D_INNER + (SSD_GROUPS + g) * SSD_STATE:D_INNER + (SSD_GROUPS + g + 1) * SSD_STATE]
        cb = _dot_nt(cgb, bgb)
        hg = state[g]
        yoff = _dot(cgb, hg.astype(BF16))
        bgt = bgb.astype(F32).T.astype(BF16)
        heads = tuple(range(g * SSD_HPG, (g + 1) * SSD_HPG))
        segs = [acum[:, h:h + 1] - acum_t[h:h + 1, :] for h in heads]
        decays = [jnp.exp(jnp.where(causal, s_, NEG_INF)) for s_ in segs]
        lmats = [(cb * d_ * dt_t[h:h + 1, :]).astype(BF16) for d_, h in zip(decays, heads)]
        xw, dsc = [], []
        for j in range(SSD_HPG // 2):
            h0, h1 = heads[2 * j], heads[2 * j + 1]
            col = h0 * SSD_HEAD_DIM
            xs_pair = act[:, col:col + LANES]
            y2 = _dot(jnp.concatenate([lmats[2 * j], lmats[2 * j + 1]], axis=0), xs_pair)
            ydiag = jnp.where(lo_half, y2[0:L], y2[L:2 * L])
            esc = jnp.where(lo_half, eacum[:, h0:h0 + 1], eacum[:, h1:h1 + 1])
            ybuf[:, col:col + LANES] = ydiag + yoff[:, 2 * j * SSD_HEAD_DIM:2 * j * SSD_HEAD_DIM + LANES] * esc
            wsc = jnp.where(lo_half, w_tail[:, h0:h0 + 1], w_tail[:, h1:h1 + 1])
            xw.append((xs_pair.astype(F32) * wsc).astype(BF16))
            dsc.append(jnp.where(lo_half1, e_last[:, h0:h0 + 1], e_last[:, h1:h1 + 1]))
        upd = _dot(bgt, jnp.concatenate(xw, axis=1))
        state[g] = hg * jnp.concatenate(dsc, axis=1) + upd
        gw = D_INNER // SSD_GROUPS
        gs_ = slice(g * gw, (g + 1) * gw)
        yg = (ybuf[:, gs_] + dskip_ref[:, gs_] * act[:, gs_].astype(F32)) * zs_ref[:, gs_].astype(F32)
        ybuf[:, gs_] = _rms(yg, gssd_ref[:, gs_])

    nd_c = jnp.where(causal, (ci - ri).astype(F32), -MASKED_DIST)
    nd_p = jnp.where(ci >= ri + jnp.where(c > 0, 0, L), (ci - ri - L).astype(F32), -MASKED_DIST)
    keep = (jnp.where(lo_half1, 1.0, 0.0).astype(BF16), jnp.where(lo_half1, 0.0, 1.0).astype(BF16))
    def attn_group(kv):
        kd_p = kprev[:, kv * LANES:(kv + 1) * LANES]
        kd_c = k_ref[:, kv * LANES:(kv + 1) * LANES]
        vd_p = vprev[:, kv * LANES:(kv + 1) * LANES]
        vd_c = v_ref[:, kv * LANES:(kv + 1) * LANES]
        heads = tuple(range(kv * GQA, (kv + 1) * GQA))
        pcs = [slice((kv * GQA + 2 * j) * HEAD_DIM, (kv * GQA + 2 * j) * HEAD_DIM + LANES) for j in range(GQA // 2)]
        qs = jnp.concatenate([q_ref[:, pcs[h % GQA // 2]] * keep[h % 2] for h in heads], axis=0)
        s_p = _dot_nt(qs, kd_p) + jnp.concatenate([ALIBI_SLOPES[h] * nd_p for h in heads], axis=0)
        s_c = _dot_nt(qs, kd_c) + jnp.concatenate([ALIBI_SLOPES[h] * nd_c for h in heads], axis=0)
        sink = jnp.concatenate([jnp.full((L, 1), sinks_ref[h], F32) for h in heads], axis=0)
        m = jnp.maximum(jnp.max(jnp.maximum(s_p, s_c), axis=-1, keepdims=True), sink)
        p_p = jnp.exp(s_p - m)
        p_c = jnp.exp(s_c - m)
        denom = jnp.sum(p_p + p_c, axis=-1, keepdims=True) + jnp.exp(sink - m)
        o = (_dot(p_p.astype(BF16), vd_p) + _dot(p_c.astype(BF16), vd_c)) / denom
        for j in range(GQA // 2):
            att[:, pcs[j]] = jnp.where(lo_half, o[2 * j * L:(2 * j + 1) * L], o[(2 * j + 1) * L:(2 * j + 2) * L])

    for g in range(SSD_GROUPS):
        ssd_group(g)
        attn_group(g)
    kprev[...] = k_ref[...]
    vprev[...] = v_ref[...]

    y_ssd = _dot(ybuf[...].astype(BF16), wdown_ref[...])
    merged = gs_ref[...].astype(F32) * y_ssd + ga_ref[...].astype(F32) * att[...]
    h1_ref[...] = x_ref[...] + _dot(merged.astype(BF16), wmix_ref[...])

    @pl.when(c == last)
    def _():
        for g in range(SSD_GROUPS):
            pstate_ref[g] = state[g].T
        pk_ref[...] = k_ref[...].astype(F32)
        pv_ref[...] = v_ref[...].astype(F32)


def _mixer(x, zs, act, q, k, v, gs, ga, dt, dtb, alog, dskip_x, g_ssd, sinks, w_down, w_mix):
    b, seq, _ = x.shape
    nc = seq // CHUNK

    def blk(width):
        return pl.BlockSpec((None, CHUNK, width), lambda i, j: (i, j, 0))

    def per_b(*shape):
        nd = len(shape)
        return pl.BlockSpec((None,) + shape, lambda i, j: (i,) + (0,) * nd)

    in_specs = [blk(CONV_DIM), blk(D_INNER), blk(LANES), blk(ATTN_WIDTH), blk(KV_DUP), blk(KV_DUP),
                blk(D_MODEL), blk(D_MODEL), blk(D_MODEL),
                _const_spec((1, LANES)), _const_spec((1, LANES)), _const_spec((1, D_INNER)), _const_spec((1, D_INNER)),
                pl.BlockSpec(memory_space=pltpu.SMEM),
                _const_spec((D_INNER, D_MODEL)), _const_spec((D_MODEL, D_MODEL))]
    out_shape = [jax.ShapeDtypeStruct((b, seq, D_MODEL), F32),
                 jax.ShapeDtypeStruct((b, SSD_GROUPS, SSD_HPG * SSD_HEAD_DIM, SSD_STATE), F32),
                 jax.ShapeDtypeStruct((b, WINDOW, KV_DUP), F32),
                 jax.ShapeDtypeStruct((b, WINDOW, KV_DUP), F32)]
    out_specs = [blk(D_MODEL), per_b(SSD_GROUPS, SSD_HPG * SSD_HEAD_DIM, SSD_STATE),
                 per_b(WINDOW, KV_DUP), per_b(WINDOW, KV_DUP)]
    scratch = [pltpu.VMEM((CHUNK, D_INNER), F32),
               pltpu.VMEM((CHUNK, ATTN_WIDTH), F32),
               pltpu.VMEM((CHUNK, KV_DUP), BF16), pltpu.VMEM((CHUNK, KV_DUP), BF16),
               pltpu.VMEM((SSD_GROUPS, SSD_STATE, SSD_HPG * SSD_HEAD_DIM), F32)]
    return pl.pallas_call(
        _mixer_body,
        grid=(b, nc),
        in_specs=in_specs,
        out_specs=out_specs,
        out_shape=out_shape,
        scratch_shapes=scratch,
        compiler_params=_params(("parallel", "arbitrary"), 48),
        name="prompt_mixer",
    )(act, zs, dt, q, k, v, gs, ga, x, dtb, alog, dskip_x, g_ssd, sinks, w_down, w_mix)


def _memkv_body(m_ref, g_ref, w_ref, o_ref):
    o_ref[...] = _dot(_rms(m_ref[...], g_ref[...]).astype(BF16), w_ref[...])


def _memkv(mem2d, g_mem, w_ckv):
    t = mem2d.shape[0]
    tm = _tile(t, 256)
    return pl.pallas_call(
        _memkv_body,
        grid=(t // tm,),
        in_specs=[pl.BlockSpec((tm, D_MODEL), lambda i: (i, 0)), _const_spec((1, D_MODEL)),
                  _const_spec((D_MODEL, 2 * C_WIDTH))],
        out_specs=pl.BlockSpec((tm, 2 * C_WIDTH), lambda i: (i, 0)),
        out_shape=jax.ShapeDtypeStruct((t, 2 * C_WIDTH), F32),
        compiler_params=_params(("parallel",), 32),
        name="memory_kv",
    )(mem2d, g_mem, w_ckv)


def _route(logits):
    rows = logits.shape[0]
    lane = lax.broadcasted_iota(jnp.int32, (rows, ROUTE_COLS), 1).astype(F32)
    big = 1e9
    is_g = lane < N_GROUPS
    lg = jnp.where(is_g, logits, NEG_INF)
    gmax = jnp.max(lg, axis=-1, keepdims=True)
    grp = jnp.min(jnp.where(lg == gmax, lane, big), axis=-1, keepdims=True)
    p_grp = 1.0 / jnp.sum(jnp.where(is_g, jnp.exp(lg - gmax), 0.0), axis=-1, keepdims=True)
    lo = N_GROUPS + EXP_PER_GROUP * grp
    in_grp = (lane >= lo) & (lane < lo + EXP_PER_GROUP)
    le = jnp.where(in_grp, logits, NEG_INF)
    m1 = jnp.max(le, axis=-1, keepdims=True)
    i1 = jnp.min(jnp.where(le == m1, lane, big), axis=-1, keepdims=True)
    le2 = jnp.where(lane == i1, NEG_INF, le)
    m2 = jnp.max(le2, axis=-1, keepdims=True)
    i2 = jnp.min(jnp.where(le2 == m2, lane, big), axis=-1, keepdims=True)
    t2 = jnp.exp(m2 - m1)
    g1 = p_grp / (1.0 + t2)
    g2 = p_grp * t2 / (1.0 + t2)
    info = jnp.where(lane == 0, i1 - N_GROUPS,
                     jnp.where(lane == 1, i2 - N_GROUPS,
                               jnp.where(lane == 2, g1, jnp.where(lane == 3, g2, 0.0))))
    return info


def _cross_body(h1_ref, mkv_ref, gc_ref, wcq_ref, wco_ref, gm_ref, wr_ref, br_ref,
                h2_ref, xm_ref, route_ref, obuf):
    h1 = h1_ref[...]
    xn = _rms(h1, gc_ref[...]).astype(BF16)
    qc = _dot(xn, wcq_ref[...])
    scale = C_HEAD_DIM ** -0.5
    for h in range(C_HEADS):
        hs = slice(h * C_HEAD_DIM, (h + 1) * C_HEAD_DIM)
        mk = mkv_ref[:, hs].astype(BF16)
        mv = mkv_ref[:, C_WIDTH + h * C_HEAD_DIM:C_WIDTH + (h + 1) * C_HEAD_DIM].astype(BF16)
        s = _dot_nt(qc[:, hs].astype(BF16), mk) * scale
        m = jnp.max(s, axis=-1, keepdims=True)
        p = jnp.exp(s - m)
        obuf[:, hs] = _dot(p.astype(BF16), mv) / jnp.sum(p, axis=-1, keepdims=True)
    h2 = h1 + _dot(obuf[...].astype(BF16), wco_ref[...])
    h2_ref[...] = h2
    xm = _rms(h2, gm_ref[...])
    xm_ref[...] = _pack_rows(xm)
    logits = _dot(xm.astype(BF16), wr_ref[...]) + br_ref[...]
    route_ref[...] = _route(logits).T[0:SUBLANES, :]


def _cross(h1, mkv, g_cross, w_cq, w_co, g_moe, w_r, b_r):
    b, seq, _ = h1.shape
    tq = _tile(seq, 1024)
    nq = seq // tq
    return pl.pallas_call(
        _cross_body,
        grid=(b, nq),
        in_specs=[pl.BlockSpec((None, tq, D_MODEL), lambda i, j: (i, j, 0)),
                  pl.BlockSpec((None, MEM_LEN, 2 * C_WIDTH), lambda i, j: (i, 0, 0)),
                  _const_spec((1, D_MODEL)), _const_spec((D_MODEL, C_WIDTH)), _const_spec((C_WIDTH, D_MODEL)),
                  _const_spec((1, D_MODEL)), _const_spec((D_MODEL, ROUTE_COLS)), _const_spec((1, ROUTE_COLS))],
        out_specs=[pl.BlockSpec((None, tq, D_MODEL), lambda i, j: (i, j, 0)),
                   pl.BlockSpec((None, tq, PACK_W), lambda i, j: (i, j, 0)),
                   pl.BlockSpec((SUBLANES, tq), lambda i, j: (0, i * nq + j))],
        out_shape=[jax.ShapeDtypeStruct((b, seq, D_MODEL), F32),
                   jax.ShapeDtypeStruct((b, seq, PACK_W), jnp.uint32),
                   jax.ShapeDtypeStruct((SUBLANES, b * seq), F32)],
        scratch_shapes=[pltpu.VMEM((tq, C_WIDTH), F32)],
        compiler_params=_params(("parallel", "parallel"), 48),
        name="prompt_cross_route",
    )(h1, mkv, g_cross, w_cq, w_co, g_moe, w_r, b_r)


RANK_TILE = 512


def _rank_body(route_ref, rank_ref, count_ref, carry):
    i = pl.program_id(0)

    @pl.when(i == 0)
    def _():
        carry[...] = jnp.zeros_like(carry)

    e1 = route_ref[0:1, :]
    e2 = route_ref[1:2, :]
    eid = lax.broadcasted_iota(jnp.int32, (N_EXPERTS, RANK_TILE), 0).astype(F32)
    is1 = e1 == eid
    is2 = e2 == eid
    onehot = jnp.where(is1 | is2, 1.0, 0.0)
    si = lax.broadcasted_iota(jnp.int32, (RANK_TILE, RANK_TILE), 0)
    ti = lax.broadcasted_iota(jnp.int32, (RANK_TILE, RANK_TILE), 1)
    before = jnp.where(si < ti, 1.0, 0.0).astype(BF16)
    prefix = _dot(onehot.astype(BF16), before) + carry[:, 0:1]
    r1 = jnp.sum(jnp.where(is1, prefix, 0.0), axis=0, keepdims=True)
    r2 = jnp.sum(jnp.where(is2, prefix, 0.0), axis=0, keepdims=True)
    row = lax.broadcasted_iota(jnp.int32, (SUBLANES, RANK_TILE), 0)
    rank_ref[...] = jnp.where(row == 0, r1, jnp.where(row == 1, r2, 0.0))
    carry[...] = carry[...] + jnp.sum(onehot, axis=1, keepdims=True)
    count_ref[...] = carry[...]


def _rank(route):
    tp = route.shape[1]
    return pl.pallas_call(
        _rank_body,
        grid=(tp // RANK_TILE,),
        in_specs=[pl.BlockSpec((SUBLANES, RANK_TILE), lambda i: (0, i))],
        out_specs=[pl.BlockSpec((SUBLANES, RANK_TILE), lambda i: (0, i)),
                   pl.BlockSpec((N_EXPERTS, LANES), lambda i: (0, 0))],
        out_shape=[jax.ShapeDtypeStruct((SUBLANES, tp), F32), jax.ShapeDtypeStruct((N_EXPERTS, LANES), F32)],
        scratch_shapes=[pltpu.VMEM((N_EXPERTS, LANES), F32)],
        compiler_params=_params(("arbitrary",), 32),
        name="moe_rank",
    )(route)


def _dest_body(route_ref, rank_ref, offs_ref, dest_ref):
    e1 = route_ref[0:1, :]
    e2 = route_ref[1:2, :]
    eid = lax.broadcasted_iota(jnp.int32, (N_EXPERTS, RANK_TILE), 0).astype(F32)
    offs = offs_ref[:, 0:1]
    d1 = jnp.sum(jnp.where(e1 == eid, offs, 0.0), axis=0, keepdims=True) + rank_ref[0:1, :]
    d2 = jnp.sum(jnp.where(e2 == eid, offs, 0.0), axis=0, keepdims=True) + rank_ref[1:2, :]
    row = lax.broadcasted_iota(jnp.int32, (SUBLANES, RANK_TILE), 0)
    dest_ref[...] = jnp.where(row == 0, d1, jnp.where(row == 1, d2, 0.0)).astype(jnp.int32)


def _dest(route, rank, offs):
    tp = route.shape[1]
    return pl.pallas_call(
        _dest_body,
        grid=(tp // RANK_TILE,),
        in_specs=[pl.BlockSpec((SUBLANES, RANK_TILE), lambda i: (0, i)),
                  pl.BlockSpec((SUBLANES, RANK_TILE), lambda i: (0, i)),
                  pl.BlockSpec((N_EXPERTS, LANES), lambda i: (0, 0))],
        out_specs=pl.BlockSpec((SUBLANES, RANK_TILE), lambda i: (0, i)),
        out_shape=jax.ShapeDtypeStruct((SUBLANES, tp), jnp.int32),
        compiler_params=_params(("parallel",), 32),
        name="moe_dest",
    )(route, rank, offs)


DMA_UNROLL = 8


def _row_copy(src, dst, s_row, d_row, sem):
    return pltpu.make_async_copy(src.at[pl.ds(s_row, 1)], dst.at[pl.ds(d_row, 1)], sem)


def _dispatch_body(td, dest_ref, xm_ref, xb_in_ref, xb_ref, sem):
    del xb_in_ref
    base = pl.program_id(0) * td

    def issue(g8, carry):
        t0 = pl.multiple_of(g8 * DMA_UNROLL, DMA_UNROLL)
        for k in range(DMA_UNROLL):
            row = base + t0 + k
            _row_copy(xm_ref, xb_ref, t0 + k, dest_ref[2 * row], sem).start()
            _row_copy(xm_ref, xb_ref, t0 + k, dest_ref[2 * row + 1], sem).start()
        return carry

    lax.fori_loop(0, td // DMA_UNROLL, issue, 0)

    def drain(g8, carry):
        for _ in range(2 * DMA_UNROLL):
            _row_copy(xm_ref, xb_ref, 0, 0, sem).wait()
        return carry

    lax.fori_loop(0, td // DMA_UNROLL, drain, 0)


def _dispatch(dest_flat, xm2d, xb):
    t = xm2d.shape[0]
    td = _tile(t, 256)
    return pl.pallas_call(
        functools.partial(_dispatch_body, td),
        grid_spec=pltpu.PrefetchScalarGridSpec(
            num_scalar_prefetch=1,
            grid=(t // td,),
            in_specs=[pl.BlockSpec((td, PACK_W), lambda i, d: (i, 0)), pl.BlockSpec(memory_space=pl.ANY)],
            out_specs=pl.BlockSpec(memory_space=pl.ANY),
            scratch_shapes=[pltpu.SemaphoreType.DMA],
        ),
        out_shape=jax.ShapeDtypeStruct(xb.shape, xb.dtype),
        input_output_aliases={2: 0},
        compiler_params=_params(("arbitrary",), 32),
        name="moe_dispatch",
    )(dest_flat, xm2d, xb)


def _expert_body(be_ref, nused_ref, xb_ref, wg_ref, wu_ref, wd_ref, yb_ref, wg_bf, wu_bf, wd_bf):
    i = pl.program_id(0)

    @pl.when(jnp.logical_or(i == 0, be_ref[i] != be_ref[jnp.maximum(i - 1, 0)]))
    def _():
        wg_bf[...] = wg_ref[...].astype(BF16)
        wu_bf[...] = wu_ref[...].astype(BF16)
        wd_bf[...] = wd_ref[...].astype(BF16)

    @pl.when(i < nused_ref[0])
    def _():
        x = _unpack_rows(xb_ref[...])
        hmid = _silu(_dot(x, wg_bf[...])) * _dot(x, wu_bf[...])
        yb_ref[...] = _dot(hmid.astype(BF16), wd_bf[...])

    @pl.when(i >= nused_ref[0])
    def _():
        yb_ref[...] = jnp.zeros_like(yb_ref)


def _experts(block_e, n_used, xb, wg, wu, wd):
    rows = xb.shape[0]
    nb = rows // MOE_ROWS

    def xmap(i, be, nu):
        return (jnp.minimum(i, nu[0] - 1), 0)

    def wmap(i, be, nu):
        return (be[i], 0, 0)

    return pl.pallas_call(
        _expert_body,
        grid_spec=pltpu.PrefetchScalarGridSpec(
            num_scalar_prefetch=2,
            grid=(nb,),
            in_specs=[pl.BlockSpec((MOE_ROWS, PACK_W), xmap),
                      pl.BlockSpec((None, D_MODEL, D_EXPERT), wmap),
                      pl.BlockSpec((None, D_MODEL, D_EXPERT), wmap),
                      pl.BlockSpec((None, D_EXPERT, D_MODEL), wmap)],
            out_specs=pl.BlockSpec((MOE_ROWS, D_MODEL), lambda i, be, nu: (i, 0)),
            scratch_shapes=[pltpu.VMEM((D_MODEL, D_EXPERT), BF16), pltpu.VMEM((D_MODEL, D_EXPERT), BF16),
                            pltpu.VMEM((D_EXPERT, D_MODEL), BF16)],
        ),
        out_shape=jax.ShapeDtypeStruct((rows, D_MODEL), F32),
        compiler_params=_params(("arbitrary",), 48),
        name="moe_experts",
    )(block_e, n_used, xb, wg, wu, wd)


def _combine_body(tc, dest_ref, yb_ref, h2_ref, gate_ref, gf_ref, y_ref, buf, sem):
    i = pl.program_id(0)

    def issue(step, slot):
        base = step * tc

        def grp(g8, carry):
            t0 = pl.multiple_of(g8 * DMA_UNROLL, DMA_UNROLL)
            for k in range(DMA_UNROLL):
                row = base + t0 + k
                for e in range(2):
                    pltpu.make_async_copy(yb_ref.at[pl.ds(dest_ref[2 * row + e], 1)],
                                          buf.at[slot, e, pl.ds(t0 + k, 1)], sem.at[slot]).start()
            return carry

        lax.fori_loop(0, tc // DMA_UNROLL, grp, 0)

    @pl.when(i == 0)
    def _():
        issue(0, 0)

    @pl.when(i + 1 < pl.num_programs(0))
    def _():
        issue(i + 1, (i + 1) % 2)

    slot = i % 2

    def drain(g8, carry):
        for _ in range(2 * DMA_UNROLL):
            pltpu.make_async_copy(yb_ref.at[pl.ds(0, 1)], buf.at[slot, 0, pl.ds(0, 1)], sem.at[slot]).wait()
        return carry

    lax.fori_loop(0, tc // DMA_UNROLL, drain, 0)
    g = gate_ref[...]
    out = h2_ref[...] + (g[:, 0:1] * buf[slot, 0] + g[:, 1:2] * buf[slot, 1])
    y_ref[...] = _rms(out, gf_ref[...])


def _combine(dest_flat, yb, h2_2d, gates, g_final):
    t = h2_2d.shape[0]
    tc = _tile(t, 256)
    return pl.pallas_call(
        functools.partial(_combine_body, tc),
        grid_spec=pltpu.PrefetchScalarGridSpec(
            num_scalar_prefetch=1,
            grid=(t // tc,),
            in_specs=[pl.BlockSpec(memory_space=pl.ANY),
                      pl.BlockSpec((tc, D_MODEL), lambda i, d: (i, 0)),
                      pl.BlockSpec((tc, 2), lambda i, d: (i, 0)),
                      pl.BlockSpec((1, D_MODEL), lambda i, d: (0, 0))],
            out_specs=pl.BlockSpec((tc, D_MODEL), lambda i, d: (i, 0)),
            scratch_shapes=[pltpu.VMEM((2, 2, tc, D_MODEL), F32), pltpu.SemaphoreType.DMA((2,))],
        ),
        out_shape=jax.ShapeDtypeStruct((t, D_MODEL), F32),
        compiler_params=_params(("arbitrary",), 32),
        name="moe_combine",
    )(dest_flat, yb, h2_2d, gates, g_final)


def _s_inproj_body(x_ref, g_ref, w_ref, o_ref):
    o_ref[...] = _dot(_rms(x_ref[...], g_ref[...]).astype(BF16), w_ref[...])


def _s_inproj(x, g_mix, w_packed_f32):
    n = x.shape[0]
    tn = PACKED_COLS // 2
    return pl.pallas_call(
        _s_inproj_body,
        grid=(PACKED_COLS // tn,),
        in_specs=[pl.BlockSpec((n, D_MODEL), lambda j: (0, 0)), pl.BlockSpec((1, D_MODEL), lambda j: (0, 0)),
                  pl.BlockSpec((D_MODEL, tn), lambda j: (0, j))],
        out_specs=pl.BlockSpec((n, tn), lambda j: (0, j)),
        out_shape=jax.ShapeDtypeStruct((n, PACKED_COLS), F32),
        compiler_params=_params(("parallel",), 48),
        name="sample_inproj",
    )(x, g_mix, w_packed_f32)


def _s_conv_body(proj_ref, cs_ref, convw_ref, convb_ref, dtb_ref, alog_ref, xexp_ref,
                 act_ref, ncs_ref, dtx_ref, e_ref):
    step = 512
    for c0 in range(0, CONV_DIM, step):
        cs = slice(c0, c0 + step)
        s0 = cs_ref[:, c0:c0 + step]
        s1 = cs_ref[:, CONV_DIM + c0:CONV_DIM + c0 + step]
        s2 = cs_ref[:, 2 * CONV_DIM + c0:2 * CONV_DIM + c0 + step]
        xn = proj_ref[:, SEG_XBC + c0:SEG_XBC + c0 + step]
        acc = convb_ref[:, cs] + convw_ref[0:1, cs] * s0
        acc = acc + convw_ref[1:2, cs] * s1
        acc = acc + convw_ref[2:3, cs] * s2
        acc = acc + convw_ref[3:4, cs] * xn
        act_ref[:, cs] = _silu(acc)
        ncs_ref[:, c0:c0 + step] = s1
        ncs_ref[:, CONV_DIM + c0:CONV_DIM + c0 + step] = s2
        ncs_ref[:, 2 * CONV_DIM + c0:2 * CONV_DIM + c0 + step] = xn
    dt = _softplus(proj_ref[:, SEG_DT:SEG_DT + LANES] + dtb_ref[...])
    e_ref[...] = jnp.exp(dt * (-jnp.exp(alog_ref[...])))
    dtx_ref[...] = _dot(dt, xexp_ref[...], precision=HIGHEST) * act_ref[:, 0:D_INNER]


def _s_conv(proj, conv_state2d, conv_w, conv_b, dtb, alog, xexp):
    n = proj.shape[0]
    return pl.pallas_call(
        _s_conv_body,
        out_shape=[jax.ShapeDtypeStruct((n, CONV_DIM), F32), jax.ShapeDtypeStruct((n, 3 * CONV_DIM), F32),
                   jax.ShapeDtypeStruct((n, D_INNER), F32), jax.ShapeDtypeStruct((n, LANES), F32)],
        compiler_params=pltpu.CompilerParams(vmem_limit_bytes=48 * 1024 * 1024),
        name="sample_conv",
    )(proj, conv_state2d, conv_w, conv_b, dtb, alog, xexp)


def _s_state_body(bb, e_ref, st_ref, dtxt_ref, b_ref, c_ref, so_ref, yt_ref):
    base = pl.program_id(0) * bb
    lane = lax.broadcasted_iota(jnp.int32, (SSD_HEAD_DIM, LANES), 1)

    for bl in range(bb):
        for h in range(SSD_HEADS):
            g = h // SSD_HPG
            brow = b_ref[bl, :, g * SSD_STATE:(g + 1) * SSD_STATE]
            xcol = dtxt_ref[bl, :, h:h + 1]
            so_ref[bl, h] = st_ref[bl, h] * e_ref[base + bl, h] + xcol * brow
        yacc = jnp.zeros((SSD_HEAD_DIM, LANES), F32)
        for h in range(SSD_HEADS):
            g = h // SSD_HPG
            crow = c_ref[bl, :, g * SSD_STATE:(g + 1) * SSD_STATE]
            yacc = jnp.where(lane == h, jnp.sum(so_ref[bl, h] * crow, axis=-1, keepdims=True), yacc)
        yt_ref[bl] = yacc


def _s_state(e, state, dtxt, bmat, cmat):
    n = state.shape[0]
    bb = _tile(n, 2, 1)
    sblk = (bb, SSD_HEADS, SSD_HEAD_DIM, SSD_STATE)
    bmat = bmat.reshape(n, 1, SSD_GROUPS * SSD_STATE)
    cmat = cmat.reshape(n, 1, SSD_GROUPS * SSD_STATE)
    return pl.pallas_call(
        functools.partial(_s_state_body, bb),
        grid=(n // bb,),
        in_specs=[pl.BlockSpec(memory_space=pltpu.SMEM),
                  pl.BlockSpec(sblk, lambda i: (i, 0, 0, 0)),
                  pl.BlockSpec((bb, SSD_HEAD_DIM, SSD_HEADS), lambda i: (i, 0, 0)),
                  pl.BlockSpec((bb, 1, SSD_GROUPS * SSD_STATE), lambda i: (i, 0, 0)),
                  pl.BlockSpec((bb, 1, SSD_GROUPS * SSD_STATE), lambda i: (i, 0, 0))],
        out_specs=[pl.BlockSpec(sblk, lambda i: (i, 0, 0, 0)),
                   pl.BlockSpec((bb, SSD_HEAD_DIM, LANES), lambda i: (i, 0, 0))],
        out_shape=[jax.ShapeDtypeStruct(state.shape, F32), jax.ShapeDtypeStruct((n, SSD_HEAD_DIM, LANES), F32)],
        compiler_params=_params(("parallel",), 40),
        name="sample_ssd_state",
    )(e, state, dtxt, bmat, cmat)


def _s_attn_body(bb, qexp_ref, ck_ref, cv_ref, kn_ref, vn_ref, sink_ref, bias_ref,
                 y_ref, ok_ref, ov_ref):
    W = WINDOW
    bias = bias_ref[...]
    sink = sink_ref[...]
    for bl in range(bb):
        kn = kn_ref[bl]
        vn = vn_ref[bl]
        qe = qexp_ref[bl].astype(BF16)
        s = _dot_nt(qe, ck_ref[bl].astype(BF16)) * (HEAD_DIM ** -0.5) - bias
        sn = jnp.sum(qe.astype(F32) * _bf16_round(kn), axis=-1, keepdims=True) * (HEAD_DIM ** -0.5)
        m = jnp.maximum(jnp.maximum(jnp.max(s, axis=-1, keepdims=True), sn), sink)
        p = jnp.exp(s - m)
        pn = jnp.exp(sn - m)
        denom = jnp.sum(p, axis=-1, keepdims=True) + pn + jnp.exp(sink - m)
        o = _dot((p / denom).astype(BF16), cv_ref[bl].astype(BF16)) + _bf16_round(pn / denom) * _bf16_round(vn)
        for h in range(N_HEADS):
            kv = h // GQA
            y_ref[bl, :, h * HEAD_DIM:(h + 1) * HEAD_DIM] = o[h:h + 1, kv * HEAD_DIM:(kv + 1) * HEAD_DIM]
        ok_ref[bl, 0:W - 1, :] = ck_ref[bl, 1:W, :]
        ok_ref[bl, W - 1:W, :] = kn
        ov_ref[bl, 0:W - 1, :] = cv_ref[bl, 1:W, :]
        ov_ref[bl, W - 1:W, :] = vn


def _s_attn(qexp, ck, cv, kn, vn, sink_col, bias_tile):
    n = ck.shape[0]
    bb = _tile(n, 8, 1)
    cblk = pl.BlockSpec((bb, WINDOW, KV_WIDTH), lambda i: (i, 0, 0))
    rblk = pl.BlockSpec((bb, 1, KV_WIDTH), lambda i: (i, 0, 0))
    y, ok, ov = pl.pallas_call(
        functools.partial(_s_attn_body, bb),
        grid=(n // bb,),
        in_specs=[pl.BlockSpec((bb, N_HEADS, KV_WIDTH), lambda i: (i, 0, 0)), cblk, cblk, rblk, rblk,
                  pl.BlockSpec((N_HEADS, 1), lambda i: (0, 0)), pl.BlockSpec((N_HEADS, WINDOW), lambda i: (0, 0))],
        out_specs=[pl.BlockSpec((bb, 1, ATTN_WIDTH), lambda i: (i, 0, 0)), cblk, cblk],
        out_shape=[jax.ShapeDtypeStruct((n, 1, ATTN_WIDTH), F32), jax.ShapeDtypeStruct(ck.shape, F32),
                   jax.ShapeDtypeStruct(cv.shape, F32)],
        compiler_params=_params(("parallel",), 32),
        name="sample_window_attn",
    )(qexp, ck, cv, kn.reshape(n, 1, KV_WIDTH), vn.reshape(n, 1, KV_WIDTH), sink_col, bias_tile)
    return y.reshape(n, ATTN_WIDTH), ok, ov


def _s_post_body(y_ref, act_ref, proj_ref, att_ref, x_ref, dskip_ref, gssd_ref, wdown_ref, wmix_ref,
                 gc_ref, wcq_ref, h1_ref, qc_ref, ybuf):
    gw = D_INNER // SSD_GROUPS
    for g in range(SSD_GROUPS):
        gs_ = slice(g * gw, (g + 1) * gw)
        yg = (y_ref[:, gs_] + dskip_ref[:, gs_] * act_ref[:, gs_]) * _silu(proj_ref[:, SEG_Z + g * gw:SEG_Z + (g + 1) * gw])
        ybuf[:, gs_] = _rms(yg, gssd_ref[:, gs_])
    y_ssd = _dot(ybuf[...].astype(BF16), wdown_ref[...])
    merged = (_sigmoid(proj_ref[:, SEG_GS:SEG_GS + D_MODEL]) * y_ssd
              + _sigmoid(proj_ref[:, SEG_GA:SEG_GA + D_MODEL]) * att_ref[...])
    h1 = x_ref[...] + _dot(merged.astype(BF16), wmix_ref[...])
    h1_ref[...] = h1
    qc_ref[...] = _bf16_round(_dot(_rms(h1, gc_ref[...]).astype(BF16), wcq_ref[...]))


def _s_post(y, act, proj, att, x, dskip_x, g_ssd, w_down, w_mix, g_cross, w_cq):
    n = x.shape[0]
    return pl.pallas_call(
        _s_post_body,
        out_shape=[jax.ShapeDtypeStruct((n, D_MODEL), F32), jax.ShapeDtypeStruct((n, C_WIDTH), F32)],
        scratch_shapes=[pltpu.VMEM((n, D_INNER), F32)],
        compiler_params=pltpu.CompilerParams(vmem_limit_bytes=48 * 1024 * 1024),
        name="sample_post_mixer",
    )(y, act, proj, att, x, dskip_x, g_ssd, w_down, w_mix, g_cross, w_cq)


def _s_cross_body(bb, qc_ref, mk_ref, mv_ref, o_ref):
    lane = lax.broadcasted_iota(jnp.int32, (MEM_LEN, LANES), 1)
    scale = C_HEAD_DIM ** -0.5

    for bl in range(bb):
        q = qc_ref[bl]
        s = jnp.zeros((MEM_LEN, LANES), F32)
        for h in range(C_HEADS):
            hs = slice(h * C_HEAD_DIM, (h + 1) * C_HEAD_DIM)
            kmat = _bf16_round(mk_ref[bl, :, h, :])
            s = jnp.where(lane == h, jnp.sum(kmat * q[:, hs], axis=-1, keepdims=True), s)
        s = s * scale
        m = jnp.max(s, axis=0, keepdims=True)
        p = jnp.exp(s - m)
        p = _bf16_round(p / jnp.sum(p, axis=0, keepdims=True))
        for h in range(C_HEADS):
            hs = slice(h * C_HEAD_DIM, (h + 1) * C_HEAD_DIM)
            vmat = _bf16_round(mv_ref[bl, :, h, :])
            o_ref[bl, :, hs] = jnp.sum(p[:, h:h + 1] * vmat, axis=0, keepdims=True)


def _s_cross(qc, mk, mv):
    n = qc.shape[0]
    bb = _tile(n, 4, 1)
    mblk = pl.BlockSpec((bb, MEM_LEN, C_HEADS, C_HEAD_DIM), lambda i: (i, 0, 0, 0))
    rblk = pl.BlockSpec((bb, 1, C_WIDTH), lambda i: (i, 0, 0))
    return pl.pallas_call(
        functools.partial(_s_cross_body, bb),
        grid=(n // bb,),
        in_specs=[rblk, mblk, mblk],
        out_specs=rblk,
        out_shape=jax.ShapeDtypeStruct((n, 1, C_WIDTH), F32),
        compiler_params=_params(("parallel",), 40),
        name="sample_cross_attn",
    )(qc.reshape(n, 1, C_WIDTH), mk, mv).reshape(n, C_WIDTH)


def _s_route_body(o_ref, h1_ref, wco_ref, gm_ref, wr_ref, br_ref, h2_ref, xm_ref, route_ref):
    h2 = h1_ref[...] + _dot(o_ref[...].astype(BF16), wco_ref[...])
    h2_ref[...] = h2
    xm = _rms(h2, gm_ref[...])
    xm_ref[...] = _pack_rows(xm)
    logits = _dot(xm.astype(BF16), wr_ref[...]) + br_ref[...]
    route_ref[...] = _route(logits).T[0:SUBLANES, :]


def _s_route(o, h1, w_co, g_moe, w_r, b_r):
    n = o.shape[0]
    return pl.pallas_call(
        _s_route_body,
        out_shape=[jax.ShapeDtypeStruct((n, D_MODEL), F32), jax.ShapeDtypeStruct((n, PACK_W), jnp.uint32),
                   jax.ShapeDtypeStruct((SUBLANES, n), F32)],
        compiler_params=pltpu.CompilerParams(vmem_limit_bytes=32 * 1024 * 1024),
        name="sample_cross_out_route",
    )(o, h1, w_co, g_moe, w_r, b_r)


def _pack_in_weights(w_in):
    cuts = np.cumsum((D_INNER, CONV_DIM, SSD_HEADS, ATTN_WIDTH, KV_WIDTH, KV_WIDTH, D_MODEL, D_MODEL))[:-1]
    z, xbc, dt, q, k, v, gs, ga = jnp.split(w_in, [int(c) for c in cuts], axis=1)
    dt = jnp.pad(dt, ((0, 0), (0, PACKED_COLS - SEG_DT - SSD_HEADS)))

    def dup(w):
        w = w.reshape(w.shape[0], N_KV, 1, HEAD_DIM)
        return jnp.broadcast_to(w, (w.shape[0], N_KV, 2, HEAD_DIM)).reshape(w.shape[0], KV_DUP)

    return jnp.concatenate([z, xbc, q, dup(k), dup(v), gs, ga, dt], axis=1)


def _undup(x):
    lead = x.shape[:-1]
    return x.reshape(lead + (N_KV, 2, HEAD_DIM))[..., 0, :].reshape(lead + (KV_WIDTH,))


def _head_expand_matrix():
    m = np.zeros((LANES, D_INNER), np.float32)
    for h in range(SSD_HEADS):
        m[h, h * SSD_HEAD_DIM:(h + 1) * SSD_HEAD_DIM] = 1.0
    return m


def _row(v, width=None):
    v = v.reshape(1, -1)
    if width is not None and v.shape[1] < width:
        v = jnp.pad(v, ((0, 0), (0, width - v.shape[1])))
    return v


def kernel(x_prompt, x_sample, state_ssd, state_conv, cache_win_k, cache_win_v, cache_mem_k, cache_mem_v, mem_prompt, g_mix, w_in, conv_w, conv_b, dt_bias, a_log, d_skip, g_ssd, w_ssd_down, attn_sinks, w_mix_out, g_cross, g_mem, w_cq, w_ckv, w_co, g_moe, w_route_group, b_route_group, w_route_expert, b_route_expert, w_e_gate, w_e_up, w_e_down, g_final):
    assert g_mix.shape[0] == 1, "single layer"
    b, seq, _ = x_prompt.shape
    n_s = x_sample.shape[0]
    t_p = b * seq
    assert seq % CHUNK == 0 and x_sample.shape[1] == 1

    w_packed_bf = _pack_in_weights(w_in[0].astype(BF16))
    w_down_bf, w_mix_bf = w_ssd_down[0].astype(BF16), w_mix_out[0].astype(BF16)
    w_cq_bf, w_co_bf = w_cq[0].astype(BF16), w_co[0].astype(BF16)
    g_mix_r, g_cross_r, g_mem_r, g_moe_r, g_final_r = (_row(g_mix[0]), _row(g_cross[0]), _row(g_mem[0]),
                                                        _row(g_moe[0]), _row(g_final))
    conv_b_r = _row(conv_b[0])
    dtb_r = _row(dt_bias[0], LANES)
    alog_r = _row(a_log[0], LANES)
    dskip_x = _row(jnp.repeat(d_skip[0], SSD_HEAD_DIM))
    g_ssd_r = _row(g_ssd[0])
    w_r = jnp.pad(jnp.concatenate([w_route_group[0], w_route_expert[0]], axis=1),
                  ((0, 0), (0, ROUTE_COLS - N_GROUPS - N_EXPERTS))).astype(BF16)
    b_r = _row(jnp.concatenate([b_route_group[0], b_route_expert[0]]), ROUTE_COLS)

    zs, act, q, k, v, gs, ga, dt, p_conv8 = _inproj(x_prompt.reshape(t_p, D_MODEL), g_mix_r, w_packed_bf,
                                                    conv_w[0], conv_b_r, seq)
    p_conv = p_conv8[:, SUBLANES - (CONV_K - 1):, :]
    r3 = lambda a: a.reshape(b, seq, a.shape[-1])
    h1, p_state, p_wk, p_wv = _mixer(
        x_prompt, r3(zs), r3(act), r3(q), r3(k), r3(v), r3(gs), r3(ga), r3(dt),
        dtb_r, alog_r, dskip_x, g_ssd_r, attn_sinks[0], w_down_bf, w_mix_bf)
    mkv = _memkv(mem_prompt.reshape(b * MEM_LEN, D_MODEL), g_mem_r, w_ckv[0].astype(BF16))
    h2_p, xm_p, route_p = _cross(h1, mkv.reshape(b, MEM_LEN, 2 * C_WIDTH), g_cross_r,
                                 w_cq_bf, w_co_bf, g_moe_r, w_r, b_r)

    xs2 = x_sample.reshape(n_s, D_MODEL)
    proj = _s_inproj(xs2, g_mix_r, w_packed_bf)
    xexp = jnp.asarray(_head_expand_matrix())
    act_s, s_conv, dtx, e_s = _s_conv(proj, state_conv[0].reshape(n_s, 3 * CONV_DIM), conv_w[0], conv_b_r,
                                      dtb_r, alog_r, xexp)
    dtxt = dtx.reshape(n_s, SSD_HEADS, SSD_HEAD_DIM).transpose(0, 2, 1)
    s_state, yt = _s_state(e_s, state_ssd[0], dtxt, act_s[:, D_INNER:D_INNER + SSD_GROUPS * SSD_STATE],
                           act_s[:, D_INNER + SSD_GROUPS * SSD_STATE:])
    y_s = yt[:, :, :SSD_HEADS].transpose(0, 2, 1).reshape(n_s, D_INNER)
    bias_tile = (jnp.asarray(ALIBI_SLOPES, F32)[:, None]
                 * (WINDOW - jnp.arange(WINDOW, dtype=jnp.int32)).astype(F32)[None, :])
    q_s = proj[:, SEG_Q:SEG_Q + ATTN_WIDTH].reshape(n_s, N_KV, GQA, 1, HEAD_DIM)
    kv_eye = jnp.eye(N_KV, dtype=F32).reshape(1, N_KV, 1, N_KV, 1)
    qexp = (q_s * kv_eye).reshape(n_s, N_HEADS, KV_WIDTH)
    att_s, s_wk, s_wv = _s_attn(qexp, cache_win_k[0].reshape(n_s, WINDOW, KV_WIDTH),
                                cache_win_v[0].reshape(n_s, WINDOW, KV_WIDTH),
                                _undup(proj[:, SEG_K:SEG_K + KV_DUP]), _undup(proj[:, SEG_V:SEG_V + KV_DUP]),
                                attn_sinks[0].reshape(N_HEADS, 1), bias_tile)
    h1_s, qc_s = _s_post(y_s, act_s, proj, att_s, xs2, dskip_x, g_ssd_r, w_down_bf, w_mix_bf, g_cross_r, w_cq_bf)
    o_s = _s_cross(qc_s, cache_mem_k[0], cache_mem_v[0])
    h2_s, xm_s, route_s = _s_route(o_s, h1_s, w_co_bf, g_moe_r, w_r, b_r)

    t_all = t_p + n_s
    t_pad = -(-t_all // RANK_TILE) * RANK_TILE
    route_all = jnp.concatenate([route_p, route_s, jnp.full((SUBLANES, t_pad - t_all), -1.0, F32)], axis=1)
    rank, counts = _rank(route_all)
    cnt = counts[:, 0].astype(jnp.int32)
    padded = (cnt + MOE_ROWS - 1) // MOE_ROWS * MOE_ROWS
    pad_end = jnp.cumsum(padded)
    offs = (pad_end - padded).astype(F32)
    nb = -(-(2 * t_all) // MOE_ROWS) + N_EXPERTS
    block_start = jnp.arange(nb, dtype=jnp.int32) * MOE_ROWS
    block_e = jnp.minimum(jnp.sum((pad_end[None, :] <= block_start[:, None]).astype(jnp.int32), axis=1),
                          N_EXPERTS - 1)
    n_used = (pad_end[-1] // MOE_ROWS).astype(jnp.int32).reshape(1)
    dest = _dest(route_all, rank, jnp.broadcast_to(offs[:, None], (N_EXPERTS, LANES)))
    dest_p = dest[0:2, :t_p].T.reshape(-1)
    dest_s = dest[0:2, t_p:t_all].T.reshape(-1)
    xb = jnp.zeros((nb * MOE_ROWS, PACK_W), jnp.uint32)
    xb = _dispatch(dest_p, xm_p.reshape(t_p, PACK_W), xb)
    xb = _dispatch(dest_s, xm_s, xb)
    yb = _experts(block_e, n_used, xb, w_e_gate[0], w_e_up[0], w_e_down[0])
    y_p = _combine(dest_p, yb, h2_p.reshape(t_p, D_MODEL), route_p[2:4, :].T, g_final_r)
    y_smp = _combine(dest_s, yb, h2_s, route_s[2:4, :].T, g_final_r)

    return (y_p.reshape(b, seq, D_MODEL), y_smp.reshape(n_s, 1, D_MODEL),
            p_state.reshape(1, b, SSD_HEADS, SSD_HEAD_DIM, SSD_STATE), p_conv[None],
            _undup(p_wk).reshape(1, b, WINDOW, N_KV, HEAD_DIM), _undup(p_wv).reshape(1, b, WINDOW, N_KV, HEAD_DIM),
            mkv[:, :C_WIDTH].reshape(1, b, MEM_LEN, C_HEADS, C_HEAD_DIM),
            mkv[:, C_WIDTH:].reshape(1, b, MEM_LEN, C_HEADS, C_HEAD_DIM),
            s_state[None], s_conv.reshape(1, n_s, CONV_K - 1, CONV_DIM),
            s_wk.reshape(1, n_s, WINDOW, N_KV, HEAD_DIM), s_wv.reshape(1, n_s, WINDOW, N_KV, HEAD_DIM))
```

```python
import functools
import math

import jax
import jax.numpy as jnp
import numpy as np
from jax import lax
from jax.experimental import pallas as pl
from jax.experimental.pallas import tpu as pltpu

F32 = jnp.float32
BF16 = jnp.bfloat16
HIGHEST = lax.Precision.HIGHEST

D_MODEL = 1024
D_INNER = 2048
SSD_HEAD_DIM = 64
SSD_HEADS = 32
SSD_GROUPS = 4
SSD_HPG = 8
SSD_STATE = 128
CONV_K = 4
CONV_DIM = 3072
CHUNK = 128
HEAD_DIM = 64
N_HEADS = 16
N_KV = 4
GQA = 4
ATTN_WIDTH = 1024
KV_WIDTH = 256
WINDOW = 128
MEM_LEN = 256
C_HEADS = 4
C_HEAD_DIM = 128
C_WIDTH = 512
N_GROUPS = 4
EXP_PER_GROUP = 8
N_EXPERTS = 32
D_EXPERT = 512
EPS = 1e-6
NEG_INF = -1e30
LANES = 128
SUBLANES = 8

KV_DUP = 2 * KV_WIDTH
SEG_Z, SEG_XBC, SEG_Q, SEG_K, SEG_V, SEG_GS, SEG_GA, SEG_DT = (
    0, 2048, 5120, 6144, 6656, 7168, 8192, 9216)
PACKED_COLS = 9472
MASKED_DIST = 1e32
ROUTE_COLS = 128

MOE_ROWS = 256
ALIBI_SLOPES = tuple(2.0 ** (-8.0 * (h + 1) / N_HEADS) for h in range(N_HEADS))


def _tile(n, pref, mult=SUBLANES):
    if n <= pref:
        return n
    for t in range(pref, 0, -1):
        if n % t == 0 and t % mult == 0:
            return t
    return n


def _params(sem, vmem_mb):
    return pltpu.CompilerParams(dimension_semantics=sem, vmem_limit_bytes=vmem_mb * 1024 * 1024)


def _const_spec(shape):
    nd = len(shape)
    return pl.BlockSpec(shape, lambda *_: (0,) * nd, pipeline_mode=pl.Buffered(1))


def _sigmoid(x):
    return 1.0 / (1.0 + jnp.exp(-x))


def _silu(x):
    return x * _sigmoid(x)


def _sigmoid_t(x):
    return 0.5 * jnp.tanh(0.5 * x) + 0.5


def _silu_t(x):
    return x * _sigmoid_t(x)


def _softplus(x):
    return jnp.maximum(x, 0.0) + jnp.log1p(jnp.exp(-jnp.abs(x)))


def _rms(x, g):
    return x * lax.rsqrt(jnp.mean(x * x, axis=-1, keepdims=True) + EPS) * g


def _bf16_round(x):
    return x.astype(BF16).astype(F32)


PACK_W = D_MODEL // 2
HI16 = np.uint32(0xFFFF0000)


def _pack_rows(x):
    lo = pltpu.bitcast(_bf16_round(x[:, :PACK_W]), jnp.uint32) >> 16
    hi = pltpu.bitcast(_bf16_round(x[:, PACK_W:]), jnp.uint32) & HI16
    return lo | hi


def _unpack_rows(p):
    lo = pltpu.bitcast(p << 16, F32)
    hi = pltpu.bitcast(p & HI16, F32)
    return jnp.concatenate([lo, hi], axis=1).astype(BF16)


def _dot(a, b, precision=None):
    return jnp.dot(a, b, preferred_element_type=F32, precision=precision)


def _dot_nt(a, b, precision=None):
    return lax.dot_general(a, b, (((1,), (1,)), ((), ())), preferred_element_type=F32, precision=precision)


HALO = 16
CONV_CHUNK = 512


def _inproj_body(tiles_per_seq, x_ref, xh_ref, g_ref, w_ref, convw_ref, convb_ref,
                 zs_ref, act_ref, q_ref, k_ref, v_ref, gs_ref, ga_ref, dt_ref, pconv_ref, cbuf, xfull):
    tm = x_ref.shape[0]
    xb = _rms(x_ref[...], g_ref[...]).astype(BF16)
    step = 512

    def plain(ref, off, width, fn):
        tasks = []
        for c0 in range(0, width, step):
            cw = min(step, width - c0)

            def task(c0=c0, cw=cw):
                ref[:, c0:c0 + cw] = fn(_dot(xb, w_ref[:, off + c0:off + c0 + cw])).astype(ref.dtype)

            tasks.append(task)
        return tasks

    light = (plain(zs_ref, SEG_Z, D_INNER, _silu_t)
             + plain(q_ref, SEG_Q, ATTN_WIDTH, lambda r: r * HEAD_DIM ** -0.5)
             + plain(k_ref, SEG_K, KV_DUP, lambda r: r) + plain(v_ref, SEG_V, KV_DUP, lambda r: r)
             + plain(gs_ref, SEG_GS, D_MODEL, _sigmoid_t) + plain(ga_ref, SEG_GA, D_MODEL, _sigmoid_t)
             + plain(dt_ref, SEG_DT, LANES, lambda r: r))

    first = pl.program_id(0) % tiles_per_seq == 0
    xh = _rms(xh_ref[...], g_ref[...]) * jnp.where(first, 0.0, 1.0)
    xfull[0:HALO, :] = xh.astype(BF16)
    xfull[HALO:HALO + tm, :] = xb

    cstep = cbuf.shape[2]

    def conv_chunk(n):
        c0 = n * cstep
        cs = slice(c0, c0 + cstep)
        cb = cbuf.at[n % 2]
        cb[...] = _dot(xfull[...], w_ref[:, SEG_XBC + c0:SEG_XBC + c0 + cstep])
        acc = convb_ref[:, cs] + convw_ref[3:4, cs] * cb[HALO:HALO + tm, :]
        acc = acc + convw_ref[2:3, cs] * cb[HALO - 1:HALO - 1 + tm, :]
        acc = acc + convw_ref[1:2, cs] * cb[HALO - 2:HALO - 2 + tm, :]
        acc = acc + convw_ref[0:1, cs] * cb[HALO - 3:HALO - 3 + tm, :]
        act_ref[:, cs] = _silu_t(acc).astype(act_ref.dtype)
        pconv_ref[:, cs] = cb[HALO + tm - SUBLANES:HALO + tm, :]

    n_conv = CONV_DIM // cstep
    per = len(light) // n_conv
    for n in range(n_conv):
        conv_chunk(n)
        for task in light[n * per:(n + 1) * per]:
            task()
    for task in light[n_conv * per:]:
        task()


def _inproj(x2d, g_mix, w_packed, conv_w, conv_b, seq):
    t = x2d.shape[0]
    tm = _tile(seq, 512, HALO)
    tiles_per_seq = seq // tm
    widths = (D_INNER, CONV_DIM, ATTN_WIDTH, KV_DUP, KV_DUP, D_MODEL, D_MODEL)
    out_shape = ([jax.ShapeDtypeStruct((t, w), BF16) for w in widths]
                 + [jax.ShapeDtypeStruct((t, LANES), F32), jax.ShapeDtypeStruct((t // seq, SUBLANES, CONV_DIM), F32)])
    out_specs = ([pl.BlockSpec((tm, w), lambda i: (i, 0)) for w in widths]
                 + [pl.BlockSpec((tm, LANES), lambda i: (i, 0)),
                    pl.BlockSpec((None, SUBLANES, CONV_DIM), lambda i: (i // tiles_per_seq, 0, 0))])
    return pl.pallas_call(
        functools.partial(_inproj_body, tiles_per_seq),
        grid=(t // tm,),
        in_specs=[pl.BlockSpec((tm, D_MODEL), lambda i: (i, 0)),
                  pl.BlockSpec((HALO, D_MODEL), lambda i: (jnp.maximum(i * (tm // HALO) - 1, 0), 0)),
                  _const_spec((1, D_MODEL)),
                  _const_spec((D_MODEL, PACKED_COLS)),
                  _const_spec((CONV_K, CONV_DIM)), _const_spec((1, CONV_DIM))],
        out_specs=out_specs,
        out_shape=out_shape,
        scratch_shapes=[pltpu.VMEM((2, HALO + tm, CONV_CHUNK), F32), pltpu.VMEM((HALO + tm, D_MODEL), BF16)],
        compiler_params=_params(("arbitrary",), 58),
        name="prompt_inproj",
    )(x2d, x2d, g_mix, w_packed, conv_w, conv_b)


def _mixer_body(act, zs_ref, dt_ref, q_ref, k_ref, v_ref, gs_ref, ga_ref, x_ref,
                dtb_ref, alog_ref, dskip_ref, gssd_ref, sinks_ref, wdown_ref, wmix_ref,
                h1_ref, pstate_ref, pk_ref, pv_ref,
                ybuf, att, kprev, vprev, state):
    c = pl.program_id(1)
    last = pl.num_programs(1) - 1
    L = CHUNK

    @pl.when(c == 0)
    def _():
        kprev[...] = jnp.zeros_like(kprev)
        vprev[...] = jnp.zeros_like(vprev)
        state[...] = jnp.zeros_like(state)

    dt = _softplus(dt_ref[...] + dtb_ref[...])
    a_neg = -jnp.exp(alog_ref[...])
    da = dt * a_neg
    ri = lax.broadcasted_iota(jnp.int32, (L, L), 0)
    ci = lax.broadcasted_iota(jnp.int32, (L, L), 1)
    causal = ri >= ci
    tri = jnp.where(causal, 1.0, 0.0).astype(BF16)
    da_hi = da.astype(BF16)
    da_mid = (da - da_hi.astype(F32)).astype(BF16)
    da_lo = (da - da_hi.astype(F32) - da_mid.astype(F32)).astype(BF16)
    acum = _dot(tri, da_hi) + (_dot(tri, da_mid) + _dot(tri, da_lo))
    acum_t = acum.T
    dt_t = dt.T
    eacum = jnp.exp(acum)
    a_last = acum[L - 1:L, :]
    w_tail = jnp.exp(a_last - acum) * dt
    e_last = jnp.exp(a_last)
    lane = lax.broadcasted_iota(jnp.int32, (L, LANES), 1)
    lo_half = lane < SSD_HEAD_DIM
    lane1 = lax.broadcasted_iota(jnp.int32, (1, LANES), 1)
    lo_half1 = lane1 < SSD_HEAD_DIM

    def ssd_group(g):
        bgb = act[:, D_INNER + g * SSD_STATE:D_INNER + (g + 1) * SSD_STATE]
        cgb = act[:, D_INNER + (SSD_GROUPS + g) * SSD_STATE:D_INNER + (SSD_GROUPS + g + 1) * SSD_STATE]
        cb = _dot_nt(cgb, bgb)
        hg = state[g]
        yoff = _dot(cgb, hg.astype(BF16))
        bgt = bgb.astype(F32).T.astype(BF16)
        heads = tuple(range(g * SSD_HPG, (g + 1) * SSD_HPG))
        segs = [acum[:, h:h + 1] - acum_t[h:h + 1, :] for h in heads]
        decays = [jnp.exp(jnp.where(causal, s_, NEG_INF)) for s_ in segs]
        lmats = [(cb * d_ * dt_t[h:h + 1, :]).astype(BF16) for d_, h in zip(decays, heads)]
        xw, dsc = [], []
        for j in range(SSD_HPG // 2):
            h0, h1 = heads[2 * j], heads[2 * j + 1]
            col = h0 * SSD_HEAD_DIM
            xs_pair = act[:, col:col + LANES]
            y2 = _dot(jnp.concatenate([lmats[2 * j], lmats[2 * j + 1]], axis=0), xs_pair)
            ydiag = jnp.where(lo_half, y2[0:L], y2[L:2 * L])
            esc = jnp.where(lo_half, eacum[:, h0:h0 + 1], eacum[:, h1:h1 + 1])
            ybuf[:, col:col + LANES] = ydiag + yoff[:, 2 * j * SSD_HEAD_DIM:2 * j * SSD_HEAD_DIM + LANES] * esc
            wsc = jnp.where(lo_half, w_tail[:, h0:h0 + 1], w_tail[:, h1:h1 + 1])
            xw.append((xs_pair.astype(F32) * wsc).astype(BF16))
            dsc.append(jnp.where(lo_half1, e_last[:, h0:h0 + 1], e_last[:, h1:h1 + 1]))
        upd = _dot(bgt, jnp.concatenate(xw, axis=1))
        state[g] = hg * jnp.concatenate(dsc, axis=1) + upd
        gw = D_INNER // SSD_GROUPS
        gs_ = slice(g * gw, (g + 1) * gw)
        yg = (ybuf[:, gs_] + dskip_ref[:, gs_] * act[:, gs_].astype(F32)) * zs_ref[:, gs_].astype(F32)
        ybuf[:, gs_] = _rms(yg, gssd_ref[:, gs_])

    nd_c = jnp.where(causal, (ci - ri).astype(F32), -MASKED_DIST)
    nd_p = jnp.where(ci >= ri + jnp.where(c > 0, 0, L), (ci - ri - L).astype(F32), -MASKED_DIST)
    keep = (jnp.where(lo_half1, 1.0, 0.0).astype(BF16), jnp.where(lo_half1, 0.0, 1.0).astype(BF16))
    def attn_group(kv):
        kd_p = kprev[:, kv * LANES:(kv + 1) * LANES]
        kd_c = k_ref[:, kv * LANES:(kv + 1) * LANES]
        vd_p = vprev[:, kv * LANES:(kv + 1) * LANES]
        vd_c = v_ref[:, kv * LANES:(kv + 1) * LANES]
        heads = tuple(range(kv * GQA, (kv + 1) * GQA))
        pcs = [slice((kv * GQA + 2 * j) * HEAD_DIM, (kv * GQA + 2 * j) * HEAD_DIM + LANES) for j in range(GQA // 2)]
        qs = jnp.concatenate([q_ref[:, pcs[h % GQA // 2]] * keep[h % 2] for h in heads], axis=0)
        s_p = _dot_nt(qs, kd_p) + jnp.concatenate([ALIBI_SLOPES[h] * nd_p for h in heads], axis=0)
        s_c = _dot_nt(qs, kd_c) + jnp.concatenate([ALIBI_SLOPES[h] * nd_c for h in heads], axis=0)
        sink = jnp.concatenate([jnp.full((L, 1), sinks_ref[h], F32) for h in heads], axis=0)
        m = jnp.maximum(jnp.max(jnp.maximum(s_p, s_c), axis=-1, keepdims=True), sink)
        p_p = jnp.exp(s_p - m)
        p_c = jnp.exp(s_c - m)
        denom = jnp.sum(p_p + p_c, axis=-1, keepdims=True) + jnp.exp(sink - m)
        o = (_dot(p_p.astype(BF16), vd_p) + _dot(p_c.astype(BF16), vd_c)) / denom
        for j in range(GQA // 2):
            att[:, pcs[j]] = jnp.where(lo_half, o[2 * j * L:(2 * j + 1) * L], o[(2 * j + 1) * L:(2 * j + 2) * L])

    for g in range(SSD_GROUPS):
        ssd_group(g)
        attn_group(g)
    kprev[...] = k_ref[...]
    vprev[...] = v_ref[...]

    y_ssd = _dot(ybuf[...].astype(BF16), wdown_ref[...])
    merged = gs_ref[...].astype(F32) * y_ssd + ga_ref[...].astype(F32) * att[...]
    h1_ref[...] = x_ref[...] + _dot(merged.astype(BF16), wmix_ref[...])

    @pl.when(c == last)
    def _():
        for g in range(SSD_GROUPS):
            pstate_ref[g] = state[g].T
        pk_ref[...] = k_ref[...].astype(F32)
        pv_ref[...] = v_ref[...].astype(F32)


def _mixer(x, zs, act, q, k, v, gs, ga, dt, dtb, alog, dskip_x, g_ssd, sinks, w_down, w_mix):
    b, seq, _ = x.shape
    nc = seq // CHUNK

    def blk(width):
        return pl.BlockSpec((None, CHUNK, width), lambda i, j: (i, j, 0))

    def per_b(*shape):
        nd = len(shape)
        return pl.BlockSpec((None,) + shape, lambda i, j: (i,) + (0,) * nd)

    in_specs = [blk(CONV_DIM), blk(D_INNER), blk(LANES), blk(ATTN_WIDTH), blk(KV_DUP), blk(KV_DUP),
                blk(D_MODEL), blk(D_MODEL), blk(D_MODEL),
                _const_spec((1, LANES)), _const_spec((1, LANES)), _const_spec((1, D_INNER)), _const_spec((1, D_INNER)),
                pl.BlockSpec(memory_space=pltpu.SMEM),
                _const_spec((D_INNER, D_MODEL)), _const_spec((D_MODEL, D_MODEL))]
    out_shape = [jax.ShapeDtypeStruct((b, seq, D_MODEL), F32),
                 jax.ShapeDtypeStruct((b, SSD_GROUPS, SSD_HPG * SSD_HEAD_DIM, SSD_STATE), F32),
                 jax.ShapeDtypeStruct((b, WINDOW, KV_DUP), F32),
                 jax.ShapeDtypeStruct((b, WINDOW, KV_DUP), F32)]
    out_specs = [blk(D_MODEL), per_b(SSD_GROUPS, SSD_HPG * SSD_HEAD_DIM, SSD_STATE),
                 per_b(WINDOW, KV_DUP), per_b(WINDOW, KV_DUP)]
    scratch = [pltpu.VMEM((CHUNK, D_INNER), F32),
               pltpu.VMEM((CHUNK, ATTN_WIDTH), F32),
               pltpu.VMEM((CHUNK, KV_DUP), BF16), pltpu.VMEM((CHUNK, KV_DUP), BF16),
               pltpu.VMEM((SSD_GROUPS, SSD_STATE, SSD_HPG * SSD_HEAD_DIM), F32)]
    return pl.pallas_call(
        _mixer_body,
        grid=(b, nc),
        in_specs=in_specs,
        out_specs=out_specs,
        out_shape=out_shape,
        scratch_shapes=scratch,
        compiler_params=_params(("parallel", "arbitrary"), 48),
        name="prompt_mixer",
    )(act, zs, dt, q, k, v, gs, ga, x, dtb, alog, dskip_x, g_ssd, sinks, w_down, w_mix)


def _memkv_body(m_ref, g_ref, w_ref, o_ref):
    o_ref[...] = _dot(_rms(m_ref[...], g_ref[...]).astype(BF16), w_ref[...])


def _memkv(mem2d, g_mem, w_ckv):
    t = mem2d.shape[0]
    tm = _tile(t, 256)
    return pl.pallas_call(
        _memkv_body,
        grid=(t // tm,),
        in_specs=[pl.BlockSpec((tm, D_MODEL), lambda i: (i, 0)), _const_spec((1, D_MODEL)),
                  _const_spec((D_MODEL, 2 * C_WIDTH))],
        out_specs=pl.BlockSpec((tm, 2 * C_WIDTH), lambda i: (i, 0)),
        out_shape=jax.ShapeDtypeStruct((t, 2 * C_WIDTH), F32),
        compiler_params=_params(("parallel",), 32),
        name="memory_kv",
    )(mem2d, g_mem, w_ckv)


def _route(logits):
    rows = logits.shape[0]
    lane = lax.broadcasted_iota(jnp.int32, (rows, ROUTE_COLS), 1).astype(F32)
    big = 1e9
    is_g = lane < N_GROUPS
    lg = jnp.where(is_g, logits, NEG_INF)
    gmax = jnp.max(lg, axis=-1, keepdims=True)
    grp = jnp.min(jnp.where(lg == gmax, lane, big), axis=-1, keepdims=True)
    p_grp = 1.0 / jnp.sum(jnp.where(is_g, jnp.exp(lg - gmax), 0.0), axis=-1, keepdims=True)
    lo = N_GROUPS + EXP_PER_GROUP * grp
    in_grp = (lane >= lo) & (lane < lo + EXP_PER_GROUP)
    le = jnp.where(in_grp, logits, NEG_INF)
    m1 = jnp.max(le, axis=-1, keepdims=True)
    i1 = jnp.min(jnp.where(le == m1, lane, big), axis=-1, keepdims=True)
    le2 = jnp.where(lane == i1, NEG_INF, le)
    m2 = jnp.max(le2, axis=-1, keepdims=True)
    i2 = jnp.min(jnp.where(le2 == m2, lane, big), axis=-1, keepdims=True)
    t2 = jnp.exp(m2 - m1)
    g1 = p_grp / (1.0 + t2)
    g2 = p_grp * t2 / (1.0 + t2)
    info = jnp.where(lane == 0, i1 - N_GROUPS,
                     jnp.where(lane == 1, i2 - N_GROUPS,
                               jnp.where(lane == 2, g1, jnp.where(lane == 3, g2, 0.0))))
    return info


def _cross_body(h1_ref, mkv_ref, gc_ref, wcq_ref, wco_ref, gm_ref, wr_ref, br_ref,
                h2_ref, xm_ref, route_ref, obuf):
    h1 = h1_ref[...]
    xn = _rms(h1, gc_ref[...]).astype(BF16)
    qc = _dot(xn, wcq_ref[...])
    scale = C_HEAD_DIM ** -0.5
    for h in range(C_HEADS):
        hs = slice(h * C_HEAD_DIM, (h + 1) * C_HEAD_DIM)
        mk = mkv_ref[:, hs].astype(BF16)
        mv = mkv_ref[:, C_WIDTH + h * C_HEAD_DIM:C_WIDTH + (h + 1) * C_HEAD_DIM].astype(BF16)
        s = _dot_nt(qc[:, hs].astype(BF16), mk) * scale
        m = jnp.max(s, axis=-1, keepdims=True)
        p = jnp.exp(s - m)
        obuf[:, hs] = _dot(p.astype(BF16), mv) / jnp.sum(p, axis=-1, keepdims=True)
    h2 = h1 + _dot(obuf[...].astype(BF16), wco_ref[...])
    h2_ref[...] = h2
    xm = _rms(h2, gm_ref[...])
    xm_ref[...] = _pack_rows(xm)
    logits = _dot(xm.astype(BF16), wr_ref[...]) + br_ref[...]
    route_ref[...] = _route(logits).T[0:SUBLANES, :]


def _cross(h1, mkv, g_cross, w_cq, w_co, g_moe, w_r, b_r):
    b, seq, _ = h1.shape
    tq = _tile(seq, 1024)
    nq = seq // tq
    return pl.pallas_call(
        _cross_body,
        grid=(b, nq),
        in_specs=[pl.BlockSpec((None, tq, D_MODEL), lambda i, j: (i, j, 0)),
                  pl.BlockSpec((None, MEM_LEN, 2 * C_WIDTH), lambda i, j: (i, 0, 0)),
                  _const_spec((1, D_MODEL)), _const_spec((D_MODEL, C_WIDTH)), _const_spec((C_WIDTH, D_MODEL)),
                  _const_spec((1, D_MODEL)), _const_spec((D_MODEL, ROUTE_COLS)), _const_spec((1, ROUTE_COLS))],
        out_specs=[pl.BlockSpec((None, tq, D_MODEL), lambda i, j: (i, j, 0)),
                   pl.BlockSpec((None, tq, PACK_W), lambda i, j: (i, j, 0)),
                   pl.BlockSpec((SUBLANES, tq), lambda i, j: (0, i * nq + j))],
        out_shape=[jax.ShapeDtypeStruct((b, seq, D_MODEL), F32),
                   jax.ShapeDtypeStruct((b, seq, PACK_W), jnp.uint32),
                   jax.ShapeDtypeStruct((SUBLANES, b * seq), F32)],
        scratch_shapes=[pltpu.VMEM((tq, C_WIDTH), F32)],
        compiler_params=_params(("parallel", "parallel"), 48),
        name="prompt_cross_route",
    )(h1, mkv, g_cross, w_cq, w_co, g_moe, w_r, b_r)


RANK_TILE = 512


def _rank_body(route_ref, rank_ref, count_ref, carry):
    i = pl.program_id(0)

    @pl.when(i == 0)
    def _():
        carry[...] = jnp.zeros_like(carry)

    e1 = route_ref[0:1, :]
    e2 = route_ref[1:2, :]
    eid = lax.broadcasted_iota(jnp.int32, (N_EXPERTS, RANK_TILE), 0).astype(F32)
    is1 = e1 == eid
    is2 = e2 == eid
    onehot = jnp.where(is1 | is2, 1.0, 0.0)
    si = lax.broadcasted_iota(jnp.int32, (RANK_TILE, RANK_TILE), 0)
    ti = lax.broadcasted_iota(jnp.int32, (RANK_TILE, RANK_TILE), 1)
    before = jnp.where(si < ti, 1.0, 0.0).astype(BF16)
    prefix = _dot(onehot.astype(BF16), before) + carry[:, 0:1]
    r1 = jnp.sum(jnp.where(is1, prefix, 0.0), axis=0, keepdims=True)
    r2 = jnp.sum(jnp.where(is2, prefix, 0.0), axis=0, keepdims=True)
    row = lax.broadcasted_iota(jnp.int32, (SUBLANES, RANK_TILE), 0)
    rank_ref[...] = jnp.where(row == 0, r1, jnp.where(row == 1, r2, 0.0))
    carry[...] = carry[...] + jnp.sum(onehot, axis=1, keepdims=True)
    count_ref[...] = carry[...]


def _rank(route):
    tp = route.shape[1]
    return pl.pallas_call(
        _rank_body,
        grid=(tp // RANK_TILE,),
        in_specs=[pl.BlockSpec((SUBLANES, RANK_TILE), lambda i: (0, i))],
        out_specs=[pl.BlockSpec((SUBLANES, RANK_TILE), lambda i: (0, i)),
                   pl.BlockSpec((N_EXPERTS, LANES), lambda i: (0, 0))],
        out_shape=[jax.ShapeDtypeStruct((SUBLANES, tp), F32), jax.ShapeDtypeStruct((N_EXPERTS, LANES), F32)],
        scratch_shapes=[pltpu.VMEM((N_EXPERTS, LANES), F32)],
        compiler_params=_params(("arbitrary",), 32),
        name="moe_rank",
    )(route)


def _dest_body(route_ref, rank_ref, offs_ref, dest_ref):
    e1 = route_ref[0:1, :]
    e2 = route_ref[1:2, :]
    eid = lax.broadcasted_iota(jnp.int32, (N_EXPERTS, RANK_TILE), 0).astype(F32)
    offs = offs_ref[:, 0:1]
    d1 = jnp.sum(jnp.where(e1 == eid, offs, 0.0), axis=0, keepdims=True) + rank_ref[0:1, :]
    d2 = jnp.sum(jnp.where(e2 == eid, offs, 0.0), axis=0, keepdims=True) + rank_ref[1:2, :]
    row = lax.broadcasted_iota(jnp.int32, (SUBLANES, RANK_TILE), 0)
    dest_ref[...] = jnp.where(row == 0, d1, jnp.where(row == 1, d2, 0.0)).astype(jnp.int32)


def _dest(route, rank, offs):
    tp = route.shape[1]
    return pl.pallas_call(
        _dest_body,
        grid=(tp // RANK_TILE,),
        in_specs=[pl.BlockSpec((SUBLANES, RANK_TILE), lambda i: (0, i)),
                  pl.BlockSpec((SUBLANES, RANK_TILE), lambda i: (0, i)),
                  pl.BlockSpec((N_EXPERTS, LANES), lambda i: (0, 0))],
        out_specs=pl.BlockSpec((SUBLANES, RANK_TILE), lambda i: (0, i)),
        out_shape=jax.ShapeDtypeStruct((SUBLANES, tp), jnp.int32),
        compiler_params=_params(("parallel",), 32),
        name="moe_dest",
    )(route, rank, offs)


DMA_UNROLL = 8


def _row_copy(src, dst, s_row, d_row, sem):
    return pltpu.make_async_copy(src.at[pl.ds(s_row, 1)], dst.at[pl.ds(d_row, 1)], sem)


def _dispatch_body(td, dest_ref, xm_ref, xb_in_ref, xb_ref, sem):
    del xb_in_ref
    base = pl.program_id(0) * td

    def issue(g8, carry):
        t0 = pl.multiple_of(g8 * DMA_UNROLL, DMA_UNROLL)
        for k in range(DMA_UNROLL):
            row = base + t0 + k
            _row_copy(xm_ref, xb_ref, t0 + k, dest_ref[2 * row], sem).start()
            _row_copy(xm_ref, xb_ref, t0 + k, dest_ref[2 * row + 1], sem).start()
        return carry

    lax.fori_loop(0, td // DMA_UNROLL, issue, 0)

    def drain(g8, carry):
        for _ in range(2 * DMA_UNROLL):
            _row_copy(xm_ref, xb_ref, 0, 0, sem).wait()
        return carry

    lax.fori_loop(0, td // DMA_UNROLL, drain, 0)


def _dispatch(dest_flat, xm2d, xb):
    t = xm2d.shape[0]
    td = _tile(t, 256)
    return pl.pallas_call(
        functools.partial(_dispatch_body, td),
        grid_spec=pltpu.PrefetchScalarGridSpec(
            num_scalar_prefetch=1,
            grid=(t // td,),
            in_specs=[pl.BlockSpec((td, PACK_W), lambda i, d: (i, 0)), pl.BlockSpec(memory_space=pl.ANY)],
            out_specs=pl.BlockSpec(memory_space=pl.ANY),
            scratch_shapes=[pltpu.SemaphoreType.DMA],
        ),
        out_shape=jax.ShapeDtypeStruct(xb.shape, xb.dtype),
        input_output_aliases={2: 0},
        compiler_params=_params(("arbitrary",), 32),
        name="moe_dispatch",
    )(dest_flat, xm2d, xb)


def _expert_body(be_ref, nused_ref, xb_ref, wg_ref, wu_ref, wd_ref, yb_ref, wg_bf, wu_bf, wd_bf):
    i = pl.program_id(0)

    @pl.when(jnp.logical_or(i == 0, be_ref[i] != be_ref[jnp.maximum(i - 1, 0)]))
    def _():
        wg_bf[...] = wg_ref[...].astype(BF16)
        wu_bf[...] = wu_ref[...].astype(BF16)
        wd_bf[...] = wd_ref[...].astype(BF16)

    @pl.when(i < nused_ref[0])
    def _():
        x = _unpack_rows(xb_ref[...])
        hmid = _silu(_dot(x, wg_bf[...])) * _dot(x, wu_bf[...])
        yb_ref[...] = _dot(hmid.astype(BF16), wd_bf[...])

    @pl.when(i >= nused_ref[0])
    def _():
        yb_ref[...] = jnp.zeros_like(yb_ref)


def _experts(block_e, n_used, xb, wg, wu, wd):
    rows = xb.shape[0]
    nb = rows // MOE_ROWS

    def xmap(i, be, nu):
        return (jnp.minimum(i, nu[0] - 1), 0)

    def wmap(i, be, nu):
        return (be[i], 0, 0)

    return pl.pallas_call(
        _expert_body,
        grid_spec=pltpu.PrefetchScalarGridSpec(
            num_scalar_prefetch=2,
            grid=(nb,),
            in_specs=[pl.BlockSpec((MOE_ROWS, PACK_W), xmap),
                      pl.BlockSpec((None, D_MODEL, D_EXPERT), wmap),
                      pl.BlockSpec((None, D_MODEL, D_EXPERT), wmap),
                      pl.BlockSpec((None, D_EXPERT, D_MODEL), wmap)],
            out_specs=pl.BlockSpec((MOE_ROWS, D_MODEL), lambda i, be, nu: (i, 0)),
            scratch_shapes=[pltpu.VMEM((D_MODEL, D_EXPERT), BF16), pltpu.VMEM((D_MODEL, D_EXPERT), BF16),
                            pltpu.VMEM((D_EXPERT, D_MODEL), BF16)],
        ),
        out_shape=jax.ShapeDtypeStruct((rows, D_MODEL), F32),
        compiler_params=_params(("arbitrary",), 48),
        name="moe_experts",
    )(block_e, n_used, xb, wg, wu, wd)


def _combine_body(tc, dest_ref, yb_ref, h2_ref, gate_ref, gf_ref, y_ref, buf, sem):
    i = pl.program_id(0)

    def issue(step, slot):
        base = step * tc

        def grp(g8, carry):
            t0 = pl.multiple_of(g8 * DMA_UNROLL, DMA_UNROLL)
            for k in range(DMA_UNROLL):
                row = base + t0 + k
                for e in range(2):
                    pltpu.make_async_copy(yb_ref.at[pl.ds(dest_ref[2 * row + e], 1)],
                                          buf.at[slot, e, pl.ds(t0 + k, 1)], sem.at[slot]).start()
            return carry

        lax.fori_loop(0, tc // DMA_UNROLL, grp, 0)

    @pl.when(i == 0)
    def _():
        issue(0, 0)

    @pl.when(i + 1 < pl.num_programs(0))
    def _():
        issue(i + 1, (i + 1) % 2)

    slot = i % 2

    def drain(g8, carry):
        for _ in range(2 * DMA_UNROLL):
            pltpu.make_async_copy(yb_ref.at[pl.ds(0, 1)], buf.at[slot, 0, pl.ds(0, 1)], sem.at[slot]).wait()
        return carry

    lax.fori_loop(0, tc // DMA_UNROLL, drain, 0)
    g = gate_ref[...]
    out = h2_ref[...] + (g[:, 0:1] * buf[slot, 0] + g[:, 1:2] * buf[slot, 1])
    y_ref[...] = _rms(out, gf_ref[...])


def _combine(dest_flat, yb, h2_2d, gates, g_final):
    t = h2_2d.shape[0]
    tc = _tile(t, 256)
    return pl.pallas_call(
        functools.partial(_combine_body, tc),
        grid_spec=pltpu.PrefetchScalarGridSpec(
            num_scalar_prefetch=1,
            grid=(t // tc,),
            in_specs=[pl.BlockSpec(memory_space=pl.ANY),
                      pl.BlockSpec((tc, D_MODEL), lambda i, d: (i, 0)),
                      pl.BlockSpec((tc, 2), lambda i, d: (i, 0)),
                      pl.BlockSpec((1, D_MODEL), lambda i, d: (0, 0))],
            out_specs=pl.BlockSpec((tc, D_MODEL), lambda i, d: (i, 0)),
            scratch_shapes=[pltpu.VMEM((2, 2, tc, D_MODEL), F32), pltpu.SemaphoreType.DMA((2,))],
        ),
        out_shape=jax.ShapeDtypeStruct((t, D_MODEL), F32),
        compiler_params=_params(("arbitrary",), 32),
        name="moe_combine",
    )(dest_flat, yb, h2_2d, gates, g_final)


def _s_inproj_body(x_ref, g_ref, w_ref, o_ref):
    o_ref[...] = _dot(_rms(x_ref[...], g_ref[...]).astype(BF16), w_ref[...])


def _s_inproj(x, g_mix, w_packed_f32):
    n = x.shape[0]
    tn = PACKED_COLS // 2
    return pl.pallas_call(
        _s_inproj_body,
        grid=(PACKED_COLS // tn,),
        in_specs=[pl.BlockSpec((n, D_MODEL), lambda j: (0, 0)), pl.BlockSpec((1, D_MODEL), lambda j: (0, 0)),
                  pl.BlockSpec((D_MODEL, tn), lambda j: (0, j))],
        out_specs=pl.BlockSpec((n, tn), lambda j: (0, j)),
        out_shape=jax.ShapeDtypeStruct((n, PACKED_COLS), F32),
        compiler_params=_params(("parallel",), 48),
        name="sample_inproj",
    )(x, g_mix, w_packed_f32)


def _s_conv_body(proj_ref, cs_ref, convw_ref, convb_ref, dtb_ref, alog_ref, xexp_ref,
                 act_ref, ncs_ref, dtx_ref, e_ref):
    step = 512
    for c0 in range(0, CONV_DIM, step):
        cs = slice(c0, c0 + step)
        s0 = cs_ref[:, c0:c0 + step]
        s1 = cs_ref[:, CONV_DIM + c0:CONV_DIM + c0 + step]
        s2 = cs_ref[:, 2 * CONV_DIM + c0:2 * CONV_DIM + c0 + step]
        xn = proj_ref[:, SEG_XBC + c0:SEG_XBC + c0 + step]
        acc = convb_ref[:, cs] + convw_ref[0:1, cs] * s0
        acc = acc + convw_ref[1:2, cs] * s1
        acc = acc + convw_ref[2:3, cs] * s2
        acc = acc + convw_ref[3:4, cs] * xn
        act_ref[:, cs] = _silu(acc)
        ncs_ref[:, c0:c0 + step] = s1
        ncs_ref[:, CONV_DIM + c0:CONV_DIM + c0 + step] = s2
        ncs_ref[:, 2 * CONV_DIM + c0:2 * CONV_DIM + c0 + step] = xn
    dt = _softplus(proj_ref[:, SEG_DT:SEG_DT + LANES] + dtb_ref[...])
    e_ref[...] = jnp.exp(dt * (-jnp.exp(alog_ref[...])))
    dtx_ref[...] = _dot(dt, xexp_ref[...], precision=HIGHEST) * act_ref[:, 0:D_INNER]


def _s_conv(proj, conv_state2d, conv_w, conv_b, dtb, alog, xexp):
    n = proj.shape[0]
    return pl.pallas_call(
        _s_conv_body,
        out_shape=[jax.ShapeDtypeStruct((n, CONV_DIM), F32), jax.ShapeDtypeStruct((n, 3 * CONV_DIM), F32),
                   jax.ShapeDtypeStruct((n, D_INNER), F32), jax.ShapeDtypeStruct((n, LANES), F32)],
        compiler_params=pltpu.CompilerParams(vmem_limit_bytes=48 * 1024 * 1024),
        name="sample_conv",
    )(proj, conv_state2d, conv_w, conv_b, dtb, alog, xexp)


def _s_state_body(bb, e_ref, st_ref, dtxt_ref, b_ref, c_ref, so_ref, yt_ref):
    base = pl.program_id(0) * bb
    lane = lax.broadcasted_iota(jnp.int32, (SSD_HEAD_DIM, LANES), 1)

    for bl in range(bb):
        for h in range(SSD_HEADS):
            g = h // SSD_HPG
            brow = b_ref[bl, :, g * SSD_STATE:(g + 1) * SSD_STATE]
            xcol = dtxt_ref[bl, :, h:h + 1]
            so_ref[bl, h] = st_ref[bl, h] * e_ref[base + bl, h] + xcol * brow
        yacc = jnp.zeros((SSD_HEAD_DIM, LANES), F32)
        for h in range(SSD_HEADS):
            g = h // SSD_HPG
            crow = c_ref[bl, :, g * SSD_STATE:(g + 1) * SSD_STATE]
            yacc = jnp.where(lane == h, jnp.sum(so_ref[bl, h] * crow, axis=-1, keepdims=True), yacc)
        yt_ref[bl] = yacc


def _s_state(e, state, dtxt, bmat, cmat):
    n = state.shape[0]
    bb = _tile(n, 2, 1)
    sblk = (bb, SSD_HEADS, SSD_HEAD_DIM, SSD_STATE)
    bmat = bmat.reshape(n, 1, SSD_GROUPS * SSD_STATE)
    cmat = cmat.reshape(n, 1, SSD_GROUPS * SSD_STATE)
    return pl.pallas_call(
        functools.partial(_s_state_body, bb),
        grid=(n // bb,),
        in_specs=[pl.BlockSpec(memory_space=pltpu.SMEM),
                  pl.BlockSpec(sblk, lambda i: (i, 0, 0, 0)),
                  pl.BlockSpec((bb, SSD_HEAD_DIM, SSD_HEADS), lambda i: (i, 0, 0)),
                  pl.BlockSpec((bb, 1, SSD_GROUPS * SSD_STATE), lambda i: (i, 0, 0)),
                  pl.BlockSpec((bb, 1, SSD_GROUPS * SSD_STATE), lambda i: (i, 0, 0))],
        out_specs=[pl.BlockSpec(sblk, lambda i: (i, 0, 0, 0)),
                   pl.BlockSpec((bb, SSD_HEAD_DIM, LANES), lambda i: (i, 0, 0))],
        out_shape=[jax.ShapeDtypeStruct(state.shape, F32), jax.ShapeDtypeStruct((n, SSD_HEAD_DIM, LANES), F32)],
        compiler_params=_params(("parallel",), 40),
        name="sample_ssd_state",
    )(e, state, dtxt, bmat, cmat)


def _s_attn_body(bb, qexp_ref, ck_ref, cv_ref, kn_ref, vn_ref, sink_ref, bias_ref,
                 y_ref, ok_ref, ov_ref):
    W = WINDOW
    bias = bias_ref[...]
    sink = sink_ref[...]
    for bl in range(bb):
        kn = kn_ref[bl]
        vn = vn_ref[bl]
        qe = qexp_ref[bl].astype(BF16)
        s = _dot_nt(qe, ck_ref[bl].astype(BF16)) * (HEAD_DIM ** -0.5) - bias
        sn = jnp.sum(qe.astype(F32) * _bf16_round(kn), axis=-1, keepdims=True) * (HEAD_DIM ** -0.5)
        m = jnp.maximum(jnp.maximum(jnp.max(s, axis=-1, keepdims=True), sn), sink)
        p = jnp.exp(s - m)
        pn = jnp.exp(sn - m)
        denom = jnp.sum(p, axis=-1, keepdims=True) + pn + jnp.exp(sink - m)
        o = _dot((p / denom).astype(BF16), cv_ref[bl].astype(BF16)) + _bf16_round(pn / denom) * _bf16_round(vn)
        for h in range(N_HEADS):
            kv = h // GQA
            y_ref[bl, :, h * HEAD_DIM:(h + 1) * HEAD_DIM] = o[h:h + 1, kv * HEAD_DIM:(kv + 1) * HEAD_DIM]
        ok_ref[bl, 0:W - 1, :] = ck_ref[bl, 1:W, :]
        ok_ref[bl, W - 1:W, :] = kn
        ov_ref[bl, 0:W - 1, :] = cv_ref[bl, 1:W, :]
        ov_ref[bl, W - 1:W, :] = vn


def _s_attn(qexp, ck, cv, kn, vn, sink_col, bias_tile):
    n = ck.shape[0]
    bb = _tile(n, 8, 1)
    cblk = pl.BlockSpec((bb, WINDOW, KV_WIDTH), lambda i: (i, 0, 0))
    rblk = pl.BlockSpec((bb, 1, KV_WIDTH), lambda i: (i, 0, 0))
    y, ok, ov = pl.pallas_call(
        functools.partial(_s_attn_body, bb),
        grid=(n // bb,),
        in_specs=[pl.BlockSpec((bb, N_HEADS, KV_WIDTH), lambda i: (i, 0, 0)), cblk, cblk, rblk, rblk,
                  pl.BlockSpec((N_HEADS, 1), lambda i: (0, 0)), pl.BlockSpec((N_HEADS, WINDOW), lambda i: (0, 0))],
        out_specs=[pl.BlockSpec((bb, 1, ATTN_WIDTH), lambda i: (i, 0, 0)), cblk, cblk],
        out_shape=[jax.ShapeDtypeStruct((n, 1, ATTN_WIDTH), F32), jax.ShapeDtypeStruct(ck.shape, F32),
                   jax.ShapeDtypeStruct(cv.shape, F32)],
        compiler_params=_params(("parallel",), 32),
        name="sample_window_attn",
    )(qexp, ck, cv, kn.reshape(n, 1, KV_WIDTH), vn.reshape(n, 1, KV_WIDTH), sink_col, bias_tile)
    return y.reshape(n, ATTN_WIDTH), ok, ov


def _s_post_body(y_ref, act_ref, proj_ref, att_ref, x_ref, dskip_ref, gssd_ref, wdown_ref, wmix_ref,
                 gc_ref, wcq_ref, h1_ref, qc_ref, ybuf):
    gw = D_INNER // SSD_GROUPS
    for g in range(SSD_GROUPS):
        gs_ = slice(g * gw, (g + 1) * gw)
        yg = (y_ref[:, gs_] + dskip_ref[:, gs_] * act_ref[:, gs_]) * _silu(proj_ref[:, SEG_Z + g * gw:SEG_Z + (g + 1) * gw])
        ybuf[:, gs_] = _rms(yg, gssd_ref[:, gs_])
    y_ssd = _dot(ybuf[...].astype(BF16), wdown_ref[...])
    merged = (_sigmoid(proj_ref[:, SEG_GS:SEG_GS + D_MODEL]) * y_ssd
              + _sigmoid(proj_ref[:, SEG_GA:SEG_GA + D_MODEL]) * att_ref[...])
    h1 = x_ref[...] + _dot(merged.astype(BF16), wmix_ref[...])
    h1_ref[...] = h1
    qc_ref[...] = _bf16_round(_dot(_rms(h1, gc_ref[...]).astype(BF16), wcq_ref[...]))


def _s_post(y, act, proj, att, x, dskip_x, g_ssd, w_down, w_mix, g_cross, w_cq):
    n = x.shape[0]
    return pl.pallas_call(
        _s_post_body,
        out_shape=[jax.ShapeDtypeStruct((n, D_MODEL), F32), jax.ShapeDtypeStruct((n, C_WIDTH), F32)],
        scratch_shapes=[pltpu.VMEM((n, D_INNER), F32)],
        compiler_params=pltpu.CompilerParams(vmem_limit_bytes=48 * 1024 * 1024),
        name="sample_post_mixer",
    )(y, act, proj, att, x, dskip_x, g_ssd, w_down, w_mix, g_cross, w_cq)


def _s_cross_body(bb, qc_ref, mk_ref, mv_ref, o_ref):
    scale = C_HEAD_DIM ** -0.5
    for bl in range(bb):
        q = qc_ref[bl]
        s = jnp.sum(_bf16_round(mk_ref[bl]) * q[None], axis=-1, keepdims=True) * scale
        m = jnp.max(s, axis=0, keepdims=True)
        p = jnp.exp(s - m)
        p = _bf16_round(p / jnp.sum(p, axis=0, keepdims=True))
        o_ref[bl] = jnp.sum(p * _bf16_round(mv_ref[bl]), axis=0)


def _s_cross(qc, mk, mv):
    n = qc.shape[0]
    bb = _tile(n, 4, 1)
    mblk = pl.BlockSpec((bb, MEM_LEN, C_HEADS, C_HEAD_DIM), lambda i: (i, 0, 0, 0))
    rblk = pl.BlockSpec((bb, C_HEADS, C_HEAD_DIM), lambda i: (i, 0, 0))
    return pl.pallas_call(
        functools.partial(_s_cross_body, bb),
        grid=(n // bb,),
        in_specs=[rblk, mblk, mblk],
        out_specs=rblk,
        out_shape=jax.ShapeDtypeStruct((n, C_HEADS, C_HEAD_DIM), F32),
        compiler_params=_params(("parallel",), 40),
        name="sample_cross_attn",
    )(qc.reshape(n, C_HEADS, C_HEAD_DIM), mk, mv).reshape(n, C_WIDTH)


def _s_route_body(o_ref, h1_ref, wco_ref, gm_ref, wr_ref, br_ref, h2_ref, xm_ref, route_ref):
    h2 = h1_ref[...] + _dot(o_ref[...].astype(BF16), wco_ref[...])
    h2_ref[...] = h2
    xm = _rms(h2, gm_ref[...])
    xm_ref[...] = _pack_rows(xm)
    logits = _dot(xm.astype(BF16), wr_ref[...]) + br_ref[...]
    route_ref[...] = _route(logits).T[0:SUBLANES, :]


def _s_route(o, h1, w_co, g_moe, w_r, b_r):
    n = o.shape[0]
    return pl.pallas_call(
        _s_route_body,
        out_shape=[jax.ShapeDtypeStruct((n, D_MODEL), F32), jax.ShapeDtypeStruct((n, PACK_W), jnp.uint32),
                   jax.ShapeDtypeStruct((SUBLANES, n), F32)],
        compiler_params=pltpu.CompilerParams(vmem_limit_bytes=32 * 1024 * 1024),
        name="sample_cross_out_route",
    )(o, h1, w_co, g_moe, w_r, b_r)


def _pack_in_weights(w_in):
    cuts = np.cumsum((D_INNER, CONV_DIM, SSD_HEADS, ATTN_WIDTH, KV_WIDTH, KV_WIDTH, D_MODEL, D_MODEL))[:-1]
    z, xbc, dt, q, k, v, gs, ga = jnp.split(w_in, [int(c) for c in cuts], axis=1)
    dt = jnp.pad(dt, ((0, 0), (0, PACKED_COLS - SEG_DT - SSD_HEADS)))

    def dup(w):
        w = w.reshape(w.shape[0], N_KV, 1, HEAD_DIM)
        return jnp.broadcast_to(w, (w.shape[0], N_KV, 2, HEAD_DIM)).reshape(w.shape[0], KV_DUP)

    return jnp.concatenate([z, xbc, q, dup(k), dup(v), gs, ga, dt], axis=1)


def _undup(x):
    lead = x.shape[:-1]
    return x.reshape(lead + (N_KV, 2, HEAD_DIM))[..., 0, :].reshape(lead + (KV_WIDTH,))


def _head_expand_matrix():
    m = np.zeros((LANES, D_INNER), np.float32)
    for h in range(SSD_HEADS):
        m[h, h * SSD_HEAD_DIM:(h + 1) * SSD_HEAD_DIM] = 1.0
    return m


def _row(v, width=None):
    v = v.reshape(1, -1)
    if width is not None and v.shape[1] < width:
        v = jnp.pad(v, ((0, 0), (0, width - v.shape[1])))
    return v


def kernel(x_prompt, x_sample, state_ssd, state_conv, cache_win_k, cache_win_v, cache_mem_k, cache_mem_v, mem_prompt, g_mix, w_in, conv_w, conv_b, dt_bias, a_log, d_skip, g_ssd, w_ssd_down, attn_sinks, w_mix_out, g_cross, g_mem, w_cq, w_ckv, w_co, g_moe, w_route_group, b_route_group, w_route_expert, b_route_expert, w_e_gate, w_e_up, w_e_down, g_final):
    assert g_mix.shape[0] == 1, "single layer"
    b, seq, _ = x_prompt.shape
    n_s = x_sample.shape[0]
    t_p = b * seq
    assert seq % CHUNK == 0 and x_sample.shape[1] == 1

    w_packed_bf = _pack_in_weights(w_in[0].astype(BF16))
    w_down_bf, w_mix_bf = w_ssd_down[0].astype(BF16), w_mix_out[0].astype(BF16)
    w_cq_bf, w_co_bf = w_cq[0].astype(BF16), w_co[0].astype(BF16)
    g_mix_r, g_cross_r, g_mem_r, g_moe_r, g_final_r = (_row(g_mix[0]), _row(g_cross[0]), _row(g_mem[0]),
                                                        _row(g_moe[0]), _row(g_final))
    conv_b_r = _row(conv_b[0])
    dtb_r = _row(dt_bias[0], LANES)
    alog_r = _row(a_log[0], LANES)
    dskip_x = _row(jnp.repeat(d_skip[0], SSD_HEAD_DIM))
    g_ssd_r = _row(g_ssd[0])
    w_r = jnp.pad(jnp.concatenate([w_route_group[0], w_route_expert[0]], axis=1),
                  ((0, 0), (0, ROUTE_COLS - N_GROUPS - N_EXPERTS))).astype(BF16)
    b_r = _row(jnp.concatenate([b_route_group[0], b_route_expert[0]]), ROUTE_COLS)

    zs, act, q, k, v, gs, ga, dt, p_conv8 = _inproj(x_prompt.reshape(t_p, D_MODEL), g_mix_r, w_packed_bf,
                                                    conv_w[0], conv_b_r, seq)
    p_conv = p_conv8[:, SUBLANES - (CONV_K - 1):, :]
    r3 = lambda a: a.reshape(b, seq, a.shape[-1])
    h1, p_state, p_wk, p_wv = _mixer(
        x_prompt, r3(zs), r3(act), r3(q), r3(k), r3(v), r3(gs), r3(ga), r3(dt),
        dtb_r, alog_r, dskip_x, g_ssd_r, attn_sinks[0], w_down_bf, w_mix_bf)
    mkv = _memkv(mem_prompt.reshape(b * MEM_LEN, D_MODEL), g_mem_r, w_ckv[0].astype(BF16))
    h2_p, xm_p, route_p = _cross(h1, mkv.reshape(b, MEM_LEN, 2 * C_WIDTH), g_cross_r,
                                 w_cq_bf, w_co_bf, g_moe_r, w_r, b_r)

    xs2 = x_sample.reshape(n_s, D_MODEL)
    proj = _s_inproj(xs2, g_mix_r, w_packed_bf)
    xexp = jnp.asarray(_head_expand_matrix())
    act_s, s_conv, dtx, e_s = _s_conv(proj, state_conv[0].reshape(n_s, 3 * CONV_DIM), conv_w[0], conv_b_r,
                                      dtb_r, alog_r, xexp)
    dtxt = dtx.reshape(n_s, SSD_HEADS, SSD_HEAD_DIM).transpose(0, 2, 1)
    s_state, yt = _s_state(e_s, state_ssd[0], dtxt, act_s[:, D_INNER:D_INNER + SSD_GROUPS * SSD_STATE],
                           act_s[:, D_INNER + SSD_GROUPS * SSD_STATE:])
    y_s = yt[:, :, :SSD_HEADS].transpose(0, 2, 1).reshape(n_s, D_INNER)
    bias_tile = (jnp.asarray(ALIBI_SLOPES, F32)[:, None]
                 * (WINDOW - jnp.arange(WINDOW, dtype=jnp.int32)).astype(F32)[None, :])
    q_s = proj[:, SEG_Q:SEG_Q + ATTN_WIDTH].reshape(n_s, N_KV, GQA, 1, HEAD_DIM)
    kv_eye = jnp.eye(N_KV, dtype=F32).reshape(1, N_KV, 1, N_KV, 1)
    qexp = (q_s * kv_eye).reshape(n_s, N_HEADS, KV_WIDTH)
    att_s, s_wk, s_wv = _s_attn(qexp, cache_win_k[0].reshape(n_s, WINDOW, KV_WIDTH),
                                cache_win_v[0].reshape(n_s, WINDOW, KV_WIDTH),
                                _undup(proj[:, SEG_K:SEG_K + KV_DUP]), _undup(proj[:, SEG_V:SEG_V + KV_DUP]),
                                attn_sinks[0].reshape(N_HEADS, 1), bias_tile)
    h1_s, qc_s = _s_post(y_s, act_s, proj, att_s, xs2, dskip_x, g_ssd_r, w_down_bf, w_mix_bf, g_cross_r, w_cq_bf)
    o_s = _s_cross(qc_s, cache_mem_k[0], cache_mem_v[0])
    h2_s, xm_s, route_s = _s_route(o_s, h1_s, w_co_bf, g_moe_r, w_r, b_r)

    t_all = t_p + n_s
    t_pad = -(-t_all // RANK_TILE) * RANK_TILE
    route_all = jnp.concatenate([route_p, route_s, jnp.full((SUBLANES, t_pad - t_all), -1.0, F32)], axis=1)
    rank, counts = _rank(route_all)
    cnt = counts[:, 0].astype(jnp.int32)
    padded = (cnt + MOE_ROWS - 1) // MOE_ROWS * MOE_ROWS
    pad_end = jnp.cumsum(padded)
    offs = (pad_end - padded).astype(F32)
    nb = -(-(2 * t_all) // MOE_ROWS) + N_EXPERTS
    block_start = jnp.arange(nb, dtype=jnp.int32) * MOE_ROWS
    block_e = jnp.minimum(jnp.sum((pad_end[None, :] <= block_start[:, None]).astype(jnp.int32), axis=1),
                          N_EXPERTS - 1)
    n_used = (pad_end[-1] // MOE_ROWS).astype(jnp.int32).reshape(1)
    dest = _dest(route_all, rank, jnp.broadcast_to(offs[:, None], (N_EXPERTS, LANES)))
    dest_p = dest[0:2, :t_p].T.reshape(-1)
    dest_s = dest[0:2, t_p:t_all].T.reshape(-1)
    xb = jnp.zeros((nb * MOE_ROWS, PACK_W), jnp.uint32)
    xb = _dispatch(dest_p, xm_p.reshape(t_p, PACK_W), xb)
    xb = _dispatch(dest_s, xm_s, xb)
    yb = _experts(block_e, n_used, xb, w_e_gate[0], w_e_up[0], w_e_down[0])
    y_p = _combine(dest_p, yb, h2_p.reshape(t_p, D_MODEL), route_p[2:4, :].T, g_final_r)
    y_smp = _combine(dest_s, yb, h2_s, route_s[2:4, :].T, g_final_r)

    return (y_p.reshape(b, seq, D_MODEL), y_smp.reshape(n_s, 1, D_MODEL),
            p_state.reshape(1, b, SSD_HEADS, SSD_HEAD_DIM, SSD_STATE), p_conv[None],
            _undup(p_wk).reshape(1, b, WINDOW, N_KV, HEAD_DIM), _undup(p_wv).reshape(1, b, WINDOW, N_KV, HEAD_DIM),
            mkv[:, :C_WIDTH].reshape(1, b, MEM_LEN, C_HEADS, C_HEAD_DIM),
            mkv[:, C_WIDTH:].reshape(1, b, MEM_LEN, C_HEADS, C_HEAD_DIM),
            s_state[None], s_conv.reshape(1, n_s, CONV_K - 1, CONV_DIM),
            s_wk.reshape(1, n_s, WINDOW, N_KV, HEAD_DIM), s_wv.reshape(1, n_s, WINDOW, N_KV, HEAD_DIM))
```

```python
import functools

import jax
import jax.numpy as jnp
import numpy as np
from jax import lax
from jax.experimental import pallas as pl
from jax.experimental.pallas import tpu as pltpu

F32 = jnp.float32
BF16 = jnp.bfloat16
HIGHEST = lax.Precision.HIGHEST

D_MODEL = 1024
D_INNER = 2048
SSD_HEAD_DIM = 64
SSD_HEADS = 32
SSD_GROUPS = 4
SSD_HPG = 8
SSD_STATE = 128
CONV_K = 4
CONV_DIM = 3072
CHUNK = 128
HEAD_DIM = 64
N_HEADS = 16
N_KV = 4
GQA = 4
ATTN_WIDTH = 1024
KV_WIDTH = 256
WINDOW = 128
MEM_LEN = 256
C_HEADS = 4
C_HEAD_DIM = 128
C_WIDTH = 512
N_GROUPS = 4
EXP_PER_GROUP = 8
N_EXPERTS = 32
D_EXPERT = 512
EPS = 1e-6
NEG_INF = -1e30
LANES = 128
SUBLANES = 8

KV_DUP = 2 * KV_WIDTH
SEG_Z, SEG_XBC, SEG_Q, SEG_K, SEG_V, SEG_GS, SEG_GA, SEG_DT = (
    0, 2048, 5120, 6144, 6656, 7168, 8192, 9216)
PACKED_COLS = 9472
MASKED_DIST = 1e32
ROUTE_COLS = 128

MOE_ROWS = 256
ALIBI_SLOPES = tuple(2.0 ** (-8.0 * (h + 1) / N_HEADS) for h in range(N_HEADS))


def _tile(n, pref, mult=SUBLANES):
    if n <= pref:
        return n
    for t in range(pref, 0, -1):
        if n % t == 0 and t % mult == 0:
            return t
    return n


def _params(sem, vmem_mb):
    return pltpu.CompilerParams(dimension_semantics=sem, vmem_limit_bytes=vmem_mb * 1024 * 1024)


def _const_spec(shape):
    nd = len(shape)
    return pl.BlockSpec(shape, lambda *_: (0,) * nd, pipeline_mode=pl.Buffered(1))


def _sigmoid(x):
    return 1.0 / (1.0 + jnp.exp(-x))


def _silu(x):
    return x * _sigmoid(x)


def _sigmoid_t(x):
    return 0.5 * jnp.tanh(0.5 * x) + 0.5


def _silu_t(x):
    return x * _sigmoid_t(x)


def _softplus(x):
    return jnp.maximum(x, 0.0) + jnp.log1p(jnp.exp(-jnp.abs(x)))


def _rms(x, g):
    return x * lax.rsqrt(jnp.mean(x * x, axis=-1, keepdims=True) + EPS) * g


def _bf16_round(x):
    return x.astype(BF16).astype(F32)


PACK_W = D_MODEL // 2
HI16 = np.uint32(0xFFFF0000)


def _pack_rows(x):
    lo = pltpu.bitcast(_bf16_round(x[:, :PACK_W]), jnp.uint32) >> 16
    hi = pltpu.bitcast(_bf16_round(x[:, PACK_W:]), jnp.uint32) & HI16
    return lo | hi


def _unpack_rows(p, dtype):
    lo = pltpu.bitcast(p << 16, F32)
    hi = pltpu.bitcast(p & HI16, F32)
    return jnp.concatenate([lo, hi], axis=1).astype(dtype)


def _dot(a, b, precision=None):
    return jnp.dot(a, b, preferred_element_type=F32, precision=precision)


def _dot_nt(a, b, precision=None):
    return lax.dot_general(a, b, (((1,), (1,)), ((), ())), preferred_element_type=F32, precision=precision)


HALO = 16
CONV_CHUNK = 512


def _inproj_body(tiles_per_seq, x_ref, xh_ref, g_ref, w_ref, convw_ref, convb_ref,
                 zs_ref, act_ref, q_ref, k_ref, v_ref, gs_ref, ga_ref, dt_ref, pconv_ref, cbuf, xfull):
    tm = x_ref.shape[0]
    xb = _rms(x_ref[...], g_ref[...]).astype(BF16)
    step = 512

    def plain(ref, off, width, fn):
        tasks = []
        for c0 in range(0, width, step):
            cw = min(step, width - c0)

            def task(c0=c0, cw=cw):
                ref[:, c0:c0 + cw] = fn(_dot(xb, w_ref[:, off + c0:off + c0 + cw])).astype(ref.dtype)

            tasks.append(task)
        return tasks

    light = (plain(zs_ref, SEG_Z, D_INNER, _silu_t)
             + plain(q_ref, SEG_Q, ATTN_WIDTH, lambda r: r * HEAD_DIM ** -0.5)
             + plain(k_ref, SEG_K, KV_DUP, lambda r: r) + plain(v_ref, SEG_V, KV_DUP, lambda r: r)
             + plain(gs_ref, SEG_GS, D_MODEL, _sigmoid_t) + plain(ga_ref, SEG_GA, D_MODEL, _sigmoid_t)
             + plain(dt_ref, SEG_DT, LANES, lambda r: r))

    first = pl.program_id(0) % tiles_per_seq == 0
    xh = _rms(xh_ref[...], g_ref[...]) * jnp.where(first, 0.0, 1.0)
    xfull[0:HALO, :] = xh.astype(BF16)
    xfull[HALO:HALO + tm, :] = xb

    cstep = cbuf.shape[2]

    def conv_chunk(n):
        c0 = n * cstep
        cs = slice(c0, c0 + cstep)
        cb = cbuf.at[n % 2]
        cb[...] = _dot(xfull[...], w_ref[:, SEG_XBC + c0:SEG_XBC + c0 + cstep])
        acc = convb_ref[:, cs] + convw_ref[3:4, cs] * cb[HALO:HALO + tm, :]
        acc = acc + convw_ref[2:3, cs] * cb[HALO - 1:HALO - 1 + tm, :]
        acc = acc + convw_ref[1:2, cs] * cb[HALO - 2:HALO - 2 + tm, :]
        acc = acc + convw_ref[0:1, cs] * cb[HALO - 3:HALO - 3 + tm, :]
        act_ref[:, cs] = _silu_t(acc).astype(act_ref.dtype)
        pconv_ref[:, cs] = cb[HALO + tm - SUBLANES:HALO + tm, :]

    n_conv = CONV_DIM // cstep
    per = len(light) // n_conv
    for n in range(n_conv):
        conv_chunk(n)
        for task in light[n * per:(n + 1) * per]:
            task()
    for task in light[n_conv * per:]:
        task()


def _inproj(x2d, g_mix, w_packed, conv_w, conv_b, seq):
    t = x2d.shape[0]
    tm = _tile(seq, 512, HALO)
    tiles_per_seq = seq // tm
    widths = (D_INNER, CONV_DIM, ATTN_WIDTH, KV_DUP, KV_DUP, D_MODEL, D_MODEL)
    out_shape = ([jax.ShapeDtypeStruct((t, w), BF16) for w in widths]
                 + [jax.ShapeDtypeStruct((t, LANES), F32), jax.ShapeDtypeStruct((t // seq, SUBLANES, CONV_DIM), F32)])
    out_specs = ([pl.BlockSpec((tm, w), lambda i: (i, 0)) for w in widths]
                 + [pl.BlockSpec((tm, LANES), lambda i: (i, 0)),
                    pl.BlockSpec((None, SUBLANES, CONV_DIM), lambda i: (i // tiles_per_seq, 0, 0))])
    return pl.pallas_call(
        functools.partial(_inproj_body, tiles_per_seq),
        grid=(t // tm,),
        in_specs=[pl.BlockSpec((tm, D_MODEL), lambda i: (i, 0)),
                  pl.BlockSpec((HALO, D_MODEL), lambda i: (jnp.maximum(i * (tm // HALO) - 1, 0), 0)),
                  _const_spec((1, D_MODEL)),
                  _const_spec((D_MODEL, PACKED_COLS)),
                  _const_spec((CONV_K, CONV_DIM)), _const_spec((1, CONV_DIM))],
        out_specs=out_specs,
        out_shape=out_shape,
        scratch_shapes=[pltpu.VMEM((2, HALO + tm, CONV_CHUNK), F32), pltpu.VMEM((HALO + tm, D_MODEL), BF16)],
        compiler_params=_params(("arbitrary",), 58),
        name="prompt_inproj",
    )(x2d, x2d, g_mix, w_packed, conv_w, conv_b)


def _mixer_body(act, zs_ref, dt_ref, q_ref, k_ref, v_ref, gs_ref, ga_ref, x_ref,
                dtb_ref, alog_ref, dskip_ref, gssd_ref, sinks_ref, wdown_ref, wmix_ref,
                h1_ref, pstate_ref, pk_ref, pv_ref,
                ybuf, att, kprev, vprev, state):
    c = pl.program_id(1)
    last = pl.num_programs(1) - 1
    L = CHUNK

    @pl.when(c == 0)
    def _():
        kprev[...] = jnp.zeros_like(kprev)
        vprev[...] = jnp.zeros_like(vprev)
        state[...] = jnp.zeros_like(state)

    dt = _softplus(dt_ref[...] + dtb_ref[...])
    a_neg = -jnp.exp(alog_ref[...])
    da = dt * a_neg
    ri = lax.broadcasted_iota(jnp.int32, (L, L), 0)
    ci = lax.broadcasted_iota(jnp.int32, (L, L), 1)
    causal = ri >= ci
    tri = jnp.where(causal, 1.0, 0.0).astype(BF16)
    da_hi = da.astype(BF16)
    da_mid = (da - da_hi.astype(F32)).astype(BF16)
    da_lo = (da - da_hi.astype(F32) - da_mid.astype(F32)).astype(BF16)
    acum = _dot(tri, da_hi) + (_dot(tri, da_mid) + _dot(tri, da_lo))
    acum_t = acum.T
    dt_t = dt.T
    eacum = jnp.exp(acum)
    a_last = acum[L - 1:L, :]
    w_tail = jnp.exp(a_last - acum) * dt
    e_last = jnp.exp(a_last)
    lane = lax.broadcasted_iota(jnp.int32, (L, LANES), 1)
    lo_half = lane < SSD_HEAD_DIM
    lane1 = lax.broadcasted_iota(jnp.int32, (1, LANES), 1)
    lo_half1 = lane1 < SSD_HEAD_DIM

    def ssd_group(g):
        bgb = act[:, D_INNER + g * SSD_STATE:D_INNER + (g + 1) * SSD_STATE]
        cgb = act[:, D_INNER + (SSD_GROUPS + g) * SSD_STATE:D_INNER + (SSD_GROUPS + g + 1) * SSD_STATE]
        cb = _dot_nt(cgb, bgb)
        hg = state[g]
        yoff = _dot(cgb, hg.astype(BF16))
        bgt = bgb.astype(F32).T.astype(BF16)
        heads = tuple(range(g * SSD_HPG, (g + 1) * SSD_HPG))
        segs = [acum[:, h:h + 1] - acum_t[h:h + 1, :] for h in heads]
        decays = [jnp.exp(jnp.where(causal, s_, NEG_INF)) for s_ in segs]
        lmats = [(cb * d_ * dt_t[h:h + 1, :]).astype(BF16) for d_, h in zip(decays, heads)]
        xw, dsc = [], []
        for j in range(SSD_HPG // 2):
            h0, h1 = heads[2 * j], heads[2 * j + 1]
            col = h0 * SSD_HEAD_DIM
            xs_pair = act[:, col:col + LANES]
            y2 = _dot(jnp.concatenate([lmats[2 * j], lmats[2 * j + 1]], axis=0), xs_pair)
            ydiag = jnp.where(lo_half, y2[0:L], y2[L:2 * L])
            esc = jnp.where(lo_half, eacum[:, h0:h0 + 1], eacum[:, h1:h1 + 1])
            ybuf[:, col:col + LANES] = ydiag + yoff[:, 2 * j * SSD_HEAD_DIM:2 * j * SSD_HEAD_DIM + LANES] * esc
            wsc = jnp.where(lo_half, w_tail[:, h0:h0 + 1], w_tail[:, h1:h1 + 1])
            xw.append((xs_pair.astype(F32) * wsc).astype(BF16))
            dsc.append(jnp.where(lo_half1, e_last[:, h0:h0 + 1], e_last[:, h1:h1 + 1]))
        upd = _dot(bgt, jnp.concatenate(xw, axis=1))
        state[g] = hg * jnp.concatenate(dsc, axis=1) + upd
        gw = D_INNER // SSD_GROUPS
        gs_ = slice(g * gw, (g + 1) * gw)
        yg = (ybuf[:, gs_] + dskip_ref[:, gs_] * act[:, gs_].astype(F32)) * zs_ref[:, gs_].astype(F32)
        ybuf[:, gs_] = _rms(yg, gssd_ref[:, gs_])

    nd_c = jnp.where(causal, (ci - ri).astype(F32), -MASKED_DIST)
    nd_p = jnp.where(ci >= ri + jnp.where(c > 0, 0, L), (ci - ri - L).astype(F32), -MASKED_DIST)
    keep = (jnp.where(lo_half1, 1.0, 0.0).astype(BF16), jnp.where(lo_half1, 0.0, 1.0).astype(BF16))
    def attn_group(kv):
        kd_p = kprev[:, kv * LANES:(kv + 1) * LANES]
        kd_c = k_ref[:, kv * LANES:(kv + 1) * LANES]
        vd_p = vprev[:, kv * LANES:(kv + 1) * LANES]
        vd_c = v_ref[:, kv * LANES:(kv + 1) * LANES]
        heads = tuple(range(kv * GQA, (kv + 1) * GQA))
        pcs = [slice((kv * GQA + 2 * j) * HEAD_DIM, (kv * GQA + 2 * j) * HEAD_DIM + LANES) for j in range(GQA // 2)]
        qs = jnp.concatenate([q_ref[:, pcs[h % GQA // 2]] * keep[h % 2] for h in heads], axis=0)
        s_p = _dot_nt(qs, kd_p) + jnp.concatenate([ALIBI_SLOPES[h] * nd_p for h in heads], axis=0)
        s_c = _dot_nt(qs, kd_c) + jnp.concatenate([ALIBI_SLOPES[h] * nd_c for h in heads], axis=0)
        sink = jnp.concatenate([jnp.full((L, 1), sinks_ref[h], F32) for h in heads], axis=0)
        m = jnp.maximum(jnp.max(jnp.maximum(s_p, s_c), axis=-1, keepdims=True), sink)
        p_p = jnp.exp(s_p - m)
        p_c = jnp.exp(s_c - m)
        denom = jnp.sum(p_p + p_c, axis=-1, keepdims=True) + jnp.exp(sink - m)
        o = (_dot(p_p.astype(BF16), vd_p) + _dot(p_c.astype(BF16), vd_c)) / denom
        for j in range(GQA // 2):
            att[:, pcs[j]] = jnp.where(lo_half, o[2 * j * L:(2 * j + 1) * L], o[(2 * j + 1) * L:(2 * j + 2) * L])

    for g in range(SSD_GROUPS):
        ssd_group(g)
        attn_group(g)
    kprev[...] = k_ref[...]
    vprev[...] = v_ref[...]

    y_ssd = _dot(ybuf[...].astype(BF16), wdown_ref[...])
    merged = gs_ref[...].astype(F32) * y_ssd + ga_ref[...].astype(F32) * att[...]
    h1_ref[...] = x_ref[...] + _dot(merged.astype(BF16), wmix_ref[...])

    @pl.when(c == last)
    def _():
        for g in range(SSD_GROUPS):
            pstate_ref[g] = state[g].T
        pk_ref[...] = k_ref[...].astype(F32)
        pv_ref[...] = v_ref[...].astype(F32)


def _mixer(x, zs, act, q, k, v, gs, ga, dt, dtb, alog, dskip_x, g_ssd, sinks, w_down, w_mix):
    b, seq, _ = x.shape
    nc = seq // CHUNK

    def blk(width):
        return pl.BlockSpec((None, CHUNK, width), lambda i, j: (i, j, 0))

    def per_b(*shape):
        nd = len(shape)
        return pl.BlockSpec((None,) + shape, lambda i, j: (i,) + (0,) * nd)

    in_specs = [blk(CONV_DIM), blk(D_INNER), blk(LANES), blk(ATTN_WIDTH), blk(KV_DUP), blk(KV_DUP),
                blk(D_MODEL), blk(D_MODEL), blk(D_MODEL),
                _const_spec((1, LANES)), _const_spec((1, LANES)), _const_spec((1, D_INNER)), _const_spec((1, D_INNER)),
                pl.BlockSpec(memory_space=pltpu.SMEM),
                _const_spec((D_INNER, D_MODEL)), _const_spec((D_MODEL, D_MODEL))]
    out_shape = [jax.ShapeDtypeStruct((b, seq, D_MODEL), F32),
                 jax.ShapeDtypeStruct((b, SSD_GROUPS, SSD_HPG * SSD_HEAD_DIM, SSD_STATE), F32),
                 jax.ShapeDtypeStruct((b, WINDOW, KV_DUP), F32),
                 jax.ShapeDtypeStruct((b, WINDOW, KV_DUP), F32)]
    out_specs = [blk(D_MODEL), per_b(SSD_GROUPS, SSD_HPG * SSD_HEAD_DIM, SSD_STATE),
                 per_b(WINDOW, KV_DUP), per_b(WINDOW, KV_DUP)]
    scratch = [pltpu.VMEM((CHUNK, D_INNER), F32),
               pltpu.VMEM((CHUNK, ATTN_WIDTH), F32),
               pltpu.VMEM((CHUNK, KV_DUP), BF16), pltpu.VMEM((CHUNK, KV_DUP), BF16),
               pltpu.VMEM((SSD_GROUPS, SSD_STATE, SSD_HPG * SSD_HEAD_DIM), F32)]
    return pl.pallas_call(
        _mixer_body,
        grid=(b, nc),
        in_specs=in_specs,
        out_specs=out_specs,
        out_shape=out_shape,
        scratch_shapes=scratch,
        compiler_params=_params(("parallel", "arbitrary"), 48),
        name="prompt_mixer",
    )(act, zs, dt, q, k, v, gs, ga, x, dtb, alog, dskip_x, g_ssd, sinks, w_down, w_mix)


def _memkv_body(m_ref, g_ref, w_ref, o_ref):
    o_ref[...] = _dot(_rms(m_ref[...], g_ref[...]).astype(BF16), w_ref[...])


def _memkv(mem2d, g_mem, w_ckv):
    t = mem2d.shape[0]
    tm = _tile(t, 256)
    return pl.pallas_call(
        _memkv_body,
        grid=(t // tm,),
        in_specs=[pl.BlockSpec((tm, D_MODEL), lambda i: (i, 0)), _const_spec((1, D_MODEL)),
                  _const_spec((D_MODEL, 2 * C_WIDTH))],
        out_specs=pl.BlockSpec((tm, 2 * C_WIDTH), lambda i: (i, 0)),
        out_shape=jax.ShapeDtypeStruct((t, 2 * C_WIDTH), F32),
        compiler_params=_params(("parallel",), 32),
        name="memory_kv",
    )(mem2d, g_mem, w_ckv)


def _route(logits):
    rows = logits.shape[0]
    lane = lax.broadcasted_iota(jnp.int32, (rows, ROUTE_COLS), 1).astype(F32)
    big = 1e9
    is_g = lane < N_GROUPS
    lg = jnp.where(is_g, logits, NEG_INF)
    gmax = jnp.max(lg, axis=-1, keepdims=True)
    grp = jnp.min(jnp.where(lg == gmax, lane, big), axis=-1, keepdims=True)
    p_grp = 1.0 / jnp.sum(jnp.where(is_g, jnp.exp(lg - gmax), 0.0), axis=-1, keepdims=True)
    lo = N_GROUPS + EXP_PER_GROUP * grp
    in_grp = (lane >= lo) & (lane < lo + EXP_PER_GROUP)
    le = jnp.where(in_grp, logits, NEG_INF)
    m1 = jnp.max(le, axis=-1, keepdims=True)
    i1 = jnp.min(jnp.where(le == m1, lane, big), axis=-1, keepdims=True)
    le2 = jnp.where(lane == i1, NEG_INF, le)
    m2 = jnp.max(le2, axis=-1, keepdims=True)
    i2 = jnp.min(jnp.where(le2 == m2, lane, big), axis=-1, keepdims=True)
    t2 = jnp.exp(m2 - m1)
    g1 = p_grp / (1.0 + t2)
    g2 = p_grp * t2 / (1.0 + t2)
    info = jnp.where(lane == 0, i1 - N_GROUPS,
                     jnp.where(lane == 1, i2 - N_GROUPS,
                               jnp.where(lane == 2, g1, jnp.where(lane == 3, g2, 0.0))))
    return info


def _cross_body(h1_ref, mkv_ref, gc_ref, wcq_ref, wco_ref, gm_ref, wr_ref, br_ref,
                h2_ref, xm_ref, route_ref, obuf):
    h1 = h1_ref[...]
    xn = _rms(h1, gc_ref[...]).astype(BF16)
    qc = _dot(xn, wcq_ref[...])
    scale = C_HEAD_DIM ** -0.5
    for h in range(C_HEADS):
        hs = slice(h * C_HEAD_DIM, (h + 1) * C_HEAD_DIM)
        mk = mkv_ref[:, hs].astype(BF16)
        mv = mkv_ref[:, C_WIDTH + h * C_HEAD_DIM:C_WIDTH + (h + 1) * C_HEAD_DIM].astype(BF16)
        s = _dot_nt(qc[:, hs].astype(BF16), mk) * scale
        m = jnp.max(s, axis=-1, keepdims=True)
        p = jnp.exp(s - m)
        obuf[:, hs] = _dot(p.astype(BF16), mv) / jnp.sum(p, axis=-1, keepdims=True)
    h2 = h1 + _dot(obuf[...].astype(BF16), wco_ref[...])
    h2_ref[...] = h2
    xm = _rms(h2, gm_ref[...])
    xm_ref[...] = _pack_rows(xm)
    logits = _dot(xm.astype(BF16), wr_ref[...]) + br_ref[...]
    route_ref[...] = _route(logits).T[0:SUBLANES, :]


def _cross(h1, mkv, g_cross, w_cq, w_co, g_moe, w_r, b_r):
    b, seq, _ = h1.shape
    tq = _tile(seq, 1024)
    nq = seq // tq
    return pl.pallas_call(
        _cross_body,
        grid=(b, nq),
        in_specs=[pl.BlockSpec((None, tq, D_MODEL), lambda i, j: (i, j, 0)),
                  pl.BlockSpec((None, MEM_LEN, 2 * C_WIDTH), lambda i, j: (i, 0, 0)),
                  _const_spec((1, D_MODEL)), _const_spec((D_MODEL, C_WIDTH)), _const_spec((C_WIDTH, D_MODEL)),
                  _const_spec((1, D_MODEL)), _const_spec((D_MODEL, ROUTE_COLS)), _const_spec((1, ROUTE_COLS))],
        out_specs=[pl.BlockSpec((None, tq, D_MODEL), lambda i, j: (i, j, 0)),
                   pl.BlockSpec((None, tq, PACK_W), lambda i, j: (i, j, 0)),
                   pl.BlockSpec((SUBLANES, tq), lambda i, j: (0, i * nq + j))],
        out_shape=[jax.ShapeDtypeStruct((b, seq, D_MODEL), F32),
                   jax.ShapeDtypeStruct((b, seq, PACK_W), jnp.uint32),
                   jax.ShapeDtypeStruct((SUBLANES, b * seq), F32)],
        scratch_shapes=[pltpu.VMEM((tq, C_WIDTH), F32)],
        compiler_params=_params(("parallel", "parallel"), 48),
        name="prompt_cross_route",
    )(h1, mkv, g_cross, w_cq, w_co, g_moe, w_r, b_r)


RANK_TILE = 512


def _rank_body(route_ref, rank_ref, count_ref, carry):
    i = pl.program_id(0)

    @pl.when(i == 0)
    def _():
        carry[...] = jnp.zeros_like(carry)

    e1 = route_ref[0:1, :]
    e2 = route_ref[1:2, :]
    eid = lax.broadcasted_iota(jnp.int32, (N_EXPERTS, RANK_TILE), 0).astype(F32)
    is1 = e1 == eid
    is2 = e2 == eid
    onehot = jnp.where(is1 | is2, 1.0, 0.0)
    si = lax.broadcasted_iota(jnp.int32, (RANK_TILE, RANK_TILE), 0)
    ti = lax.broadcasted_iota(jnp.int32, (RANK_TILE, RANK_TILE), 1)
    before = jnp.where(si < ti, 1.0, 0.0).astype(BF16)
    prefix = _dot(onehot.astype(BF16), before) + carry[:, 0:1]
    r1 = jnp.sum(jnp.where(is1, prefix, 0.0), axis=0, keepdims=True)
    r2 = jnp.sum(jnp.where(is2, prefix, 0.0), axis=0, keepdims=True)
    row = lax.broadcasted_iota(jnp.int32, (SUBLANES, RANK_TILE), 0)
    rank_ref[...] = jnp.where(row == 0, r1, jnp.where(row == 1, r2, 0.0))
    carry[...] = carry[...] + jnp.sum(onehot, axis=1, keepdims=True)
    count_ref[...] = carry[...]


def _rank(route):
    tp = route.shape[1]
    return pl.pallas_call(
        _rank_body,
        grid=(tp // RANK_TILE,),
        in_specs=[pl.BlockSpec((SUBLANES, RANK_TILE), lambda i: (0, i))],
        out_specs=[pl.BlockSpec((SUBLANES, RANK_TILE), lambda i: (0, i)),
                   pl.BlockSpec((N_EXPERTS, LANES), lambda i: (0, 0))],
        out_shape=[jax.ShapeDtypeStruct((SUBLANES, tp), F32), jax.ShapeDtypeStruct((N_EXPERTS, LANES), F32)],
        scratch_shapes=[pltpu.VMEM((N_EXPERTS, LANES), F32)],
        compiler_params=_params(("arbitrary",), 32),
        name="moe_rank",
    )(route)


def _dest_body(route_ref, rank_ref, offs_ref, dest_ref):
    e1 = route_ref[0:1, :]
    e2 = route_ref[1:2, :]
    eid = lax.broadcasted_iota(jnp.int32, (N_EXPERTS, RANK_TILE), 0).astype(F32)
    offs = offs_ref[:, 0:1]
    d1 = jnp.sum(jnp.where(e1 == eid, offs, 0.0), axis=0, keepdims=True) + rank_ref[0:1, :]
    d2 = jnp.sum(jnp.where(e2 == eid, offs, 0.0), axis=0, keepdims=True) + rank_ref[1:2, :]
    row = lax.broadcasted_iota(jnp.int32, (SUBLANES, RANK_TILE), 0)
    dest_ref[...] = jnp.where(row == 0, d1, jnp.where(row == 1, d2, 0.0)).astype(jnp.int32)


def _dest(route, rank, offs):
    tp = route.shape[1]
    return pl.pallas_call(
        _dest_body,
        grid=(tp // RANK_TILE,),
        in_specs=[pl.BlockSpec((SUBLANES, RANK_TILE), lambda i: (0, i)),
                  pl.BlockSpec((SUBLANES, RANK_TILE), lambda i: (0, i)),
                  pl.BlockSpec((N_EXPERTS, LANES), lambda i: (0, 0))],
        out_specs=pl.BlockSpec((SUBLANES, RANK_TILE), lambda i: (0, i)),
        out_shape=jax.ShapeDtypeStruct((SUBLANES, tp), jnp.int32),
        compiler_params=_params(("parallel",), 32),
        name="moe_dest",
    )(route, rank, offs)


DMA_UNROLL = 8


def _row_copy(src, dst, s_row, d_row, sem):
    return pltpu.make_async_copy(src.at[pl.ds(s_row, 1)], dst.at[pl.ds(d_row, 1)], sem)


def _dispatch_body(td, dest_ref, xm_ref, xb_in_ref, xb_ref, sem):
    del xb_in_ref
    base = pl.program_id(0) * td

    def issue(g8, carry):
        t0 = pl.multiple_of(g8 * DMA_UNROLL, DMA_UNROLL)
        for k in range(DMA_UNROLL):
            row = base + t0 + k
            _row_copy(xm_ref, xb_ref, t0 + k, dest_ref[2 * row], sem).start()
            _row_copy(xm_ref, xb_ref, t0 + k, dest_ref[2 * row + 1], sem).start()
        return carry

    lax.fori_loop(0, td // DMA_UNROLL, issue, 0)

    def drain(g8, carry):
        for _ in range(2 * DMA_UNROLL):
            _row_copy(xm_ref, xb_ref, 0, 0, sem).wait()
        return carry

    lax.fori_loop(0, td // DMA_UNROLL, drain, 0)


def _dispatch(dest_flat, xm2d, xb):
    t = xm2d.shape[0]
    td = _tile(t, 256)
    return pl.pallas_call(
        functools.partial(_dispatch_body, td),
        grid_spec=pltpu.PrefetchScalarGridSpec(
            num_scalar_prefetch=1,
            grid=(t // td,),
            in_specs=[pl.BlockSpec((td, PACK_W), lambda i, d: (i, 0)), pl.BlockSpec(memory_space=pl.ANY)],
            out_specs=pl.BlockSpec(memory_space=pl.ANY),
            scratch_shapes=[pltpu.SemaphoreType.DMA],
        ),
        out_shape=jax.ShapeDtypeStruct(xb.shape, xb.dtype),
        input_output_aliases={2: 0},
        compiler_params=_params(("arbitrary",), 32),
        name="moe_dispatch",
    )(dest_flat, xm2d, xb)


def _expert_body(be_ref, nused_ref, xb_ref, wg_ref, wu_ref, wd_ref, yb_ref, wg_bf, wu_bf, wd_bf):
    i = pl.program_id(0)

    @pl.when(jnp.logical_or(i == 0, be_ref[i] != be_ref[jnp.maximum(i - 1, 0)]))
    def _():
        wg_bf[...] = wg_ref[...].astype(BF16)
        wu_bf[...] = wu_ref[...].astype(BF16)
        wd_bf[...] = wd_ref[...].astype(BF16)

    @pl.when(i < nused_ref[0])
    def _():
        x = _unpack_rows(xb_ref[...], BF16)
        hmid = _silu(_dot(x, wg_bf[...])) * _dot(x, wu_bf[...])
        yb_ref[...] = _pack_rows(_dot(hmid.astype(BF16), wd_bf[...]))

    @pl.when(i >= nused_ref[0])
    def _():
        yb_ref[...] = jnp.zeros_like(yb_ref)


def _experts(block_e, n_used, xb, wg, wu, wd):
    rows = xb.shape[0]
    nb = rows // MOE_ROWS

    def xmap(i, be, nu):
        return (jnp.minimum(i, nu[0] - 1), 0)

    def wmap(i, be, nu):
        return (be[i], 0, 0)

    return pl.pallas_call(
        _expert_body,
        grid_spec=pltpu.PrefetchScalarGridSpec(
            num_scalar_prefetch=2,
            grid=(nb,),
            in_specs=[pl.BlockSpec((MOE_ROWS, PACK_W), xmap),
                      pl.BlockSpec((None, D_MODEL, D_EXPERT), wmap),
                      pl.BlockSpec((None, D_MODEL, D_EXPERT), wmap),
                      pl.BlockSpec((None, D_EXPERT, D_MODEL), wmap)],
            out_specs=pl.BlockSpec((MOE_ROWS, PACK_W), lambda i, be, nu: (i, 0)),
            scratch_shapes=[pltpu.VMEM((D_MODEL, D_EXPERT), BF16), pltpu.VMEM((D_MODEL, D_EXPERT), BF16),
                            pltpu.VMEM((D_EXPERT, D_MODEL), BF16)],
        ),
        out_shape=jax.ShapeDtypeStruct((rows, PACK_W), jnp.uint32),
        compiler_params=_params(("arbitrary",), 48),
        name="moe_experts",
    )(block_e, n_used, xb, wg, wu, wd)


def _combine_body(tc, dest_ref, yb_ref, h2_ref, gate_ref, gf_ref, y_ref, buf, sem):
    i = pl.program_id(0)

    def issue(step, slot):
        base = step * tc

        def grp(g8, carry):
            t0 = pl.multiple_of(g8 * DMA_UNROLL, DMA_UNROLL)
            for k in range(DMA_UNROLL):
                row = base + t0 + k
                for e in range(2):
                    pltpu.make_async_copy(yb_ref.at[pl.ds(dest_ref[2 * row + e], 1)],
                                          buf.at[slot, e, pl.ds(t0 + k, 1)], sem.at[slot]).start()
            return carry

        lax.fori_loop(0, tc // DMA_UNROLL, grp, 0)

    @pl.when(i == 0)
    def _():
        issue(0, 0)

    @pl.when(i + 1 < pl.num_programs(0))
    def _():
        issue(i + 1, (i + 1) % 2)

    slot = i % 2

    def drain(g8, carry):
        for _ in range(2 * DMA_UNROLL):
            pltpu.make_async_copy(yb_ref.at[pl.ds(0, 1)], buf.at[slot, 0, pl.ds(0, 1)], sem.at[slot]).wait()
        return carry

    lax.fori_loop(0, tc // DMA_UNROLL, drain, 0)
    g = gate_ref[...]
    out = h2_ref[...] + (g[:, 0:1] * _unpack_rows(buf[slot, 0], F32) + g[:, 1:2] * _unpack_rows(buf[slot, 1], F32))
    y_ref[...] = _rms(out, gf_ref[...])


def _combine(dest_flat, yb, h2_2d, gates, g_final):
    t = h2_2d.shape[0]
    tc = _tile(t, 256)
    return pl.pallas_call(
        functools.partial(_combine_body, tc),
        grid_spec=pltpu.PrefetchScalarGridSpec(
            num_scalar_prefetch=1,
            grid=(t // tc,),
            in_specs=[pl.BlockSpec(memory_space=pl.ANY),
                      pl.BlockSpec((tc, D_MODEL), lambda i, d: (i, 0)),
                      pl.BlockSpec((tc, 2), lambda i, d: (i, 0)),
                      pl.BlockSpec((1, D_MODEL), lambda i, d: (0, 0))],
            out_specs=pl.BlockSpec((tc, D_MODEL), lambda i, d: (i, 0)),
            scratch_shapes=[pltpu.VMEM((2, 2, tc, PACK_W), jnp.uint32), pltpu.SemaphoreType.DMA((2,))],
        ),
        out_shape=jax.ShapeDtypeStruct((t, D_MODEL), F32),
        compiler_params=_params(("arbitrary",), 32),
        name="moe_combine",
    )(dest_flat, yb, h2_2d, gates, g_final)


def _s_inproj_body(x_ref, g_ref, w_ref, o_ref):
    o_ref[...] = _dot(_rms(x_ref[...], g_ref[...]).astype(BF16), w_ref[...])


def _s_inproj(x, g_mix, w_packed_f32):
    n = x.shape[0]
    tn = PACKED_COLS // 2
    return pl.pallas_call(
        _s_inproj_body,
        grid=(PACKED_COLS // tn,),
        in_specs=[pl.BlockSpec((n, D_MODEL), lambda j: (0, 0)), pl.BlockSpec((1, D_MODEL), lambda j: (0, 0)),
                  pl.BlockSpec((D_MODEL, tn), lambda j: (0, j))],
        out_specs=pl.BlockSpec((n, tn), lambda j: (0, j)),
        out_shape=jax.ShapeDtypeStruct((n, PACKED_COLS), F32),
        compiler_params=_params(("parallel",), 48),
        name="sample_inproj",
    )(x, g_mix, w_packed_f32)


def _s_conv_body(proj_ref, cs_ref, convw_ref, convb_ref, dtb_ref, alog_ref, xexp_ref,
                 act_ref, ncs_ref, dtx_ref, e_ref):
    step = 512
    for c0 in range(0, CONV_DIM, step):
        cs = slice(c0, c0 + step)
        s0 = cs_ref[:, c0:c0 + step]
        s1 = cs_ref[:, CONV_DIM + c0:CONV_DIM + c0 + step]
        s2 = cs_ref[:, 2 * CONV_DIM + c0:2 * CONV_DIM + c0 + step]
        xn = proj_ref[:, SEG_XBC + c0:SEG_XBC + c0 + step]
        acc = convb_ref[:, cs] + convw_ref[0:1, cs] * s0
        acc = acc + convw_ref[1:2, cs] * s1
        acc = acc + convw_ref[2:3, cs] * s2
        acc = acc + convw_ref[3:4, cs] * xn
        act_ref[:, cs] = _silu(acc)
        ncs_ref[:, c0:c0 + step] = s1
        ncs_ref[:, CONV_DIM + c0:CONV_DIM + c0 + step] = s2
        ncs_ref[:, 2 * CONV_DIM + c0:2 * CONV_DIM + c0 + step] = xn
    dt = _softplus(proj_ref[:, SEG_DT:SEG_DT + LANES] + dtb_ref[...])
    e_ref[...] = jnp.exp(dt * (-jnp.exp(alog_ref[...])))
    dtx_ref[...] = _dot(dt, xexp_ref[...], precision=HIGHEST) * act_ref[:, 0:D_INNER]


def _s_conv(proj, conv_state2d, conv_w, conv_b, dtb, alog, xexp):
    n = proj.shape[0]
    return pl.pallas_call(
        _s_conv_body,
        out_shape=[jax.ShapeDtypeStruct((n, CONV_DIM), F32), jax.ShapeDtypeStruct((n, 3 * CONV_DIM), F32),
                   jax.ShapeDtypeStruct((n, D_INNER), F32), jax.ShapeDtypeStruct((n, LANES), F32)],
        compiler_params=pltpu.CompilerParams(vmem_limit_bytes=48 * 1024 * 1024),
        name="sample_conv",
    )(proj, conv_state2d, conv_w, conv_b, dtb, alog, xexp)


def _s_state_body(bb, e_ref, st_ref, dtxt_ref, b_ref, c_ref, so_ref, yt_ref):
    base = pl.program_id(0) * bb
    lane = lax.broadcasted_iota(jnp.int32, (SSD_HEAD_DIM, LANES), 1)

    for bl in range(bb):
        for h in range(SSD_HEADS):
            g = h // SSD_HPG
            brow = b_ref[bl, :, g * SSD_STATE:(g + 1) * SSD_STATE]
            xcol = dtxt_ref[bl, :, h:h + 1]
            so_ref[bl, h] = st_ref[bl, h] * e_ref[base + bl, h] + xcol * brow
        yacc = jnp.zeros((SSD_HEAD_DIM, LANES), F32)
        for h in range(SSD_HEADS):
            g = h // SSD_HPG
            crow = c_ref[bl, :, g * SSD_STATE:(g + 1) * SSD_STATE]
            yacc = jnp.where(lane == h, jnp.sum(so_ref[bl, h] * crow, axis=-1, keepdims=True), yacc)
        yt_ref[bl] = yacc


def _s_state(e, state, dtxt, bmat, cmat):
    n = state.shape[0]
    bb = _tile(n, 2, 1)
    sblk = (bb, SSD_HEADS, SSD_HEAD_DIM, SSD_STATE)
    bmat = bmat.reshape(n, 1, SSD_GROUPS * SSD_STATE)
    cmat = cmat.reshape(n, 1, SSD_GROUPS * SSD_STATE)
    return pl.pallas_call(
        functools.partial(_s_state_body, bb),
        grid=(n // bb,),
        in_specs=[pl.BlockSpec(memory_space=pltpu.SMEM),
                  pl.BlockSpec(sblk, lambda i: (i, 0, 0, 0)),
                  pl.BlockSpec((bb, SSD_HEAD_DIM, SSD_HEADS), lambda i: (i, 0, 0)),
                  pl.BlockSpec((bb, 1, SSD_GROUPS * SSD_STATE), lambda i: (i, 0, 0)),
                  pl.BlockSpec((bb, 1, SSD_GROUPS * SSD_STATE), lambda i: (i, 0, 0))],
        out_specs=[pl.BlockSpec(sblk, lambda i: (i, 0, 0, 0)),
                   pl.BlockSpec((bb, SSD_HEAD_DIM, LANES), lambda i: (i, 0, 0))],
        out_shape=[jax.ShapeDtypeStruct(state.shape, F32), jax.ShapeDtypeStruct((n, SSD_HEAD_DIM, LANES), F32)],
        compiler_params=_params(("parallel",), 40),
        name="sample_ssd_state",
    )(e, state, dtxt, bmat, cmat)


def _s_attn_body(bb, qexp_ref, ck_ref, cv_ref, kn_ref, vn_ref, sink_ref, bias_ref,
                 y_ref, ok_ref, ov_ref):
    W = WINDOW
    bias = bias_ref[...]
    sink = sink_ref[...]
    for bl in range(bb):
        kn = kn_ref[bl]
        vn = vn_ref[bl]
        qe = qexp_ref[bl].astype(BF16)
        s = _dot_nt(qe, ck_ref[bl].astype(BF16)) * (HEAD_DIM ** -0.5) - bias
        sn = jnp.sum(qe.astype(F32) * _bf16_round(kn), axis=-1, keepdims=True) * (HEAD_DIM ** -0.5)
        m = jnp.maximum(jnp.maximum(jnp.max(s, axis=-1, keepdims=True), sn), sink)
        p = jnp.exp(s - m)
        pn = jnp.exp(sn - m)
        denom = jnp.sum(p, axis=-1, keepdims=True) + pn + jnp.exp(sink - m)
        o = _dot((p / denom).astype(BF16), cv_ref[bl].astype(BF16)) + _bf16_round(pn / denom) * _bf16_round(vn)
        for h in range(N_HEADS):
            kv = h // GQA
            y_ref[bl, :, h * HEAD_DIM:(h + 1) * HEAD_DIM] = o[h:h + 1, kv * HEAD_DIM:(kv + 1) * HEAD_DIM]
        ok_ref[bl, 0:W - 1, :] = ck_ref[bl, 1:W, :]
        ok_ref[bl, W - 1:W, :] = kn
        ov_ref[bl, 0:W - 1, :] = cv_ref[bl, 1:W, :]
        ov_ref[bl, W - 1:W, :] = vn


def _s_attn(qexp, ck, cv, kn, vn, sink_col, bias_tile):
    n = ck.shape[0]
    bb = _tile(n, 8, 1)
    cblk = pl.BlockSpec((bb, WINDOW, KV_WIDTH), lambda i: (i, 0, 0))
    rblk = pl.BlockSpec((bb, 1, KV_WIDTH), lambda i: (i, 0, 0))
    y, ok, ov = pl.pallas_call(
        functools.partial(_s_attn_body, bb),
        grid=(n // bb,),
        in_specs=[pl.BlockSpec((bb, N_HEADS, KV_WIDTH), lambda i: (i, 0, 0)), cblk, cblk, rblk, rblk,
                  pl.BlockSpec((N_HEADS, 1), lambda i: (0, 0)), pl.BlockSpec((N_HEADS, WINDOW), lambda i: (0, 0))],
        out_specs=[pl.BlockSpec((bb, 1, ATTN_WIDTH), lambda i: (i, 0, 0)), cblk, cblk],
        out_shape=[jax.ShapeDtypeStruct((n, 1, ATTN_WIDTH), F32), jax.ShapeDtypeStruct(ck.shape, F32),
                   jax.ShapeDtypeStruct(cv.shape, F32)],
        compiler_params=_params(("parallel",), 32),
        name="sample_window_attn",
    )(qexp, ck, cv, kn.reshape(n, 1, KV_WIDTH), vn.reshape(n, 1, KV_WIDTH), sink_col, bias_tile)
    return y.reshape(n, ATTN_WIDTH), ok, ov


def _s_post_body(y_ref, act_ref, proj_ref, att_ref, x_ref, dskip_ref, gssd_ref, wdown_ref, wmix_ref,
                 gc_ref, wcq_ref, h1_ref, qc_ref, ybuf):
    gw = D_INNER // SSD_GROUPS
    for g in range(SSD_GROUPS):
        gs_ = slice(g * gw, (g + 1) * gw)
        yg = (y_ref[:, gs_] + dskip_ref[:, gs_] * act_ref[:, gs_]) * _silu(proj_ref[:, SEG_Z + g * gw:SEG_Z + (g + 1) * gw])
        ybuf[:, gs_] = _rms(yg, gssd_ref[:, gs_])
    y_ssd = _dot(ybuf[...].astype(BF16), wdown_ref[...])
    merged = (_sigmoid(proj_ref[:, SEG_GS:SEG_GS + D_MODEL]) * y_ssd
              + _sigmoid(proj_ref[:, SEG_GA:SEG_GA + D_MODEL]) * att_ref[...])
    h1 = x_ref[...] + _dot(merged.astype(BF16), wmix_ref[...])
    h1_ref[...] = h1
    qc_ref[...] = _bf16_round(_dot(_rms(h1, gc_ref[...]).astype(BF16), wcq_ref[...]))


def _s_post(y, act, proj, att, x, dskip_x, g_ssd, w_down, w_mix, g_cross, w_cq):
    n = x.shape[0]
    return pl.pallas_call(
        _s_post_body,
        out_shape=[jax.ShapeDtypeStruct((n, D_MODEL), F32), jax.ShapeDtypeStruct((n, C_WIDTH), F32)],
        scratch_shapes=[pltpu.VMEM((n, D_INNER), F32)],
        compiler_params=pltpu.CompilerParams(vmem_limit_bytes=48 * 1024 * 1024),
        name="sample_post_mixer",
    )(y, act, proj, att, x, dskip_x, g_ssd, w_down, w_mix, g_cross, w_cq)


def _s_cross_body(bb, qc_ref, mk_ref, mv_ref, o_ref):
    scale = C_HEAD_DIM ** -0.5
    for bl in range(bb):
        q = qc_ref[bl]
        s = jnp.sum(_bf16_round(mk_ref[bl]) * q[None], axis=-1, keepdims=True) * scale
        m = jnp.max(s, axis=0, keepdims=True)
        p = jnp.exp(s - m)
        p = _bf16_round(p / jnp.sum(p, axis=0, keepdims=True))
        o_ref[bl] = jnp.sum(p * _bf16_round(mv_ref[bl]), axis=0)


def _s_cross(qc, mk, mv):
    n = qc.shape[0]
    bb = _tile(n, 4, 1)
    mblk = pl.BlockSpec((bb, MEM_LEN, C_HEADS, C_HEAD_DIM), lambda i: (i, 0, 0, 0))
    rblk = pl.BlockSpec((bb, C_HEADS, C_HEAD_DIM), lambda i: (i, 0, 0))
    return pl.pallas_call(
        functools.partial(_s_cross_body, bb),
        grid=(n // bb,),
        in_specs=[rblk, mblk, mblk],
        out_specs=rblk,
        out_shape=jax.ShapeDtypeStruct((n, C_HEADS, C_HEAD_DIM), F32),
        compiler_params=_params(("parallel",), 40),
        name="sample_cross_attn",
    )(qc.reshape(n, C_HEADS, C_HEAD_DIM), mk, mv).reshape(n, C_WIDTH)


def _s_route_body(o_ref, h1_ref, wco_ref, gm_ref, wr_ref, br_ref, h2_ref, xm_ref, route_ref):
    h2 = h1_ref[...] + _dot(o_ref[...].astype(BF16), wco_ref[...])
    h2_ref[...] = h2
    xm = _rms(h2, gm_ref[...])
    xm_ref[...] = _pack_rows(xm)
    logits = _dot(xm.astype(BF16), wr_ref[...]) + br_ref[...]
    route_ref[...] = _route(logits).T[0:SUBLANES, :]


def _s_route(o, h1, w_co, g_moe, w_r, b_r):
    n = o.shape[0]
    return pl.pallas_call(
        _s_route_body,
        out_shape=[jax.ShapeDtypeStruct((n, D_MODEL), F32), jax.ShapeDtypeStruct((n, PACK_W), jnp.uint32),
                   jax.ShapeDtypeStruct((SUBLANES, n), F32)],
        compiler_params=pltpu.CompilerParams(vmem_limit_bytes=32 * 1024 * 1024),
        name="sample_cross_out_route",
    )(o, h1, w_co, g_moe, w_r, b_r)


def _pack_in_weights(w_in):
    cuts = np.cumsum((D_INNER, CONV_DIM, SSD_HEADS, ATTN_WIDTH, KV_WIDTH, KV_WIDTH, D_MODEL, D_MODEL))[:-1]
    z, xbc, dt, q, k, v, gs, ga = jnp.split(w_in, [int(c) for c in cuts], axis=1)
    dt = jnp.pad(dt, ((0, 0), (0, PACKED_COLS - SEG_DT - SSD_HEADS)))

    def dup(w):
        w = w.reshape(w.shape[0], N_KV, 1, HEAD_DIM)
        return jnp.broadcast_to(w, (w.shape[0], N_KV, 2, HEAD_DIM)).reshape(w.shape[0], KV_DUP)

    return jnp.concatenate([z, xbc, q, dup(k), dup(v), gs, ga, dt], axis=1)


def _undup(x):
    lead = x.shape[:-1]
    return x.reshape(lead + (N_KV, 2, HEAD_DIM))[..., 0, :].reshape(lead + (KV_WIDTH,))


def _head_expand_matrix():
    m = np.zeros((LANES, D_INNER), np.float32)
    for h in range(SSD_HEADS):
        m[h, h * SSD_HEAD_DIM:(h + 1) * SSD_HEAD_DIM] = 1.0
    return m


def _row(v, width=None):
    v = v.reshape(1, -1)
    if width is not None and v.shape[1] < width:
        v = jnp.pad(v, ((0, 0), (0, width - v.shape[1])))
    return v


def kernel(x_prompt, x_sample, state_ssd, state_conv, cache_win_k, cache_win_v, cache_mem_k, cache_mem_v, mem_prompt, g_mix, w_in, conv_w, conv_b, dt_bias, a_log, d_skip, g_ssd, w_ssd_down, attn_sinks, w_mix_out, g_cross, g_mem, w_cq, w_ckv, w_co, g_moe, w_route_group, b_route_group, w_route_expert, b_route_expert, w_e_gate, w_e_up, w_e_down, g_final):
    assert g_mix.shape[0] == 1, "single layer"
    b, seq, _ = x_prompt.shape
    n_s = x_sample.shape[0]
    t_p = b * seq
    assert seq % CHUNK == 0 and x_sample.shape[1] == 1

    w_packed_bf = _pack_in_weights(w_in[0].astype(BF16))
    w_down_bf, w_mix_bf = w_ssd_down[0].astype(BF16), w_mix_out[0].astype(BF16)
    w_cq_bf, w_co_bf = w_cq[0].astype(BF16), w_co[0].astype(BF16)
    g_mix_r, g_cross_r, g_mem_r, g_moe_r, g_final_r = (_row(g_mix[0]), _row(g_cross[0]), _row(g_mem[0]),
                                                        _row(g_moe[0]), _row(g_final))
    conv_b_r = _row(conv_b[0])
    dtb_r = _row(dt_bias[0], LANES)
    alog_r = _row(a_log[0], LANES)
    dskip_x = _row(jnp.repeat(d_skip[0], SSD_HEAD_DIM))
    g_ssd_r = _row(g_ssd[0])
    w_r = jnp.pad(jnp.concatenate([w_route_group[0], w_route_expert[0]], axis=1),
                  ((0, 0), (0, ROUTE_COLS - N_GROUPS - N_EXPERTS))).astype(BF16)
    b_r = _row(jnp.concatenate([b_route_group[0], b_route_expert[0]]), ROUTE_COLS)

    zs, act, q, k, v, gs, ga, dt, p_conv8 = _inproj(x_prompt.reshape(t_p, D_MODEL), g_mix_r, w_packed_bf,
                                                    conv_w[0], conv_b_r, seq)
    p_conv = p_conv8[:, SUBLANES - (CONV_K - 1):, :]
    r3 = lambda a: a.reshape(b, seq, a.shape[-1])
    h1, p_state, p_wk, p_wv = _mixer(
        x_prompt, r3(zs), r3(act), r3(q), r3(k), r3(v), r3(gs), r3(ga), r3(dt),
        dtb_r, alog_r, dskip_x, g_ssd_r, attn_sinks[0], w_down_bf, w_mix_bf)
    mkv = _memkv(mem_prompt.reshape(b * MEM_LEN, D_MODEL), g_mem_r, w_ckv[0].astype(BF16))
    h2_p, xm_p, route_p = _cross(h1, mkv.reshape(b, MEM_LEN, 2 * C_WIDTH), g_cross_r,
                                 w_cq_bf, w_co_bf, g_moe_r, w_r, b_r)

    xs2 = x_sample.reshape(n_s, D_MODEL)
    proj = _s_inproj(xs2, g_mix_r, w_packed_bf)
    xexp = jnp.asarray(_head_expand_matrix())
    act_s, s_conv, dtx, e_s = _s_conv(proj, state_conv[0].reshape(n_s, 3 * CONV_DIM), conv_w[0], conv_b_r,
                                      dtb_r, alog_r, xexp)
    dtxt = dtx.reshape(n_s, SSD_HEADS, SSD_HEAD_DIM).transpose(0, 2, 1)
    s_state, yt = _s_state(e_s, state_ssd[0], dtxt, act_s[:, D_INNER:D_INNER + SSD_GROUPS * SSD_STATE],
                           act_s[:, D_INNER + SSD_GROUPS * SSD_STATE:])
    y_s = yt[:, :, :SSD_HEADS].transpose(0, 2, 1).reshape(n_s, D_INNER)
    bias_tile = (jnp.asarray(ALIBI_SLOPES, F32)[:, None]
                 * (WINDOW - jnp.arange(WINDOW, dtype=jnp.int32)).astype(F32)[None, :])
    q_s = proj[:, SEG_Q:SEG_Q + ATTN_WIDTH].reshape(n_s, N_KV, GQA, 1, HEAD_DIM)
    kv_eye = jnp.eye(N_KV, dtype=F32).reshape(1, N_KV, 1, N_KV, 1)
    qexp = (q_s * kv_eye).reshape(n_s, N_HEADS, KV_WIDTH)
    att_s, s_wk, s_wv = _s_attn(qexp, cache_win_k[0].reshape(n_s, WINDOW, KV_WIDTH),
                                cache_win_v[0].reshape(n_s, WINDOW, KV_WIDTH),
                                _undup(proj[:, SEG_K:SEG_K + KV_DUP]), _undup(proj[:, SEG_V:SEG_V + KV_DUP]),
                                attn_sinks[0].reshape(N_HEADS, 1), bias_tile)
    h1_s, qc_s = _s_post(y_s, act_s, proj, att_s, xs2, dskip_x, g_ssd_r, w_down_bf, w_mix_bf, g_cross_r, w_cq_bf)
    o_s = _s_cross(qc_s, cache_mem_k[0], cache_mem_v[0])
    h2_s, xm_s, route_s = _s_route(o_s, h1_s, w_co_bf, g_moe_r, w_r, b_r)

    t_all = t_p + n_s
    t_pad = -(-t_all // RANK_TILE) * RANK_TILE
    route_all = jnp.concatenate([route_p, route_s, jnp.full((SUBLANES, t_pad - t_all), -1.0, F32)], axis=1)
    rank, counts = _rank(route_all)
    cnt = counts[:, 0].astype(jnp.int32)
    padded = (cnt + MOE_ROWS - 1) // MOE_ROWS * MOE_ROWS
    pad_end = jnp.cumsum(padded)
    offs = (pad_end - padded).astype(F32)
    nb = -(-(2 * t_all) // MOE_ROWS) + N_EXPERTS
    block_start = jnp.arange(nb, dtype=jnp.int32) * MOE_ROWS
    block_e = jnp.minimum(jnp.sum((pad_end[None, :] <= block_start[:, None]).astype(jnp.int32), axis=1),
                          N_EXPERTS - 1)
    n_used = (pad_end[-1] // MOE_ROWS).astype(jnp.int32).reshape(1)
    dest = _dest(route_all, rank, jnp.broadcast_to(offs[:, None], (N_EXPERTS, LANES)))
    dest_p = dest[0:2, :t_p].T.reshape(-1)
    dest_s = dest[0:2, t_p:t_all].T.reshape(-1)
    xb = jnp.zeros((nb * MOE_ROWS, PACK_W), jnp.uint32)
    xb = _dispatch(dest_p, xm_p.reshape(t_p, PACK_W), xb)
    xb = _dispatch(dest_s, xm_s, xb)
    yb = _experts(block_e, n_used, xb, w_e_gate[0], w_e_up[0], w_e_down[0])
    y_p = _combine(dest_p, yb, h2_p.reshape(t_p, D_MODEL), route_p[2:4, :].T, g_final_r)
    y_smp = _combine(dest_s, yb, h2_s, route_s[2:4, :].T, g_final_r)

    return (y_p.reshape(b, seq, D_MODEL), y_smp.reshape(n_s, 1, D_MODEL),
            p_state.reshape(1, b, SSD_HEADS, SSD_HEAD_DIM, SSD_STATE), p_conv[None],
            _undup(p_wk).reshape(1, b, WINDOW, N_KV, HEAD_DIM), _undup(p_wv).reshape(1, b, WINDOW, N_KV, HEAD_DIM),
            mkv[:, :C_WIDTH].reshape(1, b, MEM_LEN, C_HEADS, C_HEAD_DIM),
            mkv[:, C_WIDTH:].reshape(1, b, MEM_LEN, C_HEADS, C_HEAD_DIM),
            s_state[None], s_conv.reshape(1, n_s, CONV_K - 1, CONV_DIM),
            s_wk.reshape(1, n_s, WINDOW, N_KV, HEAD_DIM), s_wv.reshape(1, n_s, WINDOW, N_KV, HEAD_DIM))
```

```python
import functools

import jax
import jax.numpy as jnp
import numpy as np
from jax import lax
from jax.experimental import pallas as pl
from jax.experimental.pallas import tpu as pltpu

F32 = jnp.float32
BF16 = jnp.bfloat16
HIGHEST = lax.Precision.HIGHEST

D_MODEL = 1024
D_INNER = 2048
SSD_HEAD_DIM = 64
SSD_HEADS = 32
SSD_GROUPS = 4
SSD_HPG = 8
SSD_STATE = 128
CONV_K = 4
CONV_DIM = 3072
CHUNK = 128
HEAD_DIM = 64
N_HEADS = 16
N_KV = 4
GQA = 4
ATTN_WIDTH = 1024
KV_WIDTH = 256
WINDOW = 128
MEM_LEN = 256
C_HEADS = 4
C_HEAD_DIM = 128
C_WIDTH = 512
N_GROUPS = 4
EXP_PER_GROUP = 8
N_EXPERTS = 32
D_EXPERT = 512
EPS = 1e-6
NEG_INF = -1e30
LANES = 128
SUBLANES = 8

KV_DUP = 2 * KV_WIDTH
SEG_Z, SEG_XBC, SEG_Q, SEG_K, SEG_V, SEG_GS, SEG_GA, SEG_DT = (
    0, 2048, 5120, 6144, 6656, 7168, 8192, 9216)
PACKED_COLS = 9472
MASKED_DIST = 1e32
ROUTE_COLS = 128

MOE_ROWS = 256
ALIBI_SLOPES = tuple(2.0 ** (-8.0 * (h + 1) / N_HEADS) for h in range(N_HEADS))


def _tile(n, pref, mult=SUBLANES):
    if n <= pref:
        return n
    for t in range(pref, 0, -1):
        if n % t == 0 and t % mult == 0:
            return t
    return n


def _params(sem, vmem_mb):
    return pltpu.CompilerParams(dimension_semantics=sem, vmem_limit_bytes=vmem_mb * 1024 * 1024)


def _const_spec(shape):
    nd = len(shape)
    return pl.BlockSpec(shape, lambda *_: (0,) * nd, pipeline_mode=pl.Buffered(1))


def _sigmoid(x):
    return 1.0 / (1.0 + jnp.exp(-x))


def _silu(x):
    return x * _sigmoid(x)


def _sigmoid_t(x):
    return 0.5 * jnp.tanh(0.5 * x) + 0.5


def _silu_t(x):
    return x * _sigmoid_t(x)


def _softplus(x):
    return jnp.maximum(x, 0.0) + jnp.log1p(jnp.exp(-jnp.abs(x)))


def _rms(x, g):
    return x * lax.rsqrt(jnp.mean(x * x, axis=-1, keepdims=True) + EPS) * g


def _bf16_round(x):
    return x.astype(BF16).astype(F32)


PACK_W = D_MODEL // 2
HI16 = np.uint32(0xFFFF0000)


def _pack_rows(x):
    lo = pltpu.bitcast(_bf16_round(x[:, :PACK_W]), jnp.uint32) >> 16
    hi = pltpu.bitcast(_bf16_round(x[:, PACK_W:]), jnp.uint32) & HI16
    return lo | hi


def _unpack_rows(p, dtype):
    lo = pltpu.bitcast(p << 16, F32)
    hi = pltpu.bitcast(p & HI16, F32)
    return jnp.concatenate([lo, hi], axis=1).astype(dtype)


def _dot(a, b, precision=None):
    return jnp.dot(a, b, preferred_element_type=F32, precision=precision)


def _dot_nt(a, b, precision=None):
    return lax.dot_general(a, b, (((1,), (1,)), ((), ())), preferred_element_type=F32, precision=precision)


HALO = 16
CONV_CHUNK = 512


def _inproj_body(tiles_per_seq, x_ref, xh_ref, g_ref, w_ref, convw_ref, convb_ref,
                 zs_ref, act_ref, q_ref, k_ref, v_ref, gs_ref, ga_ref, dt_ref, pconv_ref, cbuf, xfull):
    tm = x_ref.shape[0]
    xb = _rms(x_ref[...], g_ref[...]).astype(BF16)
    step = 512

    def plain(ref, off, width, fn):
        tasks = []
        for c0 in range(0, width, step):
            cw = min(step, width - c0)

            def task(c0=c0, cw=cw):
                ref[:, c0:c0 + cw] = fn(_dot(xb, w_ref[:, off + c0:off + c0 + cw])).astype(ref.dtype)

            tasks.append(task)
        return tasks

    light = (plain(zs_ref, SEG_Z, D_INNER, _silu_t)
             + plain(q_ref, SEG_Q, ATTN_WIDTH, lambda r: r * HEAD_DIM ** -0.5)
             + plain(k_ref, SEG_K, KV_DUP, lambda r: r) + plain(v_ref, SEG_V, KV_DUP, lambda r: r)
             + plain(gs_ref, SEG_GS, D_MODEL, _sigmoid_t) + plain(ga_ref, SEG_GA, D_MODEL, _sigmoid_t)
             + plain(dt_ref, SEG_DT, LANES, lambda r: r))

    first = pl.program_id(0) % tiles_per_seq == 0
    xh = _rms(xh_ref[...], g_ref[...]) * jnp.where(first, 0.0, 1.0)
    xfull[0:HALO, :] = xh.astype(BF16)
    xfull[HALO:HALO + tm, :] = xb

    cstep = cbuf.shape[2]

    def conv_chunk(n):
        c0 = n * cstep
        cs = slice(c0, c0 + cstep)
        cb = cbuf.at[n % 2]
        cb[...] = _dot(xfull[...], w_ref[:, SEG_XBC + c0:SEG_XBC + c0 + cstep])
        acc = convb_ref[:, cs] + convw_ref[3:4, cs] * cb[HALO:HALO + tm, :]
        acc = acc + convw_ref[2:3, cs] * cb[HALO - 1:HALO - 1 + tm, :]
        acc = acc + convw_ref[1:2, cs] * cb[HALO - 2:HALO - 2 + tm, :]
        acc = acc + convw_ref[0:1, cs] * cb[HALO - 3:HALO - 3 + tm, :]
        act_ref[:, cs] = _silu_t(acc).astype(act_ref.dtype)
        pconv_ref[:, cs] = cb[HALO + tm - SUBLANES:HALO + tm, :]

    n_conv = CONV_DIM // cstep
    per = len(light) // n_conv
    for n in range(n_conv):
        conv_chunk(n)
        for task in light[n * per:(n + 1) * per]:
            task()
    for task in light[n_conv * per:]:
        task()


def _inproj(x2d, g_mix, w_packed, conv_w, conv_b, seq):
    t = x2d.shape[0]
    tm = _tile(seq, 512, HALO)
    tiles_per_seq = seq // tm
    widths = (D_INNER, CONV_DIM, ATTN_WIDTH, KV_DUP, KV_DUP, D_MODEL, D_MODEL)
    out_shape = ([jax.ShapeDtypeStruct((t, w), BF16) for w in widths]
                 + [jax.ShapeDtypeStruct((t, LANES), F32), jax.ShapeDtypeStruct((t // seq, SUBLANES, CONV_DIM), F32)])
    out_specs = ([pl.BlockSpec((tm, w), lambda i: (i, 0)) for w in widths]
                 + [pl.BlockSpec((tm, LANES), lambda i: (i, 0)),
                    pl.BlockSpec((None, SUBLANES, CONV_DIM), lambda i: (i // tiles_per_seq, 0, 0))])
    return pl.pallas_call(
        functools.partial(_inproj_body, tiles_per_seq),
        grid=(t // tm,),
        in_specs=[pl.BlockSpec((tm, D_MODEL), lambda i: (i, 0)),
                  pl.BlockSpec((HALO, D_MODEL), lambda i: (jnp.maximum(i * (tm // HALO) - 1, 0), 0)),
                  _const_spec((1, D_MODEL)),
                  _const_spec((D_MODEL, PACKED_COLS)),
                  _const_spec((CONV_K, CONV_DIM)), _const_spec((1, CONV_DIM))],
        out_specs=out_specs,
        out_shape=out_shape,
        scratch_shapes=[pltpu.VMEM((2, HALO + tm, CONV_CHUNK), F32), pltpu.VMEM((HALO + tm, D_MODEL), BF16)],
        compiler_params=_params(("arbitrary",), 58),
        name="prompt_inproj",
    )(x2d, x2d, g_mix, w_packed, conv_w, conv_b)


def _mixer_body(act, zs_ref, dt_ref, q_ref, k_ref, v_ref, gs_ref, ga_ref, x_ref,
                dtb_ref, alog_ref, dskip_ref, gssd_ref, sinks_ref, wdown_ref, wmix_ref,
                h1_ref, pstate_ref, pk_ref, pv_ref,
                ybuf, att, kprev, vprev, state):
    c = pl.program_id(1)
    last = pl.num_programs(1) - 1
    L = CHUNK

    @pl.when(c == 0)
    def _():
        kprev[...] = jnp.zeros_like(kprev)
        vprev[...] = jnp.zeros_like(vprev)
        state[...] = jnp.zeros_like(state)

    dt = _softplus(dt_ref[...] + dtb_ref[...])
    a_neg = -jnp.exp(alog_ref[...])
    da = dt * a_neg
    ri = lax.broadcasted_iota(jnp.int32, (L, L), 0)
    ci = lax.broadcasted_iota(jnp.int32, (L, L), 1)
    causal = ri >= ci
    tri = jnp.where(causal, 1.0, 0.0).astype(BF16)
    da_hi = da.astype(BF16)
    da_mid = (da - da_hi.astype(F32)).astype(BF16)
    da_lo = (da - da_hi.astype(F32) - da_mid.astype(F32)).astype(BF16)
    acum = _dot(tri, da_hi) + (_dot(tri, da_mid) + _dot(tri, da_lo))
    acum_t = acum.T
    dt_t = dt.T
    eacum = jnp.exp(acum)
    a_last = acum[L - 1:L, :]
    w_tail = jnp.exp(a_last - acum) * dt
    e_last = jnp.exp(a_last)
    lane = lax.broadcasted_iota(jnp.int32, (L, LANES), 1)
    lo_half = lane < SSD_HEAD_DIM
    lane1 = lax.broadcasted_iota(jnp.int32, (1, LANES), 1)
    lo_half1 = lane1 < SSD_HEAD_DIM

    def ssd_group(g):
        bgb = act[:, D_INNER + g * SSD_STATE:D_INNER + (g + 1) * SSD_STATE]
        cgb = act[:, D_INNER + (SSD_GROUPS + g) * SSD_STATE:D_INNER + (SSD_GROUPS + g + 1) * SSD_STATE]
        cb = _dot_nt(cgb, bgb)
        hg = state[g]
        yoff = _dot(cgb, hg.astype(BF16))
        bgt = bgb.astype(F32).T.astype(BF16)
        heads = tuple(range(g * SSD_HPG, (g + 1) * SSD_HPG))
        segs = [acum[:, h:h + 1] - acum_t[h:h + 1, :] for h in heads]
        decays = [jnp.exp(jnp.where(causal, s_, NEG_INF)) for s_ in segs]
        lmats = [(cb * d_ * dt_t[h:h + 1, :]).astype(BF16) for d_, h in zip(decays, heads)]
        xw, dsc = [], []
        for j in range(SSD_HPG // 2):
            h0, h1 = heads[2 * j], heads[2 * j + 1]
            col = h0 * SSD_HEAD_DIM
            xs_pair = act[:, col:col + LANES]
            y2 = _dot(jnp.concatenate([lmats[2 * j], lmats[2 * j + 1]], axis=0), xs_pair)
            ydiag = jnp.where(lo_half, y2[0:L], y2[L:2 * L])
            esc = jnp.where(lo_half, eacum[:, h0:h0 + 1], eacum[:, h1:h1 + 1])
            ybuf[:, col:col + LANES] = ydiag + yoff[:, 2 * j * SSD_HEAD_DIM:2 * j * SSD_HEAD_DIM + LANES] * esc
            wsc = jnp.where(lo_half, w_tail[:, h0:h0 + 1], w_tail[:, h1:h1 + 1])
            xw.append((xs_pair.astype(F32) * wsc).astype(BF16))
            dsc.append(jnp.where(lo_half1, e_last[:, h0:h0 + 1], e_last[:, h1:h1 + 1]))
        upd = _dot(bgt, jnp.concatenate(xw, axis=1))
        state[g] = hg * jnp.concatenate(dsc, axis=1) + upd
        gw = D_INNER // SSD_GROUPS
        gs_ = slice(g * gw, (g + 1) * gw)
        yg = (ybuf[:, gs_] + dskip_ref[:, gs_] * act[:, gs_].astype(F32)) * zs_ref[:, gs_].astype(F32)
        ybuf[:, gs_] = _rms(yg, gssd_ref[:, gs_])

    nd_c = jnp.where(causal, (ci - ri).astype(F32), -MASKED_DIST)
    nd_p = jnp.where(ci >= ri + jnp.where(c > 0, 0, L), (ci - ri - L).astype(F32), -MASKED_DIST)
    keep = (jnp.where(lo_half1, 1.0, 0.0).astype(BF16), jnp.where(lo_half1, 0.0, 1.0).astype(BF16))
    def attn_group(kv):
        kd_p = kprev[:, kv * LANES:(kv + 1) * LANES]
        kd_c = k_ref[:, kv * LANES:(kv + 1) * LANES]
        vd_p = vprev[:, kv * LANES:(kv + 1) * LANES]
        vd_c = v_ref[:, kv * LANES:(kv + 1) * LANES]
        heads = tuple(range(kv * GQA, (kv + 1) * GQA))
        pcs = [slice((kv * GQA + 2 * j) * HEAD_DIM, (kv * GQA + 2 * j) * HEAD_DIM + LANES) for j in range(GQA // 2)]
        qs = jnp.concatenate([q_ref[:, pcs[h % GQA // 2]] * keep[h % 2] for h in heads], axis=0)
        s_p = _dot_nt(qs, kd_p) + jnp.concatenate([ALIBI_SLOPES[h] * nd_p for h in heads], axis=0)
        s_c = _dot_nt(qs, kd_c) + jnp.concatenate([ALIBI_SLOPES[h] * nd_c for h in heads], axis=0)
        sink = jnp.concatenate([jnp.full((L, 1), sinks_ref[h], F32) for h in heads], axis=0)
        m = jnp.maximum(jnp.max(jnp.maximum(s_p, s_c), axis=-1, keepdims=True), sink)
        p_p = jnp.exp(s_p - m)
        p_c = jnp.exp(s_c - m)
        denom = jnp.sum(p_p + p_c, axis=-1, keepdims=True) + jnp.exp(sink - m)
        o = (_dot(p_p.astype(BF16), vd_p) + _dot(p_c.astype(BF16), vd_c)) / denom
        for j in range(GQA // 2):
            att[:, pcs[j]] = jnp.where(lo_half, o[2 * j * L:(2 * j + 1) * L], o[(2 * j + 1) * L:(2 * j + 2) * L])

    for g in range(SSD_GROUPS):
        ssd_group(g)
        attn_group(g)
    kprev[...] = k_ref[...]
    vprev[...] = v_ref[...]

    y_ssd = _dot(ybuf[...].astype(BF16), wdown_ref[...])
    merged = gs_ref[...].astype(F32) * y_ssd + ga_ref[...].astype(F32) * att[...]
    h1_ref[...] = x_ref[...] + _dot(merged.astype(BF16), wmix_ref[...])

    @pl.when(c == last)
    def _():
        for g in range(SSD_GROUPS):
            pstate_ref[g] = state[g].T
        pk_ref[...] = k_ref[...].astype(F32)
        pv_ref[...] = v_ref[...].astype(F32)


def _mixer(x, zs, act, q, k, v, gs, ga, dt, dtb, alog, dskip_x, g_ssd, sinks, w_down, w_mix):
    b, seq, _ = x.shape
    nc = seq // CHUNK

    def blk(width):
        return pl.BlockSpec((None, CHUNK, width), lambda i, j: (i, j, 0))

    def per_b(*shape):
        nd = len(shape)
        return pl.BlockSpec((None,) + shape, lambda i, j: (i,) + (0,) * nd)

    in_specs = [blk(CONV_DIM), blk(D_INNER), blk(LANES), blk(ATTN_WIDTH), blk(KV_DUP), blk(KV_DUP),
                blk(D_MODEL), blk(D_MODEL), blk(D_MODEL),
                _const_spec((1, LANES)), _const_spec((1, LANES)), _const_spec((1, D_INNER)), _const_spec((1, D_INNER)),
                pl.BlockSpec(memory_space=pltpu.SMEM),
                _const_spec((D_INNER, D_MODEL)), _const_spec((D_MODEL, D_MODEL))]
    out_shape = [jax.ShapeDtypeStruct((b, seq, D_MODEL), F32),
                 jax.ShapeDtypeStruct((b, SSD_GROUPS, SSD_HPG * SSD_HEAD_DIM, SSD_STATE), F32),
                 jax.ShapeDtypeStruct((b, WINDOW, KV_DUP), F32),
                 jax.ShapeDtypeStruct((b, WINDOW, KV_DUP), F32)]
    out_specs = [blk(D_MODEL), per_b(SSD_GROUPS, SSD_HPG * SSD_HEAD_DIM, SSD_STATE),
                 per_b(WINDOW, KV_DUP), per_b(WINDOW, KV_DUP)]
    scratch = [pltpu.VMEM((CHUNK, D_INNER), F32),
               pltpu.VMEM((CHUNK, ATTN_WIDTH), F32),
               pltpu.VMEM((CHUNK, KV_DUP), BF16), pltpu.VMEM((CHUNK, KV_DUP), BF16),
               pltpu.VMEM((SSD_GROUPS, SSD_STATE, SSD_HPG * SSD_HEAD_DIM), F32)]
    return pl.pallas_call(
        _mixer_body,
        grid=(b, nc),
        in_specs=in_specs,
        out_specs=out_specs,
        out_shape=out_shape,
        scratch_shapes=scratch,
        compiler_params=_params(("parallel", "arbitrary"), 48),
        name="prompt_mixer",
    )(act, zs, dt, q, k, v, gs, ga, x, dtb, alog, dskip_x, g_ssd, sinks, w_down, w_mix)


def _memkv_body(m_ref, g_ref, w_ref, o_ref):
    o_ref[...] = _dot(_rms(m_ref[...], g_ref[...]).astype(BF16), w_ref[...])


def _memkv(mem2d, g_mem, w_ckv):
    t = mem2d.shape[0]
    tm = _tile(t, 256)
    return pl.pallas_call(
        _memkv_body,
        grid=(t // tm,),
        in_specs=[pl.BlockSpec((tm, D_MODEL), lambda i: (i, 0)), _const_spec((1, D_MODEL)),
                  _const_spec((D_MODEL, 2 * C_WIDTH))],
        out_specs=pl.BlockSpec((tm, 2 * C_WIDTH), lambda i: (i, 0)),
        out_shape=jax.ShapeDtypeStruct((t, 2 * C_WIDTH), F32),
        compiler_params=_params(("parallel",), 32),
        name="memory_kv",
    )(mem2d, g_mem, w_ckv)


def _route(logits):
    rows = logits.shape[0]
    lane = lax.broadcasted_iota(jnp.int32, (rows, ROUTE_COLS), 1).astype(F32)
    big = 1e9
    is_g = lane < N_GROUPS
    lg = jnp.where(is_g, logits, NEG_INF)
    gmax = jnp.max(lg, axis=-1, keepdims=True)
    grp = jnp.min(jnp.where(lg == gmax, lane, big), axis=-1, keepdims=True)
    p_grp = 1.0 / jnp.sum(jnp.where(is_g, jnp.exp(lg - gmax), 0.0), axis=-1, keepdims=True)
    lo = N_GROUPS + EXP_PER_GROUP * grp
    in_grp = (lane >= lo) & (lane < lo + EXP_PER_GROUP)
    le = jnp.where(in_grp, logits, NEG_INF)
    m1 = jnp.max(le, axis=-1, keepdims=True)
    i1 = jnp.min(jnp.where(le == m1, lane, big), axis=-1, keepdims=True)
    le2 = jnp.where(lane == i1, NEG_INF, le)
    m2 = jnp.max(le2, axis=-1, keepdims=True)
    i2 = jnp.min(jnp.where(le2 == m2, lane, big), axis=-1, keepdims=True)
    t2 = jnp.exp(m2 - m1)
    g1 = p_grp / (1.0 + t2)
    g2 = p_grp * t2 / (1.0 + t2)
    info = jnp.where(lane == 0, i1 - N_GROUPS,
                     jnp.where(lane == 1, i2 - N_GROUPS,
                               jnp.where(lane == 2, g1, jnp.where(lane == 3, g2, 0.0))))
    return info


def _cross_body(h1_ref, mkv_ref, gc_ref, wcq_ref, wco_ref, gm_ref, wr_ref, br_ref,
                h2_ref, xm_ref, route_ref, obuf):
    h1 = h1_ref[...]
    xn = _rms(h1, gc_ref[...]).astype(BF16)
    qc = _dot(xn, wcq_ref[...])
    scale = C_HEAD_DIM ** -0.5
    for h in range(C_HEADS):
        hs = slice(h * C_HEAD_DIM, (h + 1) * C_HEAD_DIM)
        mk = mkv_ref[:, hs].astype(BF16)
        mv = mkv_ref[:, C_WIDTH + h * C_HEAD_DIM:C_WIDTH + (h + 1) * C_HEAD_DIM].astype(BF16)
        s = _dot_nt(qc[:, hs].astype(BF16), mk) * scale
        m = jnp.max(s, axis=-1, keepdims=True)
        p = jnp.exp(s - m)
        obuf[:, hs] = _dot(p.astype(BF16), mv) / jnp.sum(p, axis=-1, keepdims=True)
    h2 = h1 + _dot(obuf[...].astype(BF16), wco_ref[...])
    h2_ref[...] = h2
    xm = _rms(h2, gm_ref[...])
    xm_ref[...] = _pack_rows(xm)
    logits = _dot(xm.astype(BF16), wr_ref[...]) + br_ref[...]
    route_ref[...] = _route(logits).T[0:SUBLANES, :]


def _cross(h1, mkv, g_cross, w_cq, w_co, g_moe, w_r, b_r):
    b, seq, _ = h1.shape
    tq = _tile(seq, 1024)
    nq = seq // tq
    return pl.pallas_call(
        _cross_body,
        grid=(b, nq),
        in_specs=[pl.BlockSpec((None, tq, D_MODEL), lambda i, j: (i, j, 0)),
                  pl.BlockSpec((None, MEM_LEN, 2 * C_WIDTH), lambda i, j: (i, 0, 0)),
                  _const_spec((1, D_MODEL)), _const_spec((D_MODEL, C_WIDTH)), _const_spec((C_WIDTH, D_MODEL)),
                  _const_spec((1, D_MODEL)), _const_spec((D_MODEL, ROUTE_COLS)), _const_spec((1, ROUTE_COLS))],
        out_specs=[pl.BlockSpec((None, tq, D_MODEL), lambda i, j: (i, j, 0)),
                   pl.BlockSpec((None, tq, PACK_W), lambda i, j: (i, j, 0)),
                   pl.BlockSpec((SUBLANES, tq), lambda i, j: (0, i * nq + j))],
        out_shape=[jax.ShapeDtypeStruct((b, seq, D_MODEL), F32),
                   jax.ShapeDtypeStruct((b, seq, PACK_W), jnp.uint32),
                   jax.ShapeDtypeStruct((SUBLANES, b * seq), F32)],
        scratch_shapes=[pltpu.VMEM((tq, C_WIDTH), F32)],
        compiler_params=_params(("parallel", "parallel"), 48),
        name="prompt_cross_route",
    )(h1, mkv, g_cross, w_cq, w_co, g_moe, w_r, b_r)


RANK_TILE = 512


def _rank_body(route_ref, rank_ref, count_ref, carry):
    i = pl.program_id(0)

    @pl.when(i == 0)
    def _():
        carry[...] = jnp.zeros_like(carry)

    e1 = route_ref[0:1, :]
    e2 = route_ref[1:2, :]
    eid = lax.broadcasted_iota(jnp.int32, (N_EXPERTS, RANK_TILE), 0).astype(F32)
    is1 = e1 == eid
    is2 = e2 == eid
    onehot = jnp.where(is1 | is2, 1.0, 0.0)
    si = lax.broadcasted_iota(jnp.int32, (RANK_TILE, RANK_TILE), 0)
    ti = lax.broadcasted_iota(jnp.int32, (RANK_TILE, RANK_TILE), 1)
    before = jnp.where(si < ti, 1.0, 0.0).astype(BF16)
    prefix = _dot(onehot.astype(BF16), before) + carry[:, 0:1]
    r1 = jnp.sum(jnp.where(is1, prefix, 0.0), axis=0, keepdims=True)
    r2 = jnp.sum(jnp.where(is2, prefix, 0.0), axis=0, keepdims=True)
    row = lax.broadcasted_iota(jnp.int32, (SUBLANES, RANK_TILE), 0)
    rank_ref[...] = jnp.where(row == 0, r1, jnp.where(row == 1, r2, 0.0))
    carry[...] = carry[...] + jnp.sum(onehot, axis=1, keepdims=True)
    count_ref[...] = carry[...]


def _rank(route):
    tp = route.shape[1]
    return pl.pallas_call(
        _rank_body,
        grid=(tp // RANK_TILE,),
        in_specs=[pl.BlockSpec((SUBLANES, RANK_TILE), lambda i: (0, i))],
        out_specs=[pl.BlockSpec((SUBLANES, RANK_TILE), lambda i: (0, i)),
                   pl.BlockSpec((N_EXPERTS, LANES), lambda i: (0, 0))],
        out_shape=[jax.ShapeDtypeStruct((SUBLANES, tp), F32), jax.ShapeDtypeStruct((N_EXPERTS, LANES), F32)],
        scratch_shapes=[pltpu.VMEM((N_EXPERTS, LANES), F32)],
        compiler_params=_params(("arbitrary",), 32),
        name="moe_rank",
    )(route)


def _dest_body(route_ref, rank_ref, offs_ref, dest_ref):
    e1 = route_ref[0:1, :]
    e2 = route_ref[1:2, :]
    eid = lax.broadcasted_iota(jnp.int32, (N_EXPERTS, RANK_TILE), 0).astype(F32)
    offs = offs_ref[:, 0:1]
    d1 = jnp.sum(jnp.where(e1 == eid, offs, 0.0), axis=0, keepdims=True) + rank_ref[0:1, :]
    d2 = jnp.sum(jnp.where(e2 == eid, offs, 0.0), axis=0, keepdims=True) + rank_ref[1:2, :]
    row = lax.broadcasted_iota(jnp.int32, (SUBLANES, RANK_TILE), 0)
    dest_ref[...] = jnp.where(row == 0, d1, jnp.where(row == 1, d2, 0.0)).astype(jnp.int32)


def _dest(route, rank, offs):
    tp = route.shape[1]
    return pl.pallas_call(
        _dest_body,
        grid=(tp // RANK_TILE,),
        in_specs=[pl.BlockSpec((SUBLANES, RANK_TILE), lambda i: (0, i)),
                  pl.BlockSpec((SUBLANES, RANK_TILE), lambda i: (0, i)),
                  pl.BlockSpec((N_EXPERTS, LANES), lambda i: (0, 0))],
        out_specs=pl.BlockSpec((SUBLANES, RANK_TILE), lambda i: (0, i)),
        out_shape=jax.ShapeDtypeStruct((SUBLANES, tp), jnp.int32),
        compiler_params=_params(("parallel",), 32),
        name="moe_dest",
    )(route, rank, offs)


DMA_UNROLL = 8


def _row_copy(src, dst, s_row, d_row, sem):
    return pltpu.make_async_copy(src.at[pl.ds(s_row, 1)], dst.at[pl.ds(d_row, 1)], sem)


def _dispatch_body(td, dest_ref, xm_ref, xb_in_ref, xb_ref, sem):
    del xb_in_ref
    base = pl.program_id(0) * td

    def issue(g8, carry):
        t0 = pl.multiple_of(g8 * DMA_UNROLL, DMA_UNROLL)
        for k in range(DMA_UNROLL):
            row = base + t0 + k
            _row_copy(xm_ref, xb_ref, t0 + k, dest_ref[2 * row], sem).start()
            _row_copy(xm_ref, xb_ref, t0 + k, dest_ref[2 * row + 1], sem).start()
        return carry

    lax.fori_loop(0, td // DMA_UNROLL, issue, 0)

    def drain(g8, carry):
        for _ in range(2 * DMA_UNROLL):
            _row_copy(xm_ref, xb_ref, 0, 0, sem).wait()
        return carry

    lax.fori_loop(0, td // DMA_UNROLL, drain, 0)


def _dispatch(dest_flat, xm2d, xb):
    t = xm2d.shape[0]
    td = _tile(t, 512)
    return pl.pallas_call(
        functools.partial(_dispatch_body, td),
        grid_spec=pltpu.PrefetchScalarGridSpec(
            num_scalar_prefetch=1,
            grid=(t // td,),
            in_specs=[pl.BlockSpec((td, PACK_W), lambda i, d: (i, 0)), pl.BlockSpec(memory_space=pl.ANY)],
            out_specs=pl.BlockSpec(memory_space=pl.ANY),
            scratch_shapes=[pltpu.SemaphoreType.DMA],
        ),
        out_shape=jax.ShapeDtypeStruct(xb.shape, xb.dtype),
        input_output_aliases={2: 0},
        compiler_params=_params(("arbitrary",), 32),
        name="moe_dispatch",
    )(dest_flat, xm2d, xb)


def _expert_body(be_ref, nused_ref, xb_ref, wg_ref, wu_ref, wd_ref, yb_ref, wg_bf, wu_bf, wd_bf):
    i = pl.program_id(0)

    @pl.when(jnp.logical_or(i == 0, be_ref[i] != be_ref[jnp.maximum(i - 1, 0)]))
    def _():
        wg_bf[...] = wg_ref[...].astype(BF16)
        wu_bf[...] = wu_ref[...].astype(BF16)
        wd_bf[...] = wd_ref[...].astype(BF16)

    @pl.when(i < nused_ref[0])
    def _():
        x = _unpack_rows(xb_ref[...], BF16)
        hmid = _silu(_dot(x, wg_bf[...])) * _dot(x, wu_bf[...])
        yb_ref[...] = _pack_rows(_dot(hmid.astype(BF16), wd_bf[...]))

    @pl.when(i >= nused_ref[0])
    def _():
        yb_ref[...] = jnp.zeros_like(yb_ref)


def _experts(block_e, n_used, xb, wg, wu, wd):
    rows = xb.shape[0]
    nb = rows // MOE_ROWS

    def xmap(i, be, nu):
        return (jnp.minimum(i, nu[0] - 1), 0)

    def wmap(i, be, nu):
        return (be[i], 0, 0)

    return pl.pallas_call(
        _expert_body,
        grid_spec=pltpu.PrefetchScalarGridSpec(
            num_scalar_prefetch=2,
            grid=(nb,),
            in_specs=[pl.BlockSpec((MOE_ROWS, PACK_W), xmap),
                      pl.BlockSpec((None, D_MODEL, D_EXPERT), wmap),
                      pl.BlockSpec((None, D_MODEL, D_EXPERT), wmap),
                      pl.BlockSpec((None, D_EXPERT, D_MODEL), wmap)],
            out_specs=pl.BlockSpec((MOE_ROWS, PACK_W), lambda i, be, nu: (i, 0)),
            scratch_shapes=[pltpu.VMEM((D_MODEL, D_EXPERT), BF16), pltpu.VMEM((D_MODEL, D_EXPERT), BF16),
                            pltpu.VMEM((D_EXPERT, D_MODEL), BF16)],
        ),
        out_shape=jax.ShapeDtypeStruct((rows, PACK_W), jnp.uint32),
        compiler_params=_params(("arbitrary",), 48),
        name="moe_experts",
    )(block_e, n_used, xb, wg, wu, wd)


def _combine_body(tc, dest_ref, yb_ref, h2_ref, gate_ref, gf_ref, y_ref, buf, sem):
    i = pl.program_id(0)

    def issue(step, slot):
        base = step * tc

        def grp(g8, carry):
            t0 = pl.multiple_of(g8 * DMA_UNROLL, DMA_UNROLL)
            for k in range(DMA_UNROLL):
                row = base + t0 + k
                for e in range(2):
                    pltpu.make_async_copy(yb_ref.at[pl.ds(dest_ref[2 * row + e], 1)],
                                          buf.at[slot, e, pl.ds(t0 + k, 1)], sem.at[slot]).start()
            return carry

        lax.fori_loop(0, tc // DMA_UNROLL, grp, 0)

    @pl.when(i == 0)
    def _():
        issue(0, 0)

    @pl.when(i + 1 < pl.num_programs(0))
    def _():
        issue(i + 1, (i + 1) % 2)

    slot = i % 2

    def drain(g8, carry):
        for _ in range(2 * DMA_UNROLL):
            pltpu.make_async_copy(yb_ref.at[pl.ds(0, 1)], buf.at[slot, 0, pl.ds(0, 1)], sem.at[slot]).wait()
        return carry

    lax.fori_loop(0, tc // DMA_UNROLL, drain, 0)
    g = gate_ref[...]
    out = h2_ref[...] + (g[:, 0:1] * _unpack_rows(buf[slot, 0], F32) + g[:, 1:2] * _unpack_rows(buf[slot, 1], F32))
    y_ref[...] = _rms(out, gf_ref[...])


def _combine(dest_flat, yb, h2_2d, gates, g_final):
    t = h2_2d.shape[0]
    tc = _tile(t, 512)
    return pl.pallas_call(
        functools.partial(_combine_body, tc),
        grid_spec=pltpu.PrefetchScalarGridSpec(
            num_scalar_prefetch=1,
            grid=(t // tc,),
            in_specs=[pl.BlockSpec(memory_space=pl.ANY),
                      pl.BlockSpec((tc, D_MODEL), lambda i, d: (i, 0)),
                      pl.BlockSpec((tc, 2), lambda i, d: (i, 0)),
                      pl.BlockSpec((1, D_MODEL), lambda i, d: (0, 0))],
            out_specs=pl.BlockSpec((tc, D_MODEL), lambda i, d: (i, 0)),
            scratch_shapes=[pltpu.VMEM((2, 2, tc, PACK_W), jnp.uint32), pltpu.SemaphoreType.DMA((2,))],
        ),
        out_shape=jax.ShapeDtypeStruct((t, D_MODEL), F32),
        compiler_params=_params(("arbitrary",), 32),
        name="moe_combine",
    )(dest_flat, yb, h2_2d, gates, g_final)


def _s_inproj_body(x_ref, g_ref, w_ref, o_ref):
    o_ref[...] = _dot(_rms(x_ref[...], g_ref[...]).astype(BF16), w_ref[...])


def _s_inproj(x, g_mix, w_packed_f32):
    n = x.shape[0]
    tn = PACKED_COLS // 2
    return pl.pallas_call(
        _s_inproj_body,
        grid=(PACKED_COLS // tn,),
        in_specs=[pl.BlockSpec((n, D_MODEL), lambda j: (0, 0)), pl.BlockSpec((1, D_MODEL), lambda j: (0, 0)),
                  pl.BlockSpec((D_MODEL, tn), lambda j: (0, j))],
        out_specs=pl.BlockSpec((n, tn), lambda j: (0, j)),
        out_shape=jax.ShapeDtypeStruct((n, PACKED_COLS), F32),
        compiler_params=_params(("parallel",), 48),
        name="sample_inproj",
    )(x, g_mix, w_packed_f32)


def _s_conv_body(proj_ref, cs_ref, convw_ref, convb_ref, dtb_ref, alog_ref, xexp_ref,
                 act_ref, ncs_ref, dtx_ref, e_ref):
    step = 512
    for c0 in range(0, CONV_DIM, step):
        cs = slice(c0, c0 + step)
        s0 = cs_ref[:, c0:c0 + step]
        s1 = cs_ref[:, CONV_DIM + c0:CONV_DIM + c0 + step]
        s2 = cs_ref[:, 2 * CONV_DIM + c0:2 * CONV_DIM + c0 + step]
        xn = proj_ref[:, SEG_XBC + c0:SEG_XBC + c0 + step]
        acc = convb_ref[:, cs] + convw_ref[0:1, cs] * s0
        acc = acc + convw_ref[1:2, cs] * s1
        acc = acc + convw_ref[2:3, cs] * s2
        acc = acc + convw_ref[3:4, cs] * xn
        act_ref[:, cs] = _silu(acc)
        ncs_ref[:, c0:c0 + step] = s1
        ncs_ref[:, CONV_DIM + c0:CONV_DIM + c0 + step] = s2
        ncs_ref[:, 2 * CONV_DIM + c0:2 * CONV_DIM + c0 + step] = xn
    dt = _softplus(proj_ref[:, SEG_DT:SEG_DT + LANES] + dtb_ref[...])
    e_ref[...] = jnp.exp(dt * (-jnp.exp(alog_ref[...])))
    dtx_ref[...] = _dot(dt, xexp_ref[...], precision=HIGHEST) * act_ref[:, 0:D_INNER]


def _s_conv(proj, conv_state2d, conv_w, conv_b, dtb, alog, xexp):
    n = proj.shape[0]
    return pl.pallas_call(
        _s_conv_body,
        out_shape=[jax.ShapeDtypeStruct((n, CONV_DIM), F32), jax.ShapeDtypeStruct((n, 3 * CONV_DIM), F32),
                   jax.ShapeDtypeStruct((n, D_INNER), F32), jax.ShapeDtypeStruct((n, LANES), F32)],
        compiler_params=pltpu.CompilerParams(vmem_limit_bytes=48 * 1024 * 1024),
        name="sample_conv",
    )(proj, conv_state2d, conv_w, conv_b, dtb, alog, xexp)


def _s_state_body(bb, e_ref, st_ref, dtxt_ref, b_ref, c_ref, so_ref, yt_ref):
    base = pl.program_id(0) * bb
    lane = lax.broadcasted_iota(jnp.int32, (SSD_HEAD_DIM, LANES), 1)

    for bl in range(bb):
        for h in range(SSD_HEADS):
            g = h // SSD_HPG
            brow = b_ref[bl, :, g * SSD_STATE:(g + 1) * SSD_STATE]
            xcol = dtxt_ref[bl, :, h:h + 1]
            so_ref[bl, h] = st_ref[bl, h] * e_ref[base + bl, h] + xcol * brow
        yacc = jnp.zeros((SSD_HEAD_DIM, LANES), F32)
        for h in range(SSD_HEADS):
            g = h // SSD_HPG
            crow = c_ref[bl, :, g * SSD_STATE:(g + 1) * SSD_STATE]
            yacc = jnp.where(lane == h, jnp.sum(so_ref[bl, h] * crow, axis=-1, keepdims=True), yacc)
        yt_ref[bl] = yacc


def _s_state(e, state, dtxt, bmat, cmat):
    n = state.shape[0]
    bb = _tile(n, 2, 1)
    sblk = (bb, SSD_HEADS, SSD_HEAD_DIM, SSD_STATE)
    bmat = bmat.reshape(n, 1, SSD_GROUPS * SSD_STATE)
    cmat = cmat.reshape(n, 1, SSD_GROUPS * SSD_STATE)
    return pl.pallas_call(
        functools.partial(_s_state_body, bb),
        grid=(n // bb,),
        in_specs=[pl.BlockSpec(memory_space=pltpu.SMEM),
                  pl.BlockSpec(sblk, lambda i: (i, 0, 0, 0)),
                  pl.BlockSpec((bb, SSD_HEAD_DIM, SSD_HEADS), lambda i: (i, 0, 0)),
                  pl.BlockSpec((bb, 1, SSD_GROUPS * SSD_STATE), lambda i: (i, 0, 0)),
                  pl.BlockSpec((bb, 1, SSD_GROUPS * SSD_STATE), lambda i: (i, 0, 0))],
        out_specs=[pl.BlockSpec(sblk, lambda i: (i, 0, 0, 0)),
                   pl.BlockSpec((bb, SSD_HEAD_DIM, LANES), lambda i: (i, 0, 0))],
        out_shape=[jax.ShapeDtypeStruct(state.shape, F32), jax.ShapeDtypeStruct((n, SSD_HEAD_DIM, LANES), F32)],
        compiler_params=_params(("parallel",), 40),
        name="sample_ssd_state",
    )(e, state, dtxt, bmat, cmat)


def _s_attn_body(bb, qexp_ref, ck_ref, cv_ref, kn_ref, vn_ref, sink_ref, bias_ref,
                 y_ref, ok_ref, ov_ref):
    W = WINDOW
    bias = bias_ref[...]
    sink = sink_ref[...]
    for bl in range(bb):
        kn = kn_ref[bl]
        vn = vn_ref[bl]
        qe = qexp_ref[bl].astype(BF16)
        s = _dot_nt(qe, ck_ref[bl].astype(BF16)) * (HEAD_DIM ** -0.5) - bias
        sn = jnp.sum(qe.astype(F32) * _bf16_round(kn), axis=-1, keepdims=True) * (HEAD_DIM ** -0.5)
        m = jnp.maximum(jnp.maximum(jnp.max(s, axis=-1, keepdims=True), sn), sink)
        p = jnp.exp(s - m)
        pn = jnp.exp(sn - m)
        denom = jnp.sum(p, axis=-1, keepdims=True) + pn + jnp.exp(sink - m)
        o = _dot((p / denom).astype(BF16), cv_ref[bl].astype(BF16)) + _bf16_round(pn / denom) * _bf16_round(vn)
        for h in range(N_HEADS):
            kv = h // GQA
            y_ref[bl, :, h * HEAD_DIM:(h + 1) * HEAD_DIM] = o[h:h + 1, kv * HEAD_DIM:(kv + 1) * HEAD_DIM]
        ok_ref[bl, 0:W - 1, :] = ck_ref[bl, 1:W, :]
        ok_ref[bl, W - 1:W, :] = kn
        ov_ref[bl, 0:W - 1, :] = cv_ref[bl, 1:W, :]
        ov_ref[bl, W - 1:W, :] = vn


def _s_attn(qexp, ck, cv, kn, vn, sink_col, bias_tile):
    n = ck.shape[0]
    bb = _tile(n, 8, 1)
    cblk = pl.BlockSpec((bb, WINDOW, KV_WIDTH), lambda i: (i, 0, 0))
    rblk = pl.BlockSpec((bb, 1, KV_WIDTH), lambda i: (i, 0, 0))
    y, ok, ov = pl.pallas_call(
        functools.partial(_s_attn_body, bb),
        grid=(n // bb,),
        in_specs=[pl.BlockSpec((bb, N_HEADS, KV_WIDTH), lambda i: (i, 0, 0)), cblk, cblk, rblk, rblk,
                  pl.BlockSpec((N_HEADS, 1), lambda i: (0, 0)), pl.BlockSpec((N_HEADS, WINDOW), lambda i: (0, 0))],
        out_specs=[pl.BlockSpec((bb, 1, ATTN_WIDTH), lambda i: (i, 0, 0)), cblk, cblk],
        out_shape=[jax.ShapeDtypeStruct((n, 1, ATTN_WIDTH), F32), jax.ShapeDtypeStruct(ck.shape, F32),
                   jax.ShapeDtypeStruct(cv.shape, F32)],
        compiler_params=_params(("parallel",), 32),
        name="sample_window_attn",
    )(qexp, ck, cv, kn.reshape(n, 1, KV_WIDTH), vn.reshape(n, 1, KV_WIDTH), sink_col, bias_tile)
    return y.reshape(n, ATTN_WIDTH), ok, ov


def _s_post_body(y_ref, act_ref, proj_ref, att_ref, x_ref, dskip_ref, gssd_ref, wdown_ref, wmix_ref,
                 gc_ref, wcq_ref, h1_ref, qc_ref, ybuf):
    gw = D_INNER // SSD_GROUPS
    for g in range(SSD_GROUPS):
        gs_ = slice(g * gw, (g + 1) * gw)
        yg = (y_ref[:, gs_] + dskip_ref[:, gs_] * act_ref[:, gs_]) * _silu(proj_ref[:, SEG_Z + g * gw:SEG_Z + (g + 1) * gw])
        ybuf[:, gs_] = _rms(yg, gssd_ref[:, gs_])
    y_ssd = _dot(ybuf[...].astype(BF16), wdown_ref[...])
    merged = (_sigmoid(proj_ref[:, SEG_GS:SEG_GS + D_MODEL]) * y_ssd
              + _sigmoid(proj_ref[:, SEG_GA:SEG_GA + D_MODEL]) * att_ref[...])
    h1 = x_ref[...] + _dot(merged.astype(BF16), wmix_ref[...])
    h1_ref[...] = h1
    qc_ref[...] = _bf16_round(_dot(_rms(h1, gc_ref[...]).astype(BF16), wcq_ref[...]))


def _s_post(y, act, proj, att, x, dskip_x, g_ssd, w_down, w_mix, g_cross, w_cq):
    n = x.shape[0]
    return pl.pallas_call(
        _s_post_body,
        out_shape=[jax.ShapeDtypeStruct((n, D_MODEL), F32), jax.ShapeDtypeStruct((n, C_WIDTH), F32)],
        scratch_shapes=[pltpu.VMEM((n, D_INNER), F32)],
        compiler_params=pltpu.CompilerParams(vmem_limit_bytes=48 * 1024 * 1024),
        name="sample_post_mixer",
    )(y, act, proj, att, x, dskip_x, g_ssd, w_down, w_mix, g_cross, w_cq)


def _s_cross_body(bb, qc_ref, mk_ref, mv_ref, o_ref):
    scale = C_HEAD_DIM ** -0.5
    for bl in range(bb):
        q = qc_ref[bl]
        s = jnp.sum(_bf16_round(mk_ref[bl]) * q[None], axis=-1, keepdims=True) * scale
        m = jnp.max(s, axis=0, keepdims=True)
        p = jnp.exp(s - m)
        p = _bf16_round(p / jnp.sum(p, axis=0, keepdims=True))
        o_ref[bl] = jnp.sum(p * _bf16_round(mv_ref[bl]), axis=0)


def _s_cross(qc, mk, mv):
    n = qc.shape[0]
    bb = _tile(n, 4, 1)
    mblk = pl.BlockSpec((bb, MEM_LEN, C_HEADS, C_HEAD_DIM), lambda i: (i, 0, 0, 0))
    rblk = pl.BlockSpec((bb, C_HEADS, C_HEAD_DIM), lambda i: (i, 0, 0))
    return pl.pallas_call(
        functools.partial(_s_cross_body, bb),
        grid=(n // bb,),
        in_specs=[rblk, mblk, mblk],
        out_specs=rblk,
        out_shape=jax.ShapeDtypeStruct((n, C_HEADS, C_HEAD_DIM), F32),
        compiler_params=_params(("parallel",), 40),
        name="sample_cross_attn",
    )(qc.reshape(n, C_HEADS, C_HEAD_DIM), mk, mv).reshape(n, C_WIDTH)


def _s_route_body(o_ref, h1_ref, wco_ref, gm_ref, wr_ref, br_ref, h2_ref, xm_ref, route_ref):
    h2 = h1_ref[...] + _dot(o_ref[...].astype(BF16), wco_ref[...])
    h2_ref[...] = h2
    xm = _rms(h2, gm_ref[...])
    xm_ref[...] = _pack_rows(xm)
    logits = _dot(xm.astype(BF16), wr_ref[...]) + br_ref[...]
    route_ref[...] = _route(logits).T[0:SUBLANES, :]


def _s_route(o, h1, w_co, g_moe, w_r, b_r):
    n = o.shape[0]
    return pl.pallas_call(
        _s_route_body,
        out_shape=[jax.ShapeDtypeStruct((n, D_MODEL), F32), jax.ShapeDtypeStruct((n, PACK_W), jnp.uint32),
                   jax.ShapeDtypeStruct((SUBLANES, n), F32)],
        compiler_params=pltpu.CompilerParams(vmem_limit_bytes=32 * 1024 * 1024),
        name="sample_cross_out_route",
    )(o, h1, w_co, g_moe, w_r, b_r)


def _pack_in_weights(w_in):
    cuts = np.cumsum((D_INNER, CONV_DIM, SSD_HEADS, ATTN_WIDTH, KV_WIDTH, KV_WIDTH, D_MODEL, D_MODEL))[:-1]
    z, xbc, dt, q, k, v, gs, ga = jnp.split(w_in, [int(c) for c in cuts], axis=1)
    dt = jnp.pad(dt, ((0, 0), (0, PACKED_COLS - SEG_DT - SSD_HEADS)))

    def dup(w):
        w = w.reshape(w.shape[0], N_KV, 1, HEAD_DIM)
        return jnp.broadcast_to(w, (w.shape[0], N_KV, 2, HEAD_DIM)).reshape(w.shape[0], KV_DUP)

    return jnp.concatenate([z, xbc, q, dup(k), dup(v), gs, ga, dt], axis=1)


def _undup(x):
    lead = x.shape[:-1]
    return x.reshape(lead + (N_KV, 2, HEAD_DIM))[..., 0, :].reshape(lead + (KV_WIDTH,))


def _head_expand_matrix():
    m = np.zeros((LANES, D_INNER), np.float32)
    for h in range(SSD_HEADS):
        m[h, h * SSD_HEAD_DIM:(h + 1) * SSD_HEAD_DIM] = 1.0
    return m


def _row(v, width=None):
    v = v.reshape(1, -1)
    if width is not None and v.shape[1] < width:
        v = jnp.pad(v, ((0, 0), (0, width - v.shape[1])))
    return v


def kernel(x_prompt, x_sample, state_ssd, state_conv, cache_win_k, cache_win_v, cache_mem_k, cache_mem_v, mem_prompt, g_mix, w_in, conv_w, conv_b, dt_bias, a_log, d_skip, g_ssd, w_ssd_down, attn_sinks, w_mix_out, g_cross, g_mem, w_cq, w_ckv, w_co, g_moe, w_route_group, b_route_group, w_route_expert, b_route_expert, w_e_gate, w_e_up, w_e_down, g_final):
    assert g_mix.shape[0] == 1, "single layer"
    b, seq, _ = x_prompt.shape
    n_s = x_sample.shape[0]
    t_p = b * seq
    assert seq % CHUNK == 0 and x_sample.shape[1] == 1

    w_packed_bf = _pack_in_weights(w_in[0].astype(BF16))
    w_down_bf, w_mix_bf = w_ssd_down[0].astype(BF16), w_mix_out[0].astype(BF16)
    w_cq_bf, w_co_bf = w_cq[0].astype(BF16), w_co[0].astype(BF16)
    g_mix_r, g_cross_r, g_mem_r, g_moe_r, g_final_r = (_row(g_mix[0]), _row(g_cross[0]), _row(g_mem[0]),
                                                        _row(g_moe[0]), _row(g_final))
    conv_b_r = _row(conv_b[0])
    dtb_r = _row(dt_bias[0], LANES)
    alog_r = _row(a_log[0], LANES)
    dskip_x = _row(jnp.repeat(d_skip[0], SSD_HEAD_DIM))
    g_ssd_r = _row(g_ssd[0])
    w_r = jnp.pad(jnp.concatenate([w_route_group[0], w_route_expert[0]], axis=1),
                  ((0, 0), (0, ROUTE_COLS - N_GROUPS - N_EXPERTS))).astype(BF16)
    b_r = _row(jnp.concatenate([b_route_group[0], b_route_expert[0]]), ROUTE_COLS)

    zs, act, q, k, v, gs, ga, dt, p_conv8 = _inproj(x_prompt.reshape(t_p, D_MODEL), g_mix_r, w_packed_bf,
                                                    conv_w[0], conv_b_r, seq)
    p_conv = p_conv8[:, SUBLANES - (CONV_K - 1):, :]
    r3 = lambda a: a.reshape(b, seq, a.shape[-1])
    h1, p_state, p_wk, p_wv = _mixer(
        x_prompt, r3(zs), r3(act), r3(q), r3(k), r3(v), r3(gs), r3(ga), r3(dt),
        dtb_r, alog_r, dskip_x, g_ssd_r, attn_sinks[0], w_down_bf, w_mix_bf)
    mkv = _memkv(mem_prompt.reshape(b * MEM_LEN, D_MODEL), g_mem_r, w_ckv[0].astype(BF16))
    h2_p, xm_p, route_p = _cross(h1, mkv.reshape(b, MEM_LEN, 2 * C_WIDTH), g_cross_r,
                                 w_cq_bf, w_co_bf, g_moe_r, w_r, b_r)

    xs2 = x_sample.reshape(n_s, D_MODEL)
    proj = _s_inproj(xs2, g_mix_r, w_packed_bf)
    xexp = jnp.asarray(_head_expand_matrix())
    act_s, s_conv, dtx, e_s = _s_conv(proj, state_conv[0].reshape(n_s, 3 * CONV_DIM), conv_w[0], conv_b_r,
                                      dtb_r, alog_r, xexp)
    dtxt = dtx.reshape(n_s, SSD_HEADS, SSD_HEAD_DIM).transpose(0, 2, 1)
    s_state, yt = _s_state(e_s, state_ssd[0], dtxt, act_s[:, D_INNER:D_INNER + SSD_GROUPS * SSD_STATE],
                           act_s[:, D_INNER + SSD_GROUPS * SSD_STATE:])
    y_s = yt[:, :, :SSD_HEADS].transpose(0, 2, 1).reshape(n_s, D_INNER)
    bias_tile = (jnp.asarray(ALIBI_SLOPES, F32)[:, None]
                 * (WINDOW - jnp.arange(WINDOW, dtype=jnp.int32)).astype(F32)[None, :])
    q_s = proj[:, SEG_Q:SEG_Q + ATTN_WIDTH].reshape(n_s, N_KV, GQA, 1, HEAD_DIM)
    kv_eye = jnp.eye(N_KV, dtype=F32).reshape(1, N_KV, 1, N_KV, 1)
    qexp = (q_s * kv_eye).reshape(n_s, N_HEADS, KV_WIDTH)
    att_s, s_wk, s_wv = _s_attn(qexp, cache_win_k[0].reshape(n_s, WINDOW, KV_WIDTH),
                                cache_win_v[0].reshape(n_s, WINDOW, KV_WIDTH),
                                _undup(proj[:, SEG_K:SEG_K + KV_DUP]), _undup(proj[:, SEG_V:SEG_V + KV_DUP]),
                                attn_sinks[0].reshape(N_HEADS, 1), bias_tile)
    h1_s, qc_s = _s_post(y_s, act_s, proj, att_s, xs2, dskip_x, g_ssd_r, w_down_bf, w_mix_bf, g_cross_r, w_cq_bf)
    o_s = _s_cross(qc_s, cache_mem_k[0], cache_mem_v[0])
    h2_s, xm_s, route_s = _s_route(o_s, h1_s, w_co_bf, g_moe_r, w_r, b_r)

    t_all = t_p + n_s
    t_pad = -(-t_all // RANK_TILE) * RANK_TILE
    route_all = jnp.concatenate([route_p, route_s, jnp.full((SUBLANES, t_pad - t_all), -1.0, F32)], axis=1)
    rank, counts = _rank(route_all)
    cnt = counts[:, 0].astype(jnp.int32)
    padded = (cnt + MOE_ROWS - 1) // MOE_ROWS * MOE_ROWS
    pad_end = jnp.cumsum(padded)
    offs = (pad_end - padded).astype(F32)
    nb = -(-(2 * t_all) // MOE_ROWS) + N_EXPERTS
    block_start = jnp.arange(nb, dtype=jnp.int32) * MOE_ROWS
    block_e = jnp.minimum(jnp.sum((pad_end[None, :] <= block_start[:, None]).astype(jnp.int32), axis=1),
                          N_EXPERTS - 1)
    n_used = (pad_end[-1] // MOE_ROWS).astype(jnp.int32).reshape(1)
    dest = _dest(route_all, rank, jnp.broadcast_to(offs[:, None], (N_EXPERTS, LANES)))
    dest_p = dest[0:2, :t_p].T.reshape(-1)
    dest_s = dest[0:2, t_p:t_all].T.reshape(-1)
    xb = jnp.zeros((nb * MOE_ROWS, PACK_W), jnp.uint32)
    xb = _dispatch(dest_p, xm_p.reshape(t_p, PACK_W), xb)
    xb = _dispatch(dest_s, xm_s, xb)
    yb = _experts(block_e, n_used, xb, w_e_gate[0], w_e_up[0], w_e_down[0])
    y_p = _combine(dest_p, yb, h2_p.reshape(t_p, D_MODEL), route_p[2:4, :].T, g_final_r)
    y_smp = _combine(dest_s, yb, h2_s, route_s[2:4, :].T, g_final_r)

    return (y_p.reshape(b, seq, D_MODEL), y_smp.reshape(n_s, 1, D_MODEL),
            p_state.reshape(1, b, SSD_HEADS, SSD_HEAD_DIM, SSD_STATE), p_conv[None],
            _undup(p_wk).reshape(1, b, WINDOW, N_KV, HEAD_DIM), _undup(p_wv).reshape(1, b, WINDOW, N_KV, HEAD_DIM),
            mkv[:, :C_WIDTH].reshape(1, b, MEM_LEN, C_HEADS, C_HEAD_DIM),
            mkv[:, C_WIDTH:].reshape(1, b, MEM_LEN, C_HEADS, C_HEAD_DIM),
            s_state[None], s_conv.reshape(1, n_s, CONV_K - 1, CONV_DIM),
            s_wk.reshape(1, n_s, WINDOW, N_KV, HEAD_DIM), s_wv.reshape(1, n_s, WINDOW, N_KV, HEAD_DIM))
```
